```python
import math
import jax, jax.numpy as jnp
from jax import lax
import numpy as np

D_MODEL = 2048
BATCH = 4
SEQ = 2048
DEPTH = 2

HEAD_DIM = 128
SB_HEADS = 6
SB_WIDTH = SB_HEADS * HEAD_DIM
SSM_CH_PER_GROUP = 16
SSM_GROUPS = 32
SSM_WIDTH = SSM_GROUPS * SSM_CH_PER_GROUP
SSM_STATE = 64
DIFF_HEADS = 6
DIFF_QK_DIM = HEAD_DIM // 2
DIFF_QK_WIDTH = DIFF_HEADS * 2 * DIFF_QK_DIM
DIFF_WIDTH = DIFF_HEADS * HEAD_DIM
MIX_WIDTH = SB_WIDTH + SSM_WIDTH + DIFF_WIDTH
IN_SPLIT_SIZES = (SB_WIDTH, SB_WIDTH, SB_WIDTH, SSM_WIDTH, DIFF_QK_WIDTH, DIFF_QK_WIDTH, DIFF_WIDTH)
IN_COLS = 3 * SB_WIDTH + SSM_WIDTH + 2 * DIFF_QK_WIDTH + DIFF_WIDTH
D_FF = 5632
FFN_RESIDUAL_WEIGHT = 0.5
NUM_BUCKETS = 32
MAX_DISTANCE = 128
Q_BLOCK = 128
NORM_EPS = 1e-6
DT_MIN = 1e-3
DT_MAX = 1e-1

kernel_name = "hybrid_sb_s5_diffattn_macaron"


def rms_norm(x, g):
    xf = x.astype(jnp.float32)
    y = xf * lax.rsqrt(jnp.mean(xf * xf, axis=-1, keepdims=True) + NORM_EPS)
    return (y * g.astype(jnp.float32)).astype(x.dtype)


def swiglu(x, w_gate, w_up, w_down):
    return (jax.nn.silu(x @ w_gate) * (x @ w_up)) @ w_down


def split_heads(t, n_heads, d):
    b, s, _ = t.shape
    return t.reshape(b, s, n_heads, d).transpose(0, 2, 1, 3)


def to_blocks(t):
    b, h, s, d = t.shape
    return jnp.moveaxis(t.reshape(b, h, s // Q_BLOCK, Q_BLOCK, d), 2, 0)


def from_blocks(t):
    nb, b, h, q, d = t.shape
    return jnp.transpose(t, (1, 0, 3, 2, 4)).reshape(b, nb * q, h * d)


def t5_bucket(q_pos, k_pos):
    n = jnp.maximum(q_pos[:, None] - k_pos[None, :], 0)
    max_exact = NUM_BUCKETS // 2
    nf = jnp.maximum(n, 1).astype(jnp.float32)
    large = max_exact + (jnp.log(nf / max_exact) / math.log(MAX_DISTANCE / max_exact)
                         * (NUM_BUCKETS - max_exact)).astype(jnp.int32)
    large = jnp.minimum(large, NUM_BUCKETS - 1)
    return jnp.where(n < max_exact, n, large)


def stick_breaking_attention(q, k, v):
    s = q.shape[2]
    scale = 1.0 / math.sqrt(HEAD_DIM)
    k_pos = jnp.arange(s, dtype=jnp.int32)

    def block(args):
        qb, q0 = args
        z = jnp.einsum('bhqd,bhkd->bhqk', qb, k, preferred_element_type=jnp.float32) * scale
        q_pos = q0 + jnp.arange(Q_BLOCK, dtype=jnp.int32)
        mask = k_pos[None, :] < q_pos[:, None]
        log_beta = jax.nn.log_sigmoid(z)
        log_keep = jnp.where(mask, jax.nn.log_sigmoid(-z), 0.0)
        between = lax.cumsum(log_keep, axis=3, reverse=True) - log_keep
        weights = jnp.where(mask, jnp.exp(log_beta + between), 0.0)
        return jnp.einsum('bhqk,bhkd->bhqd', weights.astype(v.dtype), v)

    starts = jnp.arange(s // Q_BLOCK, dtype=jnp.int32) * Q_BLOCK
    return from_blocks(lax.map(block, (to_blocks(q), starts)))


def diff_attention(q1, q2, k1, k2, v, lam, rel_bias):
    s = q1.shape[2]
    scale = 1.0 / math.sqrt(DIFF_QK_DIM)
    k_pos = jnp.arange(s, dtype=jnp.int32)
    neg = jnp.finfo(jnp.float32).min

    def block(args):
        q1b, q2b, q0 = args
        q_pos = q0 + jnp.arange(Q_BLOCK, dtype=jnp.int32)
        bias = jnp.transpose(rel_bias[t5_bucket(q_pos, k_pos)], (2, 0, 1)).astype(jnp.float32)
        mask = k_pos[None, :] <= q_pos[:, None]

        def probs(qb, kk):
            logits = jnp.einsum('bhqd,bhkd->bhqk', qb, kk, preferred_element_type=jnp.float32) * scale + bias
            return jax.nn.softmax(jnp.where(mask, logits, neg), axis=-1)

        p = probs(q1b, k1) - lam * probs(q2b, k2)
        return jnp.einsum('bhqk,bhkd->bhqd', p.astype(v.dtype), v)

    starts = jnp.arange(s // Q_BLOCK, dtype=jnp.int32) * Q_BLOCK
    return from_blocks(lax.map(block, (to_blocks(q1), to_blocks(q2), starts)))


def s5_mixer(u, a_re, a_im, log_dt, b_re, b_im, c_re, c_im, d_skip, w_glu, b_glu):
    bsz, s, _ = u.shape
    f32 = jnp.float32
    uf = u.astype(f32).reshape(bsz, s, SSM_GROUPS, SSM_CH_PER_GROUP)
    ar, ai = a_re.astype(f32), a_im.astype(f32)
    dt = jnp.exp(log_dt.astype(f32))[:, None]
    mag = jnp.exp(ar * dt)
    lb_re, lb_im = mag * jnp.cos(ai * dt), mag * jnp.sin(ai * dt)
    den = ar * ar + ai * ai
    f_re = ((lb_re - 1.0) * ar + lb_im * ai) / den
    f_im = (lb_im * ar - (lb_re - 1.0) * ai) / den
    br, bi = b_re.astype(f32), b_im.astype(f32)
    bb_re = f_re[..., None] * br - f_im[..., None] * bi
    bb_im = f_re[..., None] * bi + f_im[..., None] * br
    bu_re = jnp.einsum('gpc,bsgc->bsgp', bb_re, uf)
    bu_im = jnp.einsum('gpc,bsgc->bsgp', bb_im, uf)
    el_re = jnp.broadcast_to(lb_re, (1, s, SSM_GROUPS, SSM_STATE))
    el_im = jnp.broadcast_to(lb_im, (1, s, SSM_GROUPS, SSM_STATE))

    def combine(e1, e2):
        a1r, a1i, b1r, b1i = e1
        a2r, a2i, b2r, b2i = e2
        return (a2r * a1r - a2i * a1i, a2r * a1i + a2i * a1r,
                a2r * b1r - a2i * b1i + b2r, a2r * b1i + a2i * b1r + b2i)

    _, _, x_re, x_im = lax.associative_scan(combine, (el_re, el_im, bu_re, bu_im), axis=1)
    y = (jnp.einsum('gcp,bsgp->bsgc', c_re.astype(f32), x_re)
         - jnp.einsum('gcp,bsgp->bsgc', c_im.astype(f32), x_im)
         + d_skip.astype(f32) * uf)
    y = jax.nn.gelu(y.reshape(bsz, s, SSM_WIDTH)).astype(u.dtype)
    return y * jax.nn.sigmoid(y @ w_glu + b_glu)


def setup_inputs(seed: int = 0) -> dict:
    key = jax.random.key(seed)
    keys = iter(jax.random.split(key, 48))

    def nrm(shape, scale):
        return jax.random.normal(next(keys), shape, jnp.float32) * scale

    def gain(shape):
        return 1.0 + nrm(shape, 0.02)

    L, D, F = DEPTH, D_MODEL, D_FF
    G, P, C = SSM_GROUPS, SSM_STATE, SSM_CH_PER_GROUP
    n_idx = jnp.arange(P, dtype=jnp.float32)
    inp = {}
    inp["x"] = nrm((BATCH, SEQ, D), 1.0)
    inp["ffn1_pre_g"] = gain((L, D))
    inp["ffn1_w_gate"] = nrm((L, D, F), D ** -0.5)
    inp["ffn1_w_up"] = nrm((L, D, F), D ** -0.5)
    inp["ffn1_w_down"] = nrm((L, F, D), F ** -0.5)
    inp["ffn1_post_g"] = gain((L, D))
    inp["mix_pre_g"] = gain((L, D))
    inp["w_in"] = nrm((L, D, IN_COLS), D ** -0.5)
    inp["ssm_a_re"] = -0.5 + nrm((L, G, P), 0.01)
    inp["ssm_a_im"] = math.pi * n_idx + nrm((L, G, P), 0.01)
    inp["ssm_log_dt"] = jax.random.uniform(next(keys), (L, G), jnp.float32,
                                           math.log(DT_MIN), math.log(DT_MAX))
    inp["ssm_b_re"] = nrm((L, G, P, C), (2 * C) ** -0.5)
    inp["ssm_b_im"] = nrm((L, G, P, C), (2 * C) ** -0.5)
    inp["ssm_c_re"] = nrm((L, G, C, P), (2 * P) ** -0.5)
    inp["ssm_c_im"] = nrm((L, G, C, P), (2 * P) ** -0.5)
    inp["ssm_d"] = nrm((L, G, C), 1.0)
    inp["ssm_w_glu"] = nrm((L, SSM_WIDTH, SSM_WIDTH), SSM_WIDTH ** -0.5)
    inp["ssm_b_glu"] = nrm((L, SSM_WIDTH), 0.02)
    inp["diff_lq1"] = nrm((L, DIFF_QK_DIM), 0.1)
    inp["diff_lk1"] = nrm((L, DIFF_QK_DIM), 0.1)
    inp["diff_lq2"] = nrm((L, DIFF_QK_DIM), 0.1)
    inp["diff_lk2"] = nrm((L, DIFF_QK_DIM), 0.1)
    inp["diff_subln_g"] = gain((L, HEAD_DIM))
    inp["rel_bias"] = nrm((NUM_BUCKETS, DIFF_HEADS), 0.5)
    inp["w_out"] = nrm((L, MIX_WIDTH, D), MIX_WIDTH ** -0.5)
    inp["mix_post_g"] = gain((L, D))
    inp["ffn2_pre_g"] = gain((L, D))
    inp["ffn2_w_gate"] = nrm((L, D, F), D ** -0.5)
    inp["ffn2_w_up"] = nrm((L, D, F), D ** -0.5)
    inp["ffn2_w_down"] = nrm((L, F, D), F ** -0.5)
    inp["ffn2_post_g"] = gain((L, D))
    return inp


def reference(x, ffn1_pre_g, ffn1_w_gate, ffn1_w_up, ffn1_w_down, ffn1_post_g,
              mix_pre_g, w_in, ssm_a_re, ssm_a_im, ssm_log_dt, ssm_b_re, ssm_b_im,
              ssm_c_re, ssm_c_im, ssm_d, ssm_w_glu, ssm_b_glu,
              diff_lq1, diff_lk1, diff_lq2, diff_lk2, diff_subln_g, rel_bias,
              w_out, mix_post_g,
              ffn2_pre_g, ffn2_w_gate, ffn2_w_up, ffn2_w_down, ffn2_post_g):
    split_points = [sum(IN_SPLIT_SIZES[:i + 1]) for i in range(len(IN_SPLIT_SIZES) - 1)]
    h = x
    for l in range(DEPTH):
        y = swiglu(rms_norm(h, ffn1_pre_g[l]), ffn1_w_gate[l], ffn1_w_up[l], ffn1_w_down[l])
        h = h + FFN_RESIDUAL_WEIGHT * rms_norm(y, ffn1_post_g[l])

        xn = rms_norm(h, mix_pre_g[l])
        proj = xn @ w_in[l]
        sb_q, sb_k, sb_v, ssm_u, dq, dk, dv = jnp.split(proj, split_points, axis=-1)
        b, s, _ = proj.shape

        o_sb = stick_breaking_attention(split_heads(sb_q, SB_HEADS, HEAD_DIM),
                                        split_heads(sb_k, SB_HEADS, HEAD_DIM),
                                        split_heads(sb_v, SB_HEADS, HEAD_DIM))

        o_ssm = s5_mixer(ssm_u, ssm_a_re[l], ssm_a_im[l], ssm_log_dt[l], ssm_b_re[l], ssm_b_im[l],
                         ssm_c_re[l], ssm_c_im[l], ssm_d[l], ssm_w_glu[l], ssm_b_glu[l])

        qh = dq.reshape(b, s, DIFF_HEADS, 2, DIFF_QK_DIM).transpose(3, 0, 2, 1, 4)
        kh = dk.reshape(b, s, DIFF_HEADS, 2, DIFF_QK_DIM).transpose(3, 0, 2, 1, 4)
        vh = split_heads(dv, DIFF_HEADS, HEAD_DIM)
        lambda_init = 0.8 - 0.6 * math.exp(-0.3 * l)
        lam = (jnp.exp(jnp.sum(diff_lq1[l].astype(jnp.float32) * diff_lk1[l].astype(jnp.float32)))
               - jnp.exp(jnp.sum(diff_lq2[l].astype(jnp.float32) * diff_lk2[l].astype(jnp.float32)))
               + lambda_init)
        o_diff = diff_attention(qh[0], qh[1], kh[0], kh[1], vh, lam, rel_bias)
        o_diff = rms_norm(o_diff.reshape(b, s, DIFF_HEADS, HEAD_DIM), diff_subln_g[l]) * (1.0 - lambda_init)
        o_diff = o_diff.reshape(b, s, DIFF_WIDTH)

        mixed = jnp.concatenate([o_sb, o_ssm, o_diff], axis=-1)
        h = h + rms_norm(mixed @ w_out[l], mix_post_g[l])

        y = swiglu(rms_norm(h, ffn2_pre_g[l]), ffn2_w_gate[l], ffn2_w_up[l], ffn2_w_down[l])
        h = h + FFN_RESIDUAL_WEIGHT * rms_norm(y, ffn2_post_g[l])
    return h
```

```python
import functools
import math

import jax
import jax.numpy as jnp
from jax import lax
from jax.experimental import pallas as pl
from jax.experimental.pallas import tpu as pltpu

F32 = jnp.float32
BF16 = jnp.bfloat16

HEAD_DIM = 128
SB_HEADS = 6
DIFF_HEADS = 6
DIFF_QK_DIM = HEAD_DIM // 2
SSM_GROUPS = 32
SSM_CH_PER_GROUP = 16
SSM_STATE = 64
SSM_WIDTH = SSM_GROUPS * SSM_CH_PER_GROUP
NUM_BUCKETS = 32
MAX_DISTANCE = 128
NORM_EPS = 1e-6
FFN_RESIDUAL_WEIGHT = 0.5

SB_Q_BLK, SB_K_BLK, SB_V_BLK = 0, 6, 12
DF_Q_BLK, DF_K_BLK, DF_V_BLK = 18, 24, 30
PROJ_COLS = 36 * HEAD_DIM
SSM_COL0 = 3 * SB_HEADS * HEAD_DIM

V7X_VMEM_LIMIT = 56 * 1024 * 1024
ROW_CHUNK = 128
ATT_TILE = 256
SSM_BATCH_PAD = 8
SSM_STATE_BLOCKS = 4
NEG_INF = float(jnp.finfo(jnp.float32).min)


def _params(n_grid, vmem=V7X_VMEM_LIMIT):
    return pltpu.CompilerParams(dimension_semantics=("arbitrary",) * n_grid, vmem_limit_bytes=vmem)


def _rms(x, g):
    ms = jnp.mean(x * x, axis=-1, keepdims=True)
    return x * lax.rsqrt(ms + NORM_EPS) * g


def _for_row_chunks(n_rows, chunk, body):
    def step(i, carry):
        body(pl.multiple_of(i * chunk, chunk))
        return carry
    lax.fori_loop(0, n_rows // chunk, step, 0)


def _ffn_kernel(x_ref, pre_g_ref, wg_ref, wu_ref, wd_ref, post_g_ref, o_ref, xn_ref, *, n_f, n_col_chunks):
    f = pl.program_id(1)
    tm, d = x_ref.shape

    @pl.when(f == 0)
    def _():
        def norm_rows(r0):
            rows = pl.ds(r0, ROW_CHUNK)
            xn_ref[rows, :] = _rms(x_ref[rows, :], pre_g_ref[...]).astype(BF16)
        _for_row_chunks(tm, ROW_CHUNK, norm_rows)

    xn = xn_ref[...]
    g = jnp.dot(xn, wg_ref[...].astype(BF16), preferred_element_type=F32)
    u = jnp.dot(xn, wu_ref[...].astype(BF16), preferred_element_type=F32)
    a = (g * jax.nn.sigmoid(g) * u).astype(BF16)
    wd = wd_ref[...].astype(BF16)
    cw = d // n_col_chunks
    for c in range(n_col_chunks):
        cols = slice(c * cw, (c + 1) * cw)
        y = jnp.dot(a, wd[:, cols], preferred_element_type=F32)

        @pl.when(f == 0)
        def _():
            o_ref[:, cols] = y

        @pl.when(f > 0)
        def _():
            o_ref[:, cols] += y

    @pl.when(f == n_f - 1)
    def _():
        def finish_rows(r0):
            rows = pl.ds(r0, ROW_CHUNK)
            o_ref[rows, :] = x_ref[rows, :] + FFN_RESIDUAL_WEIGHT * _rms(o_ref[rows, :], post_g_ref[...])
        _for_row_chunks(tm, ROW_CHUNK, finish_rows)


def _ffn(h, pre_g, w_gate, w_up, w_down, post_g, *, tm, tf):
    m, d = h.shape
    d_ff = w_gate.shape[1]
    n_f = d_ff // tf
    kern = functools.partial(_ffn_kernel, n_f=n_f, n_col_chunks=2)
    return pl.pallas_call(
        kern,
        grid=(m // tm, n_f),
        in_specs=[
            pl.BlockSpec((tm, d), lambda i, f: (i, 0), pipeline_mode=pl.Buffered(1)),
            pl.BlockSpec((1, d), lambda i, f: (0, 0)),
            pl.BlockSpec((d, tf), lambda i, f: (0, f)),
            pl.BlockSpec((d, tf), lambda i, f: (0, f)),
            pl.BlockSpec((tf, d), lambda i, f: (f, 0)),
            pl.BlockSpec((1, d), lambda i, f: (0, 0)),
        ],
        out_specs=pl.BlockSpec((tm, d), lambda i, f: (i, 0)),
        out_shape=jax.ShapeDtypeStruct((m, d), F32),
        scratch_shapes=[pltpu.VMEM((tm, d), BF16)],
        compiler_params=_params(2),
        name="ffn",
    )(h, pre_g.reshape(1, d), w_gate, w_up, w_down, post_g.reshape(1, d))


def _inproj_kernel(x_ref, g_ref, w_ref, o_ref, xn_ref):
    j = pl.program_id(1)
    tm = x_ref.shape[0]

    @pl.when(j == 0)
    def _():
        def norm_rows(r0):
            rows = pl.ds(r0, ROW_CHUNK)
            xn_ref[rows, :] = _rms(x_ref[rows, :], g_ref[...]).astype(BF16)
        _for_row_chunks(tm, ROW_CHUNK, norm_rows)

    o_ref[...] = jnp.dot(xn_ref[...], w_ref[...].astype(BF16), preferred_element_type=F32).astype(o_ref.dtype)


def _inproj(h, g, w_in, *, tm, tn, n_out_cols, col_block_of, out_dtype, name):
    m, d = h.shape
    return pl.pallas_call(
        _inproj_kernel,
        grid=(m // tm, n_out_cols // tn),
        in_specs=[
            pl.BlockSpec((tm, d), lambda i, j: (i, 0), pipeline_mode=pl.Buffered(1)),
            pl.BlockSpec((1, d), lambda i, j: (0, 0)),
            pl.BlockSpec((d, tn), lambda i, j: (0, col_block_of(j))),
        ],
        out_specs=pl.BlockSpec((tm, tn), lambda i, j: (i, j)),
        out_shape=jax.ShapeDtypeStruct((m, n_out_cols), out_dtype),
        scratch_shapes=[pltpu.VMEM((tm, d), BF16)],
        compiler_params=_params(2),
        name=name,
    )(h, g.reshape(1, d), w_in)


def _split3_bf16(x):
    hi = x.astype(BF16)
    r1 = x - hi.astype(F32)
    mid = r1.astype(BF16)
    lo = (r1 - mid.astype(F32)).astype(BF16)
    return hi, mid, lo


def _sb_kernel(q_ref, k_ref, v_ref, o_ref):
    t = ATT_TILE
    i = pl.program_id(2)
    q = q_ref[...]
    scale = 1.0 / math.sqrt(HEAD_DIM)
    row = lax.broadcasted_iota(jnp.int32, (t, t), 0)
    col = lax.broadcasted_iota(jnp.int32, (t, t), 1)
    strictly_lower = row > col
    suffix_ones = jnp.where(strictly_lower, 1.0, 0.0).astype(BF16)

    def block(j, acc, later, masked):
        k0 = pl.multiple_of(j * t, t)
        k = k_ref[pl.ds(k0, t), :]
        v = v_ref[pl.ds(k0, t), :]
        z = lax.dot_general(q, k, (((1,), (1,)), ((), ())), preferred_element_type=F32) * scale
        sp = jnp.maximum(z, 0.0) + jnp.log(1.0 + jnp.exp(-jnp.abs(z)))
        sp_in = jnp.where(strictly_lower, sp, 0.0) if masked else sp
        hi, mid, lo = _split3_bf16(sp_in)
        suffix = (jnp.dot(hi, suffix_ones, preferred_element_type=F32)
                  + jnp.dot(mid, suffix_ones, preferred_element_type=F32)
                  + jnp.dot(lo, suffix_ones, preferred_element_type=F32))
        w = jnp.exp(z - sp - suffix - later)
        if masked:
            w = jnp.where(strictly_lower, w, 0.0)
        acc = acc + jnp.dot(w.astype(BF16), v, preferred_element_type=F32)
        later = later + jnp.sum(sp_in, axis=1, keepdims=True)
        return acc, later

    acc = jnp.zeros((t, HEAD_DIM), F32)
    later = jnp.zeros((t, 1), F32)
    acc, later = block(i, acc, later, True)

    def body(n, carry):
        return block(i - 1 - n, carry[0], carry[1], False)

    acc, later = lax.fori_loop(0, i, body, (acc, later))
    o_ref[...] = acc.astype(o_ref.dtype)


def _sb_attention(proj, *, batch, seq):
    t = ATT_TILE
    return pl.pallas_call(
        _sb_kernel,
        grid=(batch, SB_HEADS, seq // t),
        in_specs=[
            pl.BlockSpec((None, t, HEAD_DIM), lambda b, h, i: (b, i, SB_Q_BLK + h)),
            pl.BlockSpec((None, seq, HEAD_DIM), lambda b, h, i: (b, 0, SB_K_BLK + h)),
            pl.BlockSpec((None, seq, HEAD_DIM), lambda b, h, i: (b, 0, SB_V_BLK + h)),
        ],
        out_specs=pl.BlockSpec((None, t, HEAD_DIM), lambda b, h, i: (b, i, h)),
        out_shape=jax.ShapeDtypeStruct((batch, seq, SB_HEADS * HEAD_DIM), BF16),
        compiler_params=_params(3),
        name="sb_attention",
    )(proj, proj, proj)


def _t5_bias_kernel(rel_ref, o_ref):
    t = ATT_TILE
    h = pl.program_id(0)
    row = lax.broadcasted_iota(jnp.int32, (t, t), 0)
    col = lax.broadcasted_iota(jnp.int32, (t, t), 1)
    max_exact = NUM_BUCKETS // 2
    for d in range(3):
        n = jnp.maximum(d * t + row - col, 0)
        nf = jnp.maximum(n, 1).astype(F32)
        large = max_exact + (jnp.log(nf / max_exact) / math.log(MAX_DISTANCE / max_exact)
                             * (NUM_BUCKETS - max_exact)).astype(jnp.int32)
        large = jnp.minimum(large, NUM_BUCKETS - 1)
        bucket = jnp.where(n < max_exact, n, large)
        bias = jnp.zeros((t, t), F32)
        for b in range(NUM_BUCKETS):
            bias = jnp.where(bucket == b, rel_ref[b, h], bias)
        o_ref[d] = bias


def _t5_bias_tiles(rel_bias):
    t = ATT_TILE
    return pl.pallas_call(
        _t5_bias_kernel,
        grid=(DIFF_HEADS,),
        in_specs=[pl.BlockSpec(memory_space=pltpu.SMEM)],
        out_specs=pl.BlockSpec((None, 3, t, t), lambda h: (h, 0, 0, 0)),
        out_shape=jax.ShapeDtypeStruct((DIFF_HEADS, 3, t, t), F32),
        compiler_params=_params(1),
        name="t5_bias_tiles",
    )(rel_bias)


def _diff_kernel(lq1_ref, lk1_ref, lq2_ref, lk2_ref, subln_g_ref, bias_ref, q_ref, k_ref, v_ref, o_ref,
                 *, lambda_init):
    t = ATT_TILE
    i = pl.program_id(2)
    scale = 1.0 / math.sqrt(DIFF_QK_DIM)
    q = q_ref[...]
    lane = lax.broadcasted_iota(jnp.int32, q.shape, 1)
    zero = jnp.zeros_like(q)
    q_maps = (jnp.where(lane < DIFF_QK_DIM, q, zero), jnp.where(lane >= DIFF_QK_DIM, q, zero))
    row = lax.broadcasted_iota(jnp.int32, (t, t), 0)
    col = lax.broadcasted_iota(jnp.int32, (t, t), 1)
    causal = col <= row

    def block(j, state, masked):
        k0 = pl.multiple_of(j * t, t)
        k = k_ref[pl.ds(k0, t), :]
        v = v_ref[pl.ds(k0, t), :]
        bias = bias_ref[jnp.minimum(i - j, 2)]
        new_state = []
        for qm, (m_prev, l_prev, acc_prev) in zip(q_maps, state):
            s = lax.dot_general(qm, k, (((1,), (1,)), ((), ())), preferred_element_type=F32) * scale + bias
            if masked:
                s = jnp.where(causal, s, NEG_INF)
            m_new = jnp.maximum(m_prev, jnp.max(s, axis=1, keepdims=True))
            alpha = jnp.exp(m_prev - m_new)
            p = jnp.exp(s - m_new)
            l_new = alpha * l_prev + jnp.sum(p, axis=1, keepdims=True)
            acc_new = alpha * acc_prev + jnp.dot(p.astype(BF16), v, preferred_element_type=F32)
            new_state.append((m_new, l_new, acc_new))
        return tuple(new_state)

    init = tuple((jnp.full((t, 1), NEG_INF, F32), jnp.zeros((t, 1), F32), jnp.zeros((t, HEAD_DIM), F32))
                 for _ in range(2))
    state = block(i, init, True)
    state = lax.fori_loop(0, i, lambda n, st: block(i - 1 - n, st, False), state)

    lam = (jnp.exp(jnp.sum(lq1_ref[...] * lk1_ref[...], axis=1, keepdims=True))
           - jnp.exp(jnp.sum(lq2_ref[...] * lk2_ref[...], axis=1, keepdims=True)) + lambda_init)
    (_, l1, acc1), (_, l2, acc2) = state
    o = acc1 / l1 - lam * (acc2 / l2)
    o_ref[...] = (_rms(o, subln_g_ref[...]) * (1.0 - lambda_init)).astype(o_ref.dtype)


def _diff_attention(proj, bias_tiles, lq1, lk1, lq2, lk2, subln_g, *, batch, seq, lambda_init):
    t = ATT_TILE
    vec = lambda n: pl.BlockSpec((1, n), lambda b, h, i: (0, 0))
    kern = functools.partial(_diff_kernel, lambda_init=lambda_init)
    return pl.pallas_call(
        kern,
        grid=(batch, DIFF_HEADS, seq // t),
        in_specs=[
            vec(DIFF_QK_DIM), vec(DIFF_QK_DIM), vec(DIFF_QK_DIM), vec(DIFF_QK_DIM), vec(HEAD_DIM),
            pl.BlockSpec((None, 3, t, t), lambda b, h, i: (h, 0, 0, 0)),
            pl.BlockSpec((None, t, HEAD_DIM), lambda b, h, i: (b, i, DF_Q_BLK + h)),
            pl.BlockSpec((None, seq, HEAD_DIM), lambda b, h, i: (b, 0, DF_K_BLK + h)),
            pl.BlockSpec((None, seq, HEAD_DIM), lambda b, h, i: (b, 0, DF_V_BLK + h)),
        ],
        out_specs=pl.BlockSpec((None, t, HEAD_DIM), lambda b, h, i: (b, i, h)),
        out_shape=jax.ShapeDtypeStruct((batch, seq, DIFF_HEADS * HEAD_DIM), BF16),
        compiler_params=_params(3),
        name="diff_attention",
    )(lq1.reshape(1, -1), lk1.reshape(1, -1), lq2.reshape(1, -1), lk2.reshape(1, -1), subln_g.reshape(1, -1),
      bias_tiles, proj, proj, proj)


def _ssm_prep_kernel(a_re_ref, a_im_ref, log_dt_ref, b_re_ref, b_im_ref, c_re_ref, c_im_ref,
                     lam_ref, bbar_ref, cmat_ref):
    ar, ai = a_re_ref[...], a_im_ref[...]
    dt = jnp.exp(log_dt_ref[...])
    mag = jnp.exp(ar * dt)
    lb_re, lb_im = mag * jnp.cos(ai * dt), mag * jnp.sin(ai * dt)
    den = ar * ar + ai * ai
    f_re = ((lb_re - 1.0) * ar + lb_im * ai) / den
    f_im = (lb_im * ar - (lb_re - 1.0) * ai) / den
    lam_ref[0:1, :] = lb_re
    lam_ref[1:2, :] = lb_im
    br, bi = b_re_ref[...], b_im_ref[...]
    nb = br.shape[1]
    bbar_ref[:, 0:nb] = (f_re * br - f_im * bi).astype(BF16)
    bbar_ref[:, nb:2 * nb] = (f_re * bi + f_im * br).astype(BF16)
    cmat_ref[0:nb, :] = c_re_ref[...].astype(BF16)
    cmat_ref[nb:2 * nb, :] = (-c_im_ref[...]).astype(BF16)


def _block_diag_groups(x):
    gpb = SSM_GROUPS // SSM_STATE_BLOCKS
    g, r, k = x.shape
    xb = x.reshape(SSM_STATE_BLOCKS, gpb, r, k)
    eye = jnp.eye(gpb, dtype=x.dtype)
    return jnp.einsum("agrk,gh->agrhk", xb, eye).reshape(SSM_STATE_BLOCKS, gpb * r, gpb * k)


def _ssm_prep(a_re, a_im, log_dt, b_re, b_im, c_re, c_im):
    n_state = SSM_GROUPS * SSM_STATE
    nb = n_state // SSM_STATE_BLOCKS
    cb = SSM_WIDTH // SSM_STATE_BLOCKS
    row = lambda x: x.reshape(1, n_state)
    log_dt_row = jnp.broadcast_to(log_dt[:, None], (SSM_GROUPS, SSM_STATE))
    b_args = [_block_diag_groups(jnp.swapaxes(b, 1, 2)) for b in (b_re, b_im)]
    c_args = [_block_diag_groups(jnp.swapaxes(c, 1, 2)) for c in (c_re, c_im)]
    vec = pl.BlockSpec((1, nb), lambda s: (0, s))
    return pl.pallas_call(
        _ssm_prep_kernel,
        grid=(SSM_STATE_BLOCKS,),
        in_specs=[vec, vec, vec,
                  pl.BlockSpec((None, cb, nb), lambda s: (s, 0, 0)),
                  pl.BlockSpec((None, cb, nb), lambda s: (s, 0, 0)),
                  pl.BlockSpec((None, nb, cb), lambda s: (s, 0, 0)),
                  pl.BlockSpec((None, nb, cb), lambda s: (s, 0, 0))],
        out_specs=[pl.BlockSpec((2, nb), lambda s: (0, s)),
                   pl.BlockSpec((None, cb, 2 * nb), lambda s: (s, 0, 0)),
                   pl.BlockSpec((None, 2 * nb, cb), lambda s: (s, 0, 0))],
        out_shape=[jax.ShapeDtypeStruct((2, n_state), F32),
                   jax.ShapeDtypeStruct((SSM_STATE_BLOCKS, cb, 2 * nb), BF16),
                   jax.ShapeDtypeStruct((SSM_STATE_BLOCKS, 2 * nb, cb), BF16)],
        compiler_params=_params(1),
        name="ssm_prep",
    )(row(a_re), row(a_im), row(log_dt_row), *b_args, *c_args)


def _gelu_tanh(x):
    return 0.5 * x * (1.0 + jnp.tanh(math.sqrt(2.0 / math.pi) * (x + 0.044715 * (x * x * x))))


def _ssm_kernel(u_ref, lam_ref, bbar_ref, cmat_ref, d_ref, wglu_ref, bglu_ref, o_ref,
                state_ref, bu_ref, y_ref, wglu_bf_ref, *, t_chunk):
    c = pl.program_id(0)
    p = SSM_BATCH_PAD
    nb = lam_ref.shape[1] // SSM_STATE_BLOCKS
    cb = SSM_WIDTH // SSM_STATE_BLOCKS

    @pl.when(c == 0)
    def _():
        state_ref[...] = jnp.zeros_like(state_ref)
        wglu_bf_ref[...] = wglu_ref[...].astype(BF16)

    for s in range(SSM_STATE_BLOCKS):
        u_s = u_ref[:, s * cb:(s + 1) * cb].astype(BF16)
        bu_ref[...] = jnp.dot(u_s, bbar_ref[s], preferred_element_type=F32)
        ar = jnp.broadcast_to(lam_ref[0:1, s * nb:(s + 1) * nb], (p, nb))
        ai = jnp.broadcast_to(lam_ref[1:2, s * nb:(s + 1) * nb], (p, nb))
        re_cols = slice(s * 2 * nb, s * 2 * nb + nb)
        im_cols = slice(s * 2 * nb + nb, (s + 1) * 2 * nb)

        def step(tt, carry):
            xr, xi = carry
            rows = pl.ds(pl.multiple_of(tt * p, p), p)
            nxr = ar * xr - ai * xi + bu_ref[rows, 0:nb]
            nxi = ar * xi + ai * xr + bu_ref[rows, nb:2 * nb]
            bu_ref[rows, 0:nb] = nxr
            bu_ref[rows, nb:2 * nb] = nxi
            return nxr, nxi

        xr, xi = lax.fori_loop(0, t_chunk, step, (state_ref[:, re_cols], state_ref[:, im_cols]), unroll=8)
        state_ref[:, re_cols] = xr
        state_ref[:, im_cols] = xi
        y_ref[:, s * cb:(s + 1) * cb] = jnp.dot(bu_ref[...].astype(BF16), cmat_ref[s],
                                                preferred_element_type=F32)

    y = _gelu_tanh(y_ref[...] + d_ref[...] * u_ref[...])
    gate = jax.nn.sigmoid(jnp.dot(y.astype(BF16), wglu_bf_ref[...], preferred_element_type=F32) + bglu_ref[...])
    o_ref[...] = (y * gate).astype(o_ref.dtype)


def _ssm(u_pad, lam, bbar, cmat, d_skip, w_glu, b_glu, *, seq, t_chunk):
    p = SSM_BATCH_PAD
    n_state = lam.shape[1]
    nb = n_state // SSM_STATE_BLOCKS
    rows = t_chunk * p
    full = lambda shape: pl.BlockSpec(shape, lambda c: (0,) * len(shape))
    kern = functools.partial(_ssm_kernel, t_chunk=t_chunk)
    return pl.pallas_call(
        kern,
        grid=(seq // t_chunk,),
        in_specs=[
            pl.BlockSpec((rows, SSM_WIDTH), lambda c: (c, 0)),
            full(lam.shape), full(bbar.shape), full(cmat.shape),
            full((1, SSM_WIDTH)), full((SSM_WIDTH, SSM_WIDTH)), full((1, SSM_WIDTH)),
        ],
        out_specs=pl.BlockSpec((rows, SSM_WIDTH), lambda c: (c, 0)),
        out_shape=jax.ShapeDtypeStruct((seq * p, SSM_WIDTH), BF16),
        scratch_shapes=[
            pltpu.VMEM((p, 2 * n_state), F32),
            pltpu.VMEM((rows, 2 * nb), F32),
            pltpu.VMEM((rows, SSM_WIDTH), F32),
            pltpu.VMEM((SSM_WIDTH, SSM_WIDTH), BF16),
        ],
        compiler_params=_params(1),
        name="ssm_scan",
    )(u_pad, lam, bbar, cmat, d_skip.reshape(1, SSM_WIDTH), w_glu, b_glu.reshape(1, SSM_WIDTH))


def _outproj_kernel(h_ref, x_ref, w_ref, g_ref, o_ref, *, n_k):
    kk = pl.program_id(1)
    tm = h_ref.shape[0]
    y = jnp.dot(x_ref[...], w_ref[...].astype(BF16), preferred_element_type=F32)

    @pl.when(kk == 0)
    def _():
        o_ref[...] = y

    @pl.when(kk > 0)
    def _():
        o_ref[...] += y

    @pl.when(kk == n_k - 1)
    def _():
        def finish_rows(r0):
            rows = pl.ds(r0, ROW_CHUNK)
            o_ref[rows, :] = h_ref[rows, :] + _rms(o_ref[rows, :], g_ref[...])
        _for_row_chunks(tm, ROW_CHUNK, finish_rows)


def _outproj(h, mixed, w_out, g, *, tm, tk):
    m, d = h.shape
    k = mixed.shape[1]
    n_k = k // tk
    kern = functools.partial(_outproj_kernel, n_k=n_k)
    return pl.pallas_call(
        kern,
        grid=(m // tm, n_k),
        in_specs=[
            pl.BlockSpec((tm, d), lambda i, kk: (i, 0), pipeline_mode=pl.Buffered(1)),
            pl.BlockSpec((tm, tk), lambda i, kk: (i, kk)),
            pl.BlockSpec((tk, d), lambda i, kk: (kk, 0)),
            pl.BlockSpec((1, d), lambda i, kk: (0, 0)),
        ],
        out_specs=pl.BlockSpec((tm, d), lambda i, kk: (i, 0)),
        out_shape=jax.ShapeDtypeStruct((m, d), F32),
        compiler_params=_params(2),
        name="outproj",
    )(h, mixed, w_out, g.reshape(1, d))


def _trunk(x, ffn1_pre_g, ffn1_w_gate, ffn1_w_up, ffn1_w_down, ffn1_post_g, mix_pre_g, w_in, ssm_a_re, ssm_a_im,
           ssm_log_dt, ssm_b_re, ssm_b_im, ssm_c_re, ssm_c_im, ssm_d, ssm_w_glu, ssm_b_glu, diff_lq1, diff_lk1,
           diff_lq2, diff_lk2, diff_subln_g, rel_bias, w_out, mix_post_g, ffn2_pre_g, ffn2_w_gate, ffn2_w_up,
           ffn2_w_down, ffn2_post_g, *, tm, tf, t_chunk):
    batch, seq, d = x.shape
    m = batch * seq
    depth = w_in.shape[0]
    h = x.reshape(m, d)
    bias_tiles = _t5_bias_tiles(rel_bias)
    skip_ssm = lambda j: j + jnp.where(j >= SSM_COL0 // 256, SSM_WIDTH // 256, 0)
    for l in range(depth):
        h = _ffn(h, ffn1_pre_g[l], ffn1_w_gate[l], ffn1_w_up[l], ffn1_w_down[l], ffn1_post_g[l], tm=tm, tf=tf)

        proj = _inproj(h, mix_pre_g[l], w_in[l], tm=tm, tn=256, n_out_cols=PROJ_COLS, col_block_of=skip_ssm,
                       out_dtype=BF16, name="inproj_heads").reshape(batch, seq, PROJ_COLS)
        ssm_u = _inproj(h, mix_pre_g[l], w_in[l], tm=tm, tn=256, n_out_cols=SSM_WIDTH,
                        col_block_of=lambda j: j + SSM_COL0 // 256, out_dtype=F32, name="inproj_ssm")

        o_sb = _sb_attention(proj, batch=batch, seq=seq)

        lam, bbar, cmat = _ssm_prep(ssm_a_re[l], ssm_a_im[l], ssm_log_dt[l], ssm_b_re[l], ssm_b_im[l],
                                    ssm_c_re[l], ssm_c_im[l])
        u_pad = jnp.pad(jnp.swapaxes(ssm_u.reshape(batch, seq, SSM_WIDTH), 0, 1),
                        ((0, 0), (0, SSM_BATCH_PAD - batch), (0, 0))).reshape(seq * SSM_BATCH_PAD, SSM_WIDTH)
        o_ssm = _ssm(u_pad, lam, bbar, cmat, ssm_d[l], ssm_w_glu[l], ssm_b_glu[l], seq=seq, t_chunk=t_chunk)
        o_ssm = jnp.swapaxes(o_ssm.reshape(seq, SSM_BATCH_PAD, SSM_WIDTH)[:, :batch], 0, 1)

        lambda_init = 0.8 - 0.6 * math.exp(-0.3 * l)
        o_diff = _diff_attention(proj, bias_tiles, diff_lq1[l], diff_lk1[l], diff_lq2[l], diff_lk2[l],
                                 diff_subln_g[l], batch=batch, seq=seq, lambda_init=lambda_init)

        mixed = jnp.concatenate([o_sb, o_ssm, o_diff], axis=-1).reshape(m, -1)
        h = _outproj(h, mixed, w_out[l], mix_post_g[l], tm=tm, tk=512)

        h = _ffn(h, ffn2_pre_g[l], ffn2_w_gate[l], ffn2_w_up[l], ffn2_w_down[l], ffn2_post_g[l], tm=tm, tf=tf)
    return h.reshape(batch, seq, d)


def kernel(x, ffn1_pre_g, ffn1_w_gate, ffn1_w_up, ffn1_w_down, ffn1_post_g, mix_pre_g, w_in, ssm_a_re, ssm_a_im, ssm_log_dt, ssm_b_re, ssm_b_im, ssm_c_re, ssm_c_im, ssm_d, ssm_w_glu, ssm_b_glu, diff_lq1, diff_lk1, diff_lq2, diff_lk2, diff_subln_g, rel_bias, w_out, mix_post_g, ffn2_pre_g, ffn2_w_gate, ffn2_w_up, ffn2_w_down, ffn2_post_g):
    assert x.shape[0] <= SSM_BATCH_PAD and x.shape[1] % ATT_TILE == 0
    return _trunk(x, ffn1_pre_g, ffn1_w_gate, ffn1_w_up, ffn1_w_down, ffn1_post_g, mix_pre_g, w_in, ssm_a_re,
                  ssm_a_im, ssm_log_dt, ssm_b_re, ssm_b_im, ssm_c_re, ssm_c_im, ssm_d, ssm_w_glu, ssm_b_glu,
                  diff_lq1, diff_lk1, diff_lq2, diff_lk2, diff_subln_g, rel_bias, w_out, mix_post_g, ffn2_pre_g,
                  ffn2_w_gate, ffn2_w_up, ffn2_w_down, ffn2_post_g, tm=1024, tf=256, t_chunk=128)
```

```python
import functools
import math

import jax
import jax.numpy as jnp
from jax import lax
from jax.experimental import pallas as pl
from jax.experimental.pallas import tpu as pltpu

F32 = jnp.float32
BF16 = jnp.bfloat16

HEAD_DIM = 128
SB_HEADS = 6
DIFF_HEADS = 6
DIFF_QK_DIM = HEAD_DIM // 2
SSM_GROUPS = 32
SSM_CH_PER_GROUP = 16
SSM_STATE = 64
SSM_WIDTH = SSM_GROUPS * SSM_CH_PER_GROUP
NUM_BUCKETS = 32
MAX_DISTANCE = 128
NORM_EPS = 1e-6
FFN_RESIDUAL_WEIGHT = 0.5
LOG2E = 1.0 / math.log(2.0)

SB_Q_BLK, SB_K_BLK, SB_V_BLK = 0, 6, 12
DF_Q_BLK, DF_K_BLK, DF_V_BLK = 18, 24, 30
PROJ_COLS = 36 * HEAD_DIM
SSM_COL0 = 3 * SB_HEADS * HEAD_DIM
W_IN_SUB = 256
PROJ_STEP_COLS = 3 * W_IN_SUB

V7X_VMEM_LIMIT = 56 * 1024 * 1024
ROW_CHUNK = 128
ATT_TILE = 256
SSM_BATCH_PAD = 8
SSM_STATE_BLOCKS = 4
NEG_INF = float(jnp.finfo(jnp.float32).min)


def _params(n_grid, vmem=V7X_VMEM_LIMIT):
    return pltpu.CompilerParams(dimension_semantics=("arbitrary",) * n_grid, vmem_limit_bytes=vmem)


def _rms(x, g):
    ms = jnp.mean(x * x, axis=-1, keepdims=True)
    return x * lax.rsqrt(ms + NORM_EPS) * g


def _for_row_chunks(n_rows, chunk, body):
    def step(i, carry):
        body(pl.multiple_of(i * chunk, chunk))
        return carry
    lax.fori_loop(0, n_rows // chunk, step, 0)


def _layer_vec_spec(layer, n, n_grid):
    zeros = (0,) * 2
    return pl.BlockSpec((None, 1, n), lambda *_: (layer,) + zeros)


def _as_rows(p):
    return p.reshape(p.shape[0], 1, p.shape[1])


def _ffn_kernel(x_ref, pre_g_ref, wg_ref, wu_ref, wd_ref, post_g_ref, o_ref, xn_ref, *, n_f, n_col_chunks):
    f = pl.program_id(1)
    tm, d = x_ref.shape

    @pl.when(f == 0)
    def _():
        def norm_rows(r0):
            rows = pl.ds(r0, ROW_CHUNK)
            xn_ref[rows, :] = _rms(x_ref[rows, :], pre_g_ref[...]).astype(BF16)
            o_ref[rows, :] = jnp.zeros((ROW_CHUNK, d), F32)
        _for_row_chunks(tm, ROW_CHUNK, norm_rows)

    xn = xn_ref[...]
    g = jnp.dot(xn, wg_ref[...].astype(BF16), preferred_element_type=F32)
    u = jnp.dot(xn, wu_ref[...].astype(BF16), preferred_element_type=F32)
    a = (g * jax.nn.sigmoid(g) * u).astype(BF16)
    wd = wd_ref[...].astype(BF16)
    cw = d // n_col_chunks
    for c in range(n_col_chunks):
        cols = slice(c * cw, (c + 1) * cw)
        o_ref[:, cols] += jnp.dot(a, wd[:, cols], preferred_element_type=F32)

    @pl.when(f == n_f - 1)
    def _():
        def finish_rows(r0):
            rows = pl.ds(r0, ROW_CHUNK)
            o_ref[rows, :] = x_ref[rows, :] + FFN_RESIDUAL_WEIGHT * _rms(o_ref[rows, :], post_g_ref[...])
        _for_row_chunks(tm, ROW_CHUNK, finish_rows)


def _ffn(h, layer, pre_g, w_gate, w_up, w_down, post_g, *, tm, tf):
    m, d = h.shape
    d_ff = w_gate.shape[2]
    n_f = d_ff // tf
    kern = functools.partial(_ffn_kernel, n_f=n_f, n_col_chunks=4)
    return pl.pallas_call(
        kern,
        grid=(m // tm, n_f),
        in_specs=[
            pl.BlockSpec((tm, d), lambda i, f: (i, 0), pipeline_mode=pl.Buffered(1)),
            _layer_vec_spec(layer, d, 2),
            pl.BlockSpec((None, d, tf), lambda i, f: (layer, 0, f)),
            pl.BlockSpec((None, d, tf), lambda i, f: (layer, 0, f)),
            pl.BlockSpec((None, tf, d), lambda i, f: (layer, f, 0)),
            _layer_vec_spec(layer, d, 2),
        ],
        out_specs=pl.BlockSpec((tm, d), lambda i, f: (i, 0)),
        out_shape=jax.ShapeDtypeStruct((m, d), F32),
        scratch_shapes=[pltpu.VMEM((tm, d), BF16)],
        compiler_params=_params(2),
        name="ffn",
    )(h, _as_rows(pre_g), w_gate, w_up, w_down, _as_rows(post_g))


def _inproj_kernel(x_ref, g_ref, w0_ref, w1_ref, w2_ref, proj_ref, u_ref, xn_ref, *, n_head_steps):
    j = pl.program_id(1)
    tm = x_ref.shape[0]
    sub = W_IN_SUB

    @pl.when(j == 0)
    def _():
        def norm_rows(r0):
            rows = pl.ds(r0, ROW_CHUNK)
            xn_ref[rows, :] = _rms(x_ref[rows, :], g_ref[...]).astype(BF16)
        _for_row_chunks(tm, ROW_CHUNK, norm_rows)

    @pl.when(j < n_head_steps)
    def _():
        xn = xn_ref[...]
        for c, w_ref in enumerate((w0_ref, w1_ref, w2_ref)):
            y = jnp.dot(xn, w_ref[...].astype(BF16), preferred_element_type=F32)
            proj_ref[:, c * sub:(c + 1) * sub] = y.astype(proj_ref.dtype)

    @pl.when(j == n_head_steps)
    def _():
        xn = xn_ref[...]
        for c, w_ref in enumerate((w0_ref, w1_ref)):
            u_ref[:, c * sub:(c + 1) * sub] = jnp.dot(xn, w_ref[...].astype(BF16), preferred_element_type=F32)


def _inproj(h, layer, g, w_in, *, tm):
    m, d = h.shape
    n_head_steps = PROJ_COLS // PROJ_STEP_COLS
    ssm_blk0 = SSM_COL0 // W_IN_SUB
    n_ssm_blks = SSM_WIDTH // W_IN_SUB
    last_blk = w_in.shape[2] // W_IN_SUB - 1

    def w_spec(c):
        def index(i, j):
            head_blk = 3 * j + c + jnp.where(3 * j + c >= ssm_blk0, n_ssm_blks, 0)
            blk = jnp.where(j < n_head_steps, head_blk, ssm_blk0 + c)
            return (layer, 0, jnp.minimum(blk, last_blk))
        return pl.BlockSpec((None, d, W_IN_SUB), index)

    kern = functools.partial(_inproj_kernel, n_head_steps=n_head_steps)
    return pl.pallas_call(
        kern,
        grid=(m // tm, n_head_steps + 1),
        in_specs=[
            pl.BlockSpec((tm, d), lambda i, j: (i, 0), pipeline_mode=pl.Buffered(1)),
            _layer_vec_spec(layer, d, 2),
            w_spec(0), w_spec(1), w_spec(2),
        ],
        out_specs=[
            pl.BlockSpec((tm, PROJ_STEP_COLS), lambda i, j: (i, jnp.minimum(j, n_head_steps - 1))),
            pl.BlockSpec((tm, SSM_WIDTH), lambda i, j: (i, 0)),
        ],
        out_shape=[jax.ShapeDtypeStruct((m, PROJ_COLS), BF16), jax.ShapeDtypeStruct((m, SSM_WIDTH), F32)],
        scratch_shapes=[pltpu.VMEM((tm, d), BF16)],
        compiler_params=_params(2),
        name="inproj",
    )(h, _as_rows(g), w_in, w_in, w_in)


def _split2_bf16(x):
    hi = x.astype(BF16)
    lo = (x - hi.astype(F32)).astype(BF16)
    return hi, lo


def _mask_first_tile(x, keep):
    t = ATT_TILE
    head = jnp.where(keep, x[0:t], 0.0)
    return head if x.shape[0] == t else jnp.concatenate([head, x[t:]], axis=0)


def _sb_kernel(q_ref, k_ref, v_ref, o_ref):
    t = ATT_TILE
    seq = q_ref.shape[0]
    n_tiles = seq // t
    scale2 = LOG2E / math.sqrt(HEAD_DIM)
    row = lax.broadcasted_iota(jnp.int32, (t, t), 0)
    col = lax.broadcasted_iota(jnp.int32, (t, t), 1)
    strictly_lower = row > col
    ones = jnp.where(strictly_lower, 1.0, 0.0).astype(BF16)
    suffix_ones = jnp.concatenate([ones, ones], axis=0)
    acc = [None] * n_tiles
    later = [None] * n_tiles

    for j in range(n_tiles - 1, -1, -1):
        r0 = j * t
        k = k_ref[r0:r0 + t, :]
        v = v_ref[r0:r0 + t, :]
        z = lax.dot_general(q_ref[r0:seq, :], k, (((1,), (1,)), ((), ())), preferred_element_type=F32) * scale2
        pos = jnp.maximum(z, 0.0)
        neg = jnp.minimum(z, 0.0)
        l2 = jnp.log2(1.0 + jnp.exp2(neg - pos))
        sp = pos + l2
        log_beta = neg - l2
        sp_in = _mask_first_tile(sp, strictly_lower)
        hi, lo = _split2_bf16(sp_in)
        suffix = jnp.dot(jnp.concatenate([hi, lo], axis=1), suffix_ones, preferred_element_type=F32)
        later_rows = jnp.concatenate([jnp.zeros((t, 1), F32)] + later[j + 1:], axis=0)
        w = jnp.exp2(log_beta - suffix - later_rows)
        w = _mask_first_tile(w, strictly_lower)
        pv = jnp.dot(w.astype(BF16), v, preferred_element_type=F32)
        row_sum = jnp.sum(sp_in, axis=1, keepdims=True)
        for i in range(j, n_tiles):
            rows = slice((i - j) * t, (i - j + 1) * t)
            acc[i] = pv[rows] if i == j else acc[i] + pv[rows]
            later[i] = row_sum[rows] if i == j else later[i] + row_sum[rows]

    for i in range(n_tiles):
        o_ref[i * t:(i + 1) * t, :] = acc[i].astype(o_ref.dtype)


def _sb_attention(proj, *, batch, seq):
    spec = lambda blk: pl.BlockSpec((None, seq, HEAD_DIM), lambda b, h: (b, 0, blk + h))
    return pl.pallas_call(
        _sb_kernel,
        grid=(batch, SB_HEADS),
        in_specs=[spec(SB_Q_BLK), spec(SB_K_BLK), spec(SB_V_BLK)],
        out_specs=spec(0),
        out_shape=jax.ShapeDtypeStruct((batch, seq, SB_HEADS * HEAD_DIM), BF16),
        compiler_params=_params(2),
        name="sb_attention",
    )(proj, proj, proj)


def _t5_bias_kernel(rel_ref, o_ref):
    t = ATT_TILE
    h = pl.program_id(0)
    row = lax.broadcasted_iota(jnp.int32, (t, t), 0)
    col = lax.broadcasted_iota(jnp.int32, (t, t), 1)
    max_exact = NUM_BUCKETS // 2
    for d in range(3):
        n = jnp.maximum(d * t + row - col, 0)
        nf = jnp.maximum(n, 1).astype(F32)
        large = max_exact + (jnp.log(nf / max_exact) / math.log(MAX_DISTANCE / max_exact)
                             * (NUM_BUCKETS - max_exact)).astype(jnp.int32)
        large = jnp.minimum(large, NUM_BUCKETS - 1)
        bucket = jnp.where(n < max_exact, n, large)
        bias = jnp.zeros((t, t), F32)
        for b in range(NUM_BUCKETS):
            bias = jnp.where(bucket == b, rel_ref[b, h], bias)
        o_ref[d] = bias * LOG2E


def _t5_bias_tiles(rel_bias):
    t = ATT_TILE
    return pl.pallas_call(
        _t5_bias_kernel,
        grid=(DIFF_HEADS,),
        in_specs=[pl.BlockSpec(memory_space=pltpu.SMEM)],
        out_specs=pl.BlockSpec((None, 3, t, t), lambda h: (h, 0, 0, 0)),
        out_shape=jax.ShapeDtypeStruct((DIFF_HEADS, 3, t, t), F32),
        compiler_params=_params(1),
        name="t5_bias_tiles",
    )(rel_bias)


def _diff_kernel(lq1_ref, lk1_ref, lq2_ref, lk2_ref, subln_g_ref, bias_ref, q_ref, k_ref, v_ref, o_ref,
                 *, lambda_init):
    t = ATT_TILE
    seq = q_ref.shape[0]
    n_tiles = seq // t
    scale2 = LOG2E / math.sqrt(DIFF_QK_DIM)
    lane = lax.broadcasted_iota(jnp.int32, (t, HEAD_DIM), 1)
    row = lax.broadcasted_iota(jnp.int32, (t, t), 0)
    col = lax.broadcasted_iota(jnp.int32, (t, t), 1)
    causal = col <= row
    lam = (jnp.exp(jnp.sum(lq1_ref[...] * lk1_ref[...], axis=1, keepdims=True))
           - jnp.exp(jnp.sum(lq2_ref[...] * lk2_ref[...], axis=1, keepdims=True)) + lambda_init)

    for i in range(n_tiles):
        n_keys = (i + 1) * t
        q = q_ref[i * t:(i + 1) * t, :]
        zero = jnp.zeros_like(q)
        q_maps = (jnp.where(lane < DIFF_QK_DIM, q, zero), jnp.where(lane >= DIFF_QK_DIM, q, zero))
        k = k_ref[0:n_keys, :]
        exps, sums = [], []
        for qm in q_maps:
            raw = lax.dot_general(qm, k, (((1,), (1,)), ((), ())), preferred_element_type=F32)
            parts = []
            for j in range(i + 1):
                s_j = raw[:, j * t:(j + 1) * t] * scale2 + bias_ref[min(i - j, 2)]
                parts.append(jnp.where(causal, s_j, NEG_INF) if j == i else s_j)
            s = parts[0] if i == 0 else jnp.concatenate(parts, axis=1)
            e = jnp.exp2(s - jnp.max(s, axis=1, keepdims=True))
            exps.append(e)
            sums.append(jnp.sum(e, axis=1, keepdims=True))
        p = exps[0] * (1.0 / sums[0]) - exps[1] * (lam / sums[1])
        o = jnp.dot(p.astype(BF16), v_ref[0:n_keys, :], preferred_element_type=F32)
        o_ref[i * t:(i + 1) * t, :] = (_rms(o, subln_g_ref[...]) * (1.0 - lambda_init)).astype(o_ref.dtype)


def _diff_attention(proj, bias_tiles, layer, lq1, lk1, lq2, lk2, subln_g, *, batch, seq, lambda_init):
    t = ATT_TILE
    vec = lambda n: _layer_vec_spec(layer, n, 2)
    spec = lambda blk: pl.BlockSpec((None, seq, HEAD_DIM), lambda b, h: (b, 0, blk + h))
    kern = functools.partial(_diff_kernel, lambda_init=lambda_init)
    return pl.pallas_call(
        kern,
        grid=(batch, DIFF_HEADS),
        in_specs=[
            vec(DIFF_QK_DIM), vec(DIFF_QK_DIM), vec(DIFF_QK_DIM), vec(DIFF_QK_DIM), vec(HEAD_DIM),
            pl.BlockSpec((None, 3, t, t), lambda b, h: (h, 0, 0, 0)),
            spec(DF_Q_BLK), spec(DF_K_BLK), spec(DF_V_BLK),
        ],
        out_specs=spec(0),
        out_shape=jax.ShapeDtypeStruct((batch, seq, DIFF_HEADS * HEAD_DIM), BF16),
        compiler_params=_params(2),
        name="diff_attention",
    )(_as_rows(lq1), _as_rows(lk1), _as_rows(lq2), _as_rows(lk2), _as_rows(subln_g), bias_tiles, proj, proj, proj)


def _ssm_prep_kernel(a_re_ref, a_im_ref, log_dt_ref, b_re_ref, b_im_ref, c_re_ref, c_im_ref,
                     lam_ref, bbar_ref, cmat_ref):
    ar, ai = a_re_ref[...], a_im_ref[...]
    dt = jnp.exp(log_dt_ref[...])
    mag = jnp.exp(ar * dt)
    lb_re, lb_im = mag * jnp.cos(ai * dt), mag * jnp.sin(ai * dt)
    den = ar * ar + ai * ai
    f_re = ((lb_re - 1.0) * ar + lb_im * ai) / den
    f_im = (lb_im * ar - (lb_re - 1.0) * ai) / den
    lam_ref[0:1, :] = lb_re
    lam_ref[1:2, :] = lb_im
    br, bi = b_re_ref[...], b_im_ref[...]
    nb = br.shape[1]
    bbar_ref[:, 0:nb] = (f_re * br - f_im * bi).astype(BF16)
    bbar_ref[:, nb:2 * nb] = (f_re * bi + f_im * br).astype(BF16)
    cmat_ref[0:nb, :] = c_re_ref[...].astype(BF16)
    cmat_ref[nb:2 * nb, :] = (-c_im_ref[...]).astype(BF16)


def _block_diag_groups(x):
    gpb = SSM_GROUPS // SSM_STATE_BLOCKS
    nl, g, r, k = x.shape
    xb = x.reshape(nl, SSM_STATE_BLOCKS, gpb, r, k)
    eye = jnp.eye(gpb, dtype=x.dtype)
    return jnp.einsum("lagrk,gh->lagrhk", xb, eye).reshape(nl, SSM_STATE_BLOCKS, gpb * r, gpb * k)


def _ssm_prep(a_re, a_im, log_dt, b_re, b_im, c_re, c_im):
    n_layers = a_re.shape[0]
    n_state = SSM_GROUPS * SSM_STATE
    nb = n_state // SSM_STATE_BLOCKS
    cb = SSM_WIDTH // SSM_STATE_BLOCKS
    row = lambda x: x.reshape(n_layers, 1, n_state)
    log_dt_row = jnp.broadcast_to(log_dt[:, :, None], (n_layers, SSM_GROUPS, SSM_STATE))
    b_args = [_block_diag_groups(jnp.swapaxes(b, 2, 3)) for b in (b_re, b_im)]
    c_args = [_block_diag_groups(jnp.swapaxes(c, 2, 3)) for c in (c_re, c_im)]
    vec = pl.BlockSpec((None, 1, nb), lambda l, s: (l, 0, s))
    return pl.pallas_call(
        _ssm_prep_kernel,
        grid=(n_layers, SSM_STATE_BLOCKS),
        in_specs=[vec, vec, vec,
                  pl.BlockSpec((None, None, cb, nb), lambda l, s: (l, s, 0, 0)),
                  pl.BlockSpec((None, None, cb, nb), lambda l, s: (l, s, 0, 0)),
                  pl.BlockSpec((None, None, nb, cb), lambda l, s: (l, s, 0, 0)),
                  pl.BlockSpec((None, None, nb, cb), lambda l, s: (l, s, 0, 0))],
        out_specs=[pl.BlockSpec((None, 2, nb), lambda l, s: (l, 0, s)),
                   pl.BlockSpec((None, None, cb, 2 * nb), lambda l, s: (l, s, 0, 0)),
                   pl.BlockSpec((None, None, 2 * nb, cb), lambda l, s: (l, s, 0, 0))],
        out_shape=[jax.ShapeDtypeStruct((n_layers, 2, n_state), F32),
                   jax.ShapeDtypeStruct((n_layers, SSM_STATE_BLOCKS, cb, 2 * nb), BF16),
                   jax.ShapeDtypeStruct((n_layers, SSM_STATE_BLOCKS, 2 * nb, cb), BF16)],
        compiler_params=_params(2),
        name="ssm_prep",
    )(row(a_re), row(a_im), row(log_dt_row), *b_args, *c_args)


def _gelu_tanh(x):
    return 0.5 * x * (1.0 + jnp.tanh(math.sqrt(2.0 / math.pi) * (x + 0.044715 * (x * x * x))))


def _ssm_kernel(u_ref, lam_ref, bbar_ref, cmat_ref, d_ref, wglu_ref, bglu_ref, o_ref,
                state_ref, bu_ref, y_ref, wglu_bf_ref, *, t_chunk):
    c = pl.program_id(0)
    p = SSM_BATCH_PAD
    nb = lam_ref.shape[1] // SSM_STATE_BLOCKS
    cb = SSM_WIDTH // SSM_STATE_BLOCKS

    @pl.when(c == 0)
    def _():
        state_ref[...] = jnp.zeros_like(state_ref)
        wglu_bf_ref[...] = wglu_ref[...].astype(BF16)

    for s in range(SSM_STATE_BLOCKS):
        u_s = u_ref[:, s * cb:(s + 1) * cb].astype(BF16)
        bu_ref[...] = jnp.dot(u_s, bbar_ref[s], preferred_element_type=F32)
        ar = jnp.broadcast_to(lam_ref[0:1, s * nb:(s + 1) * nb], (p, nb))
        ai = jnp.broadcast_to(lam_ref[1:2, s * nb:(s + 1) * nb], (p, nb))
        re_cols = slice(s * 2 * nb, s * 2 * nb + nb)
        im_cols = slice(s * 2 * nb + nb, (s + 1) * 2 * nb)

        def step(tt, carry):
            xr, xi = carry
            rows = pl.ds(pl.multiple_of(tt * p, p), p)
            nxr = ar * xr - ai * xi + bu_ref[rows, 0:nb]
            nxi = ar * xi + ai * xr + bu_ref[rows, nb:2 * nb]
            bu_ref[rows, 0:nb] = nxr
            bu_ref[rows, nb:2 * nb] = nxi
            return nxr, nxi

        xr, xi = lax.fori_loop(0, t_chunk, step, (state_ref[:, re_cols], state_ref[:, im_cols]), unroll=8)
        state_ref[:, re_cols] = xr
        state_ref[:, im_cols] = xi
        y_ref[:, s * cb:(s + 1) * cb] = jnp.dot(bu_ref[...].astype(BF16), cmat_ref[s],
                                                preferred_element_type=F32)

    y = _gelu_tanh(y_ref[...] + d_ref[...] * u_ref[...])
    gate = jax.nn.sigmoid(jnp.dot(y.astype(BF16), wglu_bf_ref[...], preferred_element_type=F32) + bglu_ref[...])
    o_ref[...] = (y * gate).astype(o_ref.dtype)


def _ssm(u_pad, layer, lam, bbar, cmat, d_skip, w_glu, b_glu, *, seq, t_chunk):
    p = SSM_BATCH_PAD
    n_state = lam.shape[2]
    nb = n_state // SSM_STATE_BLOCKS
    rows = t_chunk * p
    layer_block = lambda shape: pl.BlockSpec((None,) + shape[1:], lambda c: (layer,) + (0,) * (len(shape) - 1))
    kern = functools.partial(_ssm_kernel, t_chunk=t_chunk)
    d_rows = d_skip.reshape(d_skip.shape[0], 1, SSM_WIDTH)
    return pl.pallas_call(
        kern,
        grid=(seq // t_chunk,),
        in_specs=[
            pl.BlockSpec((rows, SSM_WIDTH), lambda c: (c, 0)),
            layer_block(lam.shape), layer_block(bbar.shape), layer_block(cmat.shape),
            layer_block(d_rows.shape), layer_block(w_glu.shape), layer_block((b_glu.shape[0], 1, SSM_WIDTH)),
        ],
        out_specs=pl.BlockSpec((rows, SSM_WIDTH), lambda c: (c, 0)),
        out_shape=jax.ShapeDtypeStruct((seq * p, SSM_WIDTH), BF16),
        scratch_shapes=[
            pltpu.VMEM((p, 2 * n_state), F32),
            pltpu.VMEM((rows, 2 * nb), F32),
            pltpu.VMEM((rows, SSM_WIDTH), F32),
            pltpu.VMEM((SSM_WIDTH, SSM_WIDTH), BF16),
        ],
        compiler_params=_params(1),
        name="ssm_scan",
    )(u_pad, lam, bbar, cmat, d_rows, w_glu, _as_rows(b_glu))


def _cast_kernel(x_ref, o_ref):
    o_ref[...] = x_ref[...].astype(o_ref.dtype)


def _cast_bf16(w, *, rows):
    nl, r, c = w.shape
    return pl.pallas_call(
        _cast_kernel,
        grid=(nl, r // rows),
        in_specs=[pl.BlockSpec((None, rows, c), lambda l, i: (l, i, 0))],
        out_specs=pl.BlockSpec((None, rows, c), lambda l, i: (l, i, 0)),
        out_shape=jax.ShapeDtypeStruct(w.shape, BF16),
        compiler_params=_params(2),
        name="cast_bf16",
    )(w)


def _outproj_kernel(h_ref, sb_ref, ssm_ref, df_ref, w_ref, g_ref, o_ref, mixed_ref):
    w_sb, w_ssm = sb_ref.shape[1], ssm_ref.shape[1]
    mixed_ref[:, 0:w_sb] = sb_ref[...]
    mixed_ref[:, w_sb:w_sb + w_ssm] = ssm_ref[...]
    mixed_ref[:, w_sb + w_ssm:] = df_ref[...]
    y = jnp.dot(mixed_ref[...], w_ref[...], preferred_element_type=F32)
    o_ref[...] = h_ref[...] + _rms(y, g_ref[...])


def _outproj(h, o_sb, o_ssm, o_diff, layer, w_out_bf, g, *, tm):
    m, d = h.shape
    k = w_out_bf.shape[1]
    row_block = lambda x: pl.BlockSpec((tm, x.shape[1]), lambda i: (i, 0))
    return pl.pallas_call(
        _outproj_kernel,
        grid=(m // tm,),
        in_specs=[
            row_block(h), row_block(o_sb), row_block(o_ssm), row_block(o_diff),
            pl.BlockSpec((None, k, d), lambda i: (layer, 0, 0), pipeline_mode=pl.Buffered(1)),
            _layer_vec_spec(layer, d, 1),
        ],
        out_specs=row_block(h),
        out_shape=jax.ShapeDtypeStruct((m, d), F32),
        scratch_shapes=[pltpu.VMEM((tm, k), BF16)],
        compiler_params=_params(1),
        name="outproj",
    )(h, o_sb, o_ssm, o_diff, w_out_bf, _as_rows(g))


def _trunk(x, ffn1_pre_g, ffn1_w_gate, ffn1_w_up, ffn1_w_down, ffn1_post_g, mix_pre_g, w_in, ssm_a_re, ssm_a_im,
           ssm_log_dt, ssm_b_re, ssm_b_im, ssm_c_re, ssm_c_im, ssm_d, ssm_w_glu, ssm_b_glu, diff_lq1, diff_lk1,
           diff_lq2, diff_lk2, diff_subln_g, rel_bias, w_out, mix_post_g, ffn2_pre_g, ffn2_w_gate, ffn2_w_up,
           ffn2_w_down, ffn2_post_g, *, tm, tf, tm_out, t_chunk):
    batch, seq, d = x.shape
    m = batch * seq
    depth = w_in.shape[0]
    h = x.reshape(m, d)
    bias_tiles = _t5_bias_tiles(rel_bias)
    lam, bbar, cmat = _ssm_prep(ssm_a_re, ssm_a_im, ssm_log_dt, ssm_b_re, ssm_b_im, ssm_c_re, ssm_c_im)
    w_out_bf = _cast_bf16(w_out, rows=512)
    for l in range(depth):
        h = _ffn(h, l, ffn1_pre_g, ffn1_w_gate, ffn1_w_up, ffn1_w_down, ffn1_post_g, tm=tm, tf=tf)

        proj, ssm_u = _inproj(h, l, mix_pre_g, w_in, tm=tm)
        proj = proj.reshape(batch, seq, PROJ_COLS)

        o_sb = _sb_attention(proj, batch=batch, seq=seq)

        u_pad = jnp.pad(jnp.swapaxes(ssm_u.reshape(batch, seq, SSM_WIDTH), 0, 1),
                        ((0, 0), (0, SSM_BATCH_PAD - batch), (0, 0))).reshape(seq * SSM_BATCH_PAD, SSM_WIDTH)
        o_ssm = _ssm(u_pad, l, lam, bbar, cmat, ssm_d, ssm_w_glu, ssm_b_glu, seq=seq, t_chunk=t_chunk)
        o_ssm = jnp.swapaxes(o_ssm.reshape(seq, SSM_BATCH_PAD, SSM_WIDTH)[:, :batch], 0, 1).reshape(m, SSM_WIDTH)

        lambda_init = 0.8 - 0.6 * math.exp(-0.3 * l)
        o_diff = _diff_attention(proj, bias_tiles, l, diff_lq1, diff_lk1, diff_lq2, diff_lk2, diff_subln_g,
                                 batch=batch, seq=seq, lambda_init=lambda_init)

        h = _outproj(h, o_sb.reshape(m, -1), o_ssm, o_diff.reshape(m, -1), l, w_out_bf, mix_post_g, tm=tm_out)

        h = _ffn(h, l, ffn2_pre_g, ffn2_w_gate, ffn2_w_up, ffn2_w_down, ffn2_post_g, tm=tm, tf=tf)
    return h.reshape(batch, seq, d)


def kernel(x, ffn1_pre_g, ffn1_w_gate, ffn1_w_up, ffn1_w_down, ffn1_post_g, mix_pre_g, w_in, ssm_a_re, ssm_a_im, ssm_log_dt, ssm_b_re, ssm_b_im, ssm_c_re, ssm_c_im, ssm_d, ssm_w_glu, ssm_b_glu, diff_lq1, diff_lk1, diff_lq2, diff_lk2, diff_subln_g, rel_bias, w_out, mix_post_g, ffn2_pre_g, ffn2_w_gate, ffn2_w_up, ffn2_w_down, ffn2_post_g):
    assert x.shape[0] <= SSM_BATCH_PAD and x.shape[1] % ATT_TILE == 0
    return _trunk(x, ffn1_pre_g, ffn1_w_gate, ffn1_w_up, ffn1_w_down, ffn1_post_g, mix_pre_g, w_in, ssm_a_re,
                  ssm_a_im, ssm_log_dt, ssm_b_re, ssm_b_im, ssm_c_re, ssm_c_im, ssm_d, ssm_w_glu, ssm_b_glu,
                  diff_lq1, diff_lk1, diff_lq2, diff_lk2, diff_subln_g, rel_bias, w_out, mix_post_g, ffn2_pre_g,
                  ffn2_w_gate, ffn2_w_up, ffn2_w_down, ffn2_post_g, tm=1024, tf=256, tm_out=512, t_chunk=128)
```

```python
import functools
import math

import jax
import jax.numpy as jnp
from jax import lax
from jax.experimental import pallas as pl
from jax.experimental.pallas import tpu as pltpu

F32 = jnp.float32
BF16 = jnp.bfloat16

HEAD_DIM = 128
SB_HEADS = 6
DIFF_HEADS = 6
DIFF_QK_DIM = HEAD_DIM // 2
SSM_GROUPS = 32
SSM_CH_PER_GROUP = 16
SSM_STATE = 64
SSM_WIDTH = SSM_GROUPS * SSM_CH_PER_GROUP
NUM_BUCKETS = 32
MAX_DISTANCE = 128
NORM_EPS = 1e-6
FFN_RESIDUAL_WEIGHT = 0.5
LOG2E = 1.0 / math.log(2.0)

SB_Q_BLK, SB_K_BLK, SB_V_BLK = 0, 6, 12
DF_Q_BLK, DF_K_BLK, DF_V_BLK = 18, 24, 30
PROJ_COLS = 36 * HEAD_DIM
SSM_COL0 = 3 * SB_HEADS * HEAD_DIM
W_IN_SUB = 256
PROJ_STEP_COLS = 3 * W_IN_SUB

V7X_VMEM_LIMIT = 56 * 1024 * 1024
FFN_SUB = 256
ROW_CHUNK = 128
ATT_TILE = 256
SSM_BATCH_PAD = 8
SSM_STATE_BLOCKS = 4
NEG_INF = float(jnp.finfo(jnp.float32).min)


def _params(n_grid, vmem=V7X_VMEM_LIMIT):
    return pltpu.CompilerParams(dimension_semantics=("arbitrary",) * n_grid, vmem_limit_bytes=vmem)


def _rms(x, g):
    ms = jnp.mean(x * x, axis=-1, keepdims=True)
    return x * lax.rsqrt(ms + NORM_EPS) * g


def _for_row_chunks(n_rows, chunk, body):
    def step(i, carry):
        body(pl.multiple_of(i * chunk, chunk))
        return carry
    lax.fori_loop(0, n_rows // chunk, step, 0)


def _layer_vec_spec(layer, n, n_grid):
    zeros = (0,) * 2
    return pl.BlockSpec((None, 1, n), lambda *_: (layer,) + zeros)


def _as_rows(p):
    return p.reshape(p.shape[0], 1, p.shape[1])


def _ffn_kernel(x_ref, pre_g_ref, wg_ref, wu_ref, wd_ref, post_g_ref, o_ref, xn_ref, a_ref, *, n_f, n_col_chunks):
    f = pl.program_id(1)
    tm, d = x_ref.shape

    @pl.when(f == 0)
    def _():
        def norm_rows(r0):
            rows = pl.ds(r0, ROW_CHUNK)
            xn_ref[rows, :] = _rms(x_ref[rows, :], pre_g_ref[...]).astype(BF16)
            o_ref[rows, :] = jnp.zeros((ROW_CHUNK, d), F32)
        _for_row_chunks(tm, ROW_CHUNK, norm_rows)

    xn = xn_ref[...]
    tf = wg_ref.shape[1]
    for hf in range(tf // FFN_SUB):
        sub = slice(hf * FFN_SUB, (hf + 1) * FFN_SUB)
        g = jnp.dot(xn, wg_ref[:, sub].astype(BF16), preferred_element_type=F32)
        u = jnp.dot(xn, wu_ref[:, sub].astype(BF16), preferred_element_type=F32)
        a_ref[:, sub] = (g * jax.nn.sigmoid(g) * u).astype(BF16)
    a = a_ref[...]
    cw = d // n_col_chunks
    for c in range(n_col_chunks):
        cols = slice(c * cw, (c + 1) * cw)
        o_ref[:, cols] += jnp.dot(a, wd_ref[:, cols].astype(BF16), preferred_element_type=F32)

    @pl.when(f == n_f - 1)
    def _():
        def finish_rows(r0):
            rows = pl.ds(r0, ROW_CHUNK)
            o_ref[rows, :] = x_ref[rows, :] + FFN_RESIDUAL_WEIGHT * _rms(o_ref[rows, :], post_g_ref[...])
        _for_row_chunks(tm, ROW_CHUNK, finish_rows)


def _ffn(h, layer, pre_g, w_gate, w_up, w_down, post_g, *, tm, tf):
    m, d = h.shape
    d_ff = w_gate.shape[2]
    n_f = d_ff // tf
    kern = functools.partial(_ffn_kernel, n_f=n_f, n_col_chunks=4)
    return pl.pallas_call(
        kern,
        grid=(m // tm, n_f),
        in_specs=[
            pl.BlockSpec((tm, d), lambda i, f: (i, 0), pipeline_mode=pl.Buffered(1)),
            _layer_vec_spec(layer, d, 2),
            pl.BlockSpec((None, d, tf), lambda i, f: (layer, 0, f)),
            pl.BlockSpec((None, d, tf), lambda i, f: (layer, 0, f)),
            pl.BlockSpec((None, tf, d), lambda i, f: (layer, f, 0)),
            _layer_vec_spec(layer, d, 2),
        ],
        out_specs=pl.BlockSpec((tm, d), lambda i, f: (i, 0), pipeline_mode=pl.Buffered(1)),
        out_shape=jax.ShapeDtypeStruct((m, d), F32),
        scratch_shapes=[pltpu.VMEM((tm, d), BF16), pltpu.VMEM((tm, tf), BF16)],
        compiler_params=_params(2),
        name="ffn",
    )(h, _as_rows(pre_g), w_gate, w_up, w_down, _as_rows(post_g))


def _inproj_kernel(x_ref, g_ref, w0_ref, w1_ref, w2_ref, proj_ref, u_ref, xn_ref, *, n_head_steps):
    j = pl.program_id(1)
    tm = x_ref.shape[0]
    sub = W_IN_SUB

    @pl.when(j == 0)
    def _():
        def norm_rows(r0):
            rows = pl.ds(r0, ROW_CHUNK)
            xn_ref[rows, :] = _rms(x_ref[rows, :], g_ref[...]).astype(BF16)
        _for_row_chunks(tm, ROW_CHUNK, norm_rows)

    @pl.when(j < n_head_steps)
    def _():
        xn = xn_ref[...]
        for c, w_ref in enumerate((w0_ref, w1_ref, w2_ref)):
            y = jnp.dot(xn, w_ref[...].astype(BF16), preferred_element_type=F32)
            proj_ref[:, c * sub:(c + 1) * sub] = y.astype(proj_ref.dtype)

    @pl.when(j == n_head_steps)
    def _():
        xn = xn_ref[...]
        for c, w_ref in enumerate((w0_ref, w1_ref)):
            u_ref[:, c * sub:(c + 1) * sub] = jnp.dot(xn, w_ref[...].astype(BF16), preferred_element_type=F32)


def _inproj(h, layer, g, w_in, *, tm):
    m, d = h.shape
    n_head_steps = PROJ_COLS // PROJ_STEP_COLS
    ssm_blk0 = SSM_COL0 // W_IN_SUB
    n_ssm_blks = SSM_WIDTH // W_IN_SUB
    last_blk = w_in.shape[2] // W_IN_SUB - 1

    def w_spec(c):
        def index(i, j):
            head_blk = 3 * j + c + jnp.where(3 * j + c >= ssm_blk0, n_ssm_blks, 0)
            blk = jnp.where(j < n_head_steps, head_blk, ssm_blk0 + c)
            return (layer, 0, jnp.minimum(blk, last_blk))
        return pl.BlockSpec((None, d, W_IN_SUB), index)

    kern = functools.partial(_inproj_kernel, n_head_steps=n_head_steps)
    return pl.pallas_call(
        kern,
        grid=(m // tm, n_head_steps + 1),
        in_specs=[
            pl.BlockSpec((tm, d), lambda i, j: (i, 0), pipeline_mode=pl.Buffered(1)),
            _layer_vec_spec(layer, d, 2),
            w_spec(0), w_spec(1), w_spec(2),
        ],
        out_specs=[
            pl.BlockSpec((tm, PROJ_STEP_COLS), lambda i, j: (i, jnp.minimum(j, n_head_steps - 1))),
            pl.BlockSpec((tm, SSM_WIDTH), lambda i, j: (i, 0)),
        ],
        out_shape=[jax.ShapeDtypeStruct((m, PROJ_COLS), BF16), jax.ShapeDtypeStruct((m, SSM_WIDTH), F32)],
        scratch_shapes=[pltpu.VMEM((tm, d), BF16)],
        compiler_params=_params(2),
        name="inproj",
    )(h, _as_rows(g), w_in, w_in, w_in)


def _split2_bf16(x):
    hi = x.astype(BF16)
    lo = (x - hi.astype(F32)).astype(BF16)
    return hi, lo


def _mask_first_tile(x, keep):
    t = ATT_TILE
    head = jnp.where(keep, x[0:t], 0.0)
    return head if x.shape[0] == t else jnp.concatenate([head, x[t:]], axis=0)


def _sb_kernel(q_ref, k_ref, v_ref, o_ref):
    t = ATT_TILE
    seq = q_ref.shape[0]
    n_tiles = seq // t
    scale2 = LOG2E / math.sqrt(HEAD_DIM)
    row = lax.broadcasted_iota(jnp.int32, (t, t), 0)
    col = lax.broadcasted_iota(jnp.int32, (t, t), 1)
    strictly_lower = row > col
    ones = jnp.where(strictly_lower, 1.0, 0.0).astype(BF16)
    suffix_ones = jnp.concatenate([ones, ones], axis=0)
    acc = [None] * n_tiles
    later = [None] * n_tiles

    for j in range(n_tiles - 1, -1, -1):
        r0 = j * t
        k = k_ref[r0:r0 + t, :]
        v = v_ref[r0:r0 + t, :]
        z = lax.dot_general(q_ref[r0:seq, :], k, (((1,), (1,)), ((), ())), preferred_element_type=F32) * scale2
        pos = jnp.maximum(z, 0.0)
        neg = jnp.minimum(z, 0.0)
        l2 = jnp.log2(1.0 + jnp.exp2(neg - pos))
        sp = pos + l2
        log_beta = neg - l2
        sp_in = _mask_first_tile(sp, strictly_lower)
        hi, lo = _split2_bf16(sp_in)
        suffix = jnp.dot(jnp.concatenate([hi, lo], axis=1), suffix_ones, preferred_element_type=F32)
        later_rows = jnp.concatenate([jnp.zeros((t, 1), F32)] + later[j + 1:], axis=0)
        w = jnp.exp2(log_beta - suffix - later_rows)
        w = _mask_first_tile(w, strictly_lower)
        pv = jnp.dot(w.astype(BF16), v, preferred_element_type=F32)
        row_sum = jnp.sum(sp_in, axis=1, keepdims=True)
        for i in range(j, n_tiles):
            rows = slice((i - j) * t, (i - j + 1) * t)
            acc[i] = pv[rows] if i == j else acc[i] + pv[rows]
            later[i] = row_sum[rows] if i == j else later[i] + row_sum[rows]

    for i in range(n_tiles):
        o_ref[i * t:(i + 1) * t, :] = acc[i].astype(o_ref.dtype)


def _sb_attention(proj, *, batch, seq):
    spec = lambda blk: pl.BlockSpec((None, seq, HEAD_DIM), lambda b, h: (b, 0, blk + h))
    return pl.pallas_call(
        _sb_kernel,
        grid=(batch, SB_HEADS),
        in_specs=[spec(SB_Q_BLK), spec(SB_K_BLK), spec(SB_V_BLK)],
        out_specs=spec(0),
        out_shape=jax.ShapeDtypeStruct((batch, seq, SB_HEADS * HEAD_DIM), BF16),
        compiler_params=_params(2),
        name="sb_attention",
    )(proj, proj, proj)


def _t5_bias_kernel(rel_ref, o_ref):
    t = ATT_TILE
    h = pl.program_id(0)
    row = lax.broadcasted_iota(jnp.int32, (t, t), 0)
    col = lax.broadcasted_iota(jnp.int32, (t, t), 1)
    max_exact = NUM_BUCKETS // 2
    for d in range(3):
        n = jnp.maximum(d * t + row - col, 0)
        nf = jnp.maximum(n, 1).astype(F32)
        large = max_exact + (jnp.log(nf / max_exact) / math.log(MAX_DISTANCE / max_exact)
                             * (NUM_BUCKETS - max_exact)).astype(jnp.int32)
        large = jnp.minimum(large, NUM_BUCKETS - 1)
        bucket = jnp.where(n < max_exact, n, large)
        bias = jnp.zeros((t, t), F32)
        for b in range(NUM_BUCKETS):
            bias = jnp.where(bucket == b, rel_ref[b, h], bias)
        o_ref[d] = bias * LOG2E


def _t5_bias_tiles(rel_bias):
    t = ATT_TILE
    return pl.pallas_call(
        _t5_bias_kernel,
        grid=(DIFF_HEADS,),
        in_specs=[pl.BlockSpec(memory_space=pltpu.SMEM)],
        out_specs=pl.BlockSpec((None, 3, t, t), lambda h: (h, 0, 0, 0)),
        out_shape=jax.ShapeDtypeStruct((DIFF_HEADS, 3, t, t), F32),
        compiler_params=_params(1),
        name="t5_bias_tiles",
    )(rel_bias)


def _diff_kernel(lq1_ref, lk1_ref, lq2_ref, lk2_ref, subln_g_ref, bias_ref, q_ref, k_ref, v_ref, o_ref,
                 *, lambda_init):
    t = ATT_TILE
    seq = q_ref.shape[0]
    n_tiles = seq // t
    scale2 = LOG2E / math.sqrt(DIFF_QK_DIM)
    lane = lax.broadcasted_iota(jnp.int32, (t, HEAD_DIM), 1)
    row = lax.broadcasted_iota(jnp.int32, (t, t), 0)
    col = lax.broadcasted_iota(jnp.int32, (t, t), 1)
    causal = col <= row
    lam = (jnp.exp(jnp.sum(lq1_ref[...] * lk1_ref[...], axis=1, keepdims=True))
           - jnp.exp(jnp.sum(lq2_ref[...] * lk2_ref[...], axis=1, keepdims=True)) + lambda_init)

    for i in range(n_tiles):
        n_keys = (i + 1) * t
        q = q_ref[i * t:(i + 1) * t, :]
        zero = jnp.zeros_like(q)
        q_maps = (jnp.where(lane < DIFF_QK_DIM, q, zero), jnp.where(lane >= DIFF_QK_DIM, q, zero))
        k = k_ref[0:n_keys, :]
        exps, sums = [], []
        for qm in q_maps:
            raw = lax.dot_general(qm, k, (((1,), (1,)), ((), ())), preferred_element_type=F32)
            parts = []
            for j in range(i + 1):
                s_j = raw[:, j * t:(j + 1) * t] * scale2 + bias_ref[min(i - j, 2)]
                parts.append(jnp.where(causal, s_j, NEG_INF) if j == i else s_j)
            s = parts[0] if i == 0 else jnp.concatenate(parts, axis=1)
            e = jnp.exp2(s - jnp.max(s, axis=1, keepdims=True))
            exps.append(e)
            sums.append(jnp.sum(e, axis=1, keepdims=True))
        p = exps[0] * (1.0 / sums[0]) - exps[1] * (lam / sums[1])
        o = jnp.dot(p.astype(BF16), v_ref[0:n_keys, :], preferred_element_type=F32)
        o_ref[i * t:(i + 1) * t, :] = (_rms(o, subln_g_ref[...]) * (1.0 - lambda_init)).astype(o_ref.dtype)


def _diff_attention(proj, bias_tiles, layer, lq1, lk1, lq2, lk2, subln_g, *, batch, seq, lambda_init):
    t = ATT_TILE
    vec = lambda n: _layer_vec_spec(layer, n, 2)
    spec = lambda blk: pl.BlockSpec((None, seq, HEAD_DIM), lambda b, h: (b, 0, blk + h))
    kern = functools.partial(_diff_kernel, lambda_init=lambda_init)
    return pl.pallas_call(
        kern,
        grid=(batch, DIFF_HEADS),
        in_specs=[
            vec(DIFF_QK_DIM), vec(DIFF_QK_DIM), vec(DIFF_QK_DIM), vec(DIFF_QK_DIM), vec(HEAD_DIM),
            pl.BlockSpec((None, 3, t, t), lambda b, h: (h, 0, 0, 0)),
            spec(DF_Q_BLK), spec(DF_K_BLK), spec(DF_V_BLK),
        ],
        out_specs=spec(0),
        out_shape=jax.ShapeDtypeStruct((batch, seq, DIFF_HEADS * HEAD_DIM), BF16),
        compiler_params=_params(2),
        name="diff_attention",
    )(_as_rows(lq1), _as_rows(lk1), _as_rows(lq2), _as_rows(lk2), _as_rows(subln_g), bias_tiles, proj, proj, proj)


def _ssm_prep_kernel(a_re_ref, a_im_ref, log_dt_ref, b_re_ref, b_im_ref, c_re_ref, c_im_ref,
                     lam_ref, bbar_ref, cmat_ref):
    ar, ai = a_re_ref[...], a_im_ref[...]
    dt = jnp.exp(log_dt_ref[...])
    mag = jnp.exp(ar * dt)
    lb_re, lb_im = mag * jnp.cos(ai * dt), mag * jnp.sin(ai * dt)
    den = ar * ar + ai * ai
    f_re = ((lb_re - 1.0) * ar + lb_im * ai) / den
    f_im = (lb_im * ar - (lb_re - 1.0) * ai) / den
    lam_ref[0:1, :] = lb_re
    lam_ref[1:2, :] = lb_im
    br, bi = b_re_ref[...], b_im_ref[...]
    nb = br.shape[1]
    bbar_ref[:, 0:nb] = (f_re * br - f_im * bi).astype(BF16)
    bbar_ref[:, nb:2 * nb] = (f_re * bi + f_im * br).astype(BF16)
    cmat_ref[0:nb, :] = c_re_ref[...].astype(BF16)
    cmat_ref[nb:2 * nb, :] = (-c_im_ref[...]).astype(BF16)


def _block_diag_groups(x):
    gpb = SSM_GROUPS // SSM_STATE_BLOCKS
    nl, g, r, k = x.shape
    xb = x.reshape(nl, SSM_STATE_BLOCKS, gpb, r, k)
    eye = jnp.eye(gpb, dtype=x.dtype)
    return jnp.einsum("lagrk,gh->lagrhk", xb, eye).reshape(nl, SSM_STATE_BLOCKS, gpb * r, gpb * k)


def _ssm_prep(a_re, a_im, log_dt, b_re, b_im, c_re, c_im):
    n_layers = a_re.shape[0]
    n_state = SSM_GROUPS * SSM_STATE
    nb = n_state // SSM_STATE_BLOCKS
    cb = SSM_WIDTH // SSM_STATE_BLOCKS
    row = lambda x: x.reshape(n_layers, 1, n_state)
    log_dt_row = jnp.broadcast_to(log_dt[:, :, None], (n_layers, SSM_GROUPS, SSM_STATE))
    b_args = [_block_diag_groups(jnp.swapaxes(b, 2, 3)) for b in (b_re, b_im)]
    c_args = [_block_diag_groups(jnp.swapaxes(c, 2, 3)) for c in (c_re, c_im)]
    vec = pl.BlockSpec((None, 1, nb), lambda l, s: (l, 0, s))
    return pl.pallas_call(
        _ssm_prep_kernel,
        grid=(n_layers, SSM_STATE_BLOCKS),
        in_specs=[vec, vec, vec,
                  pl.BlockSpec((None, None, cb, nb), lambda l, s: (l, s, 0, 0)),
                  pl.BlockSpec((None, None, cb, nb), lambda l, s: (l, s, 0, 0)),
                  pl.BlockSpec((None, None, nb, cb), lambda l, s: (l, s, 0, 0)),
                  pl.BlockSpec((None, None, nb, cb), lambda l, s: (l, s, 0, 0))],
        out_specs=[pl.BlockSpec((None, 2, nb), lambda l, s: (l, 0, s)),
                   pl.BlockSpec((None, None, cb, 2 * nb), lambda l, s: (l, s, 0, 0)),
                   pl.BlockSpec((None, None, 2 * nb, cb), lambda l, s: (l, s, 0, 0))],
        out_shape=[jax.ShapeDtypeStruct((n_layers, 2, n_state), F32),
                   jax.ShapeDtypeStruct((n_layers, SSM_STATE_BLOCKS, cb, 2 * nb), BF16),
                   jax.ShapeDtypeStruct((n_layers, SSM_STATE_BLOCKS, 2 * nb, cb), BF16)],
        compiler_params=_params(2),
        name="ssm_prep",
    )(row(a_re), row(a_im), row(log_dt_row), *b_args, *c_args)


def _gelu_tanh(x):
    return 0.5 * x * (1.0 + jnp.tanh(math.sqrt(2.0 / math.pi) * (x + 0.044715 * (x * x * x))))


def _ssm_kernel(u_ref, lam_ref, bbar_ref, cmat_ref, d_ref, wglu_ref, bglu_ref, o_ref,
                state_ref, bu_ref, y_ref, wglu_bf_ref, *, t_chunk):
    c = pl.program_id(0)
    p = SSM_BATCH_PAD
    nb = lam_ref.shape[1] // SSM_STATE_BLOCKS
    cb = SSM_WIDTH // SSM_STATE_BLOCKS

    @pl.when(c == 0)
    def _():
        state_ref[...] = jnp.zeros_like(state_ref)
        wglu_bf_ref[...] = wglu_ref[...].astype(BF16)

    for s in range(SSM_STATE_BLOCKS):
        u_s = u_ref[:, s * cb:(s + 1) * cb].astype(BF16)
        bu_ref[...] = jnp.dot(u_s, bbar_ref[s], preferred_element_type=F32)
        ar = jnp.broadcast_to(lam_ref[0:1, s * nb:(s + 1) * nb], (p, nb))
        ai = jnp.broadcast_to(lam_ref[1:2, s * nb:(s + 1) * nb], (p, nb))
        re_cols = slice(s * 2 * nb, s * 2 * nb + nb)
        im_cols = slice(s * 2 * nb + nb, (s + 1) * 2 * nb)

        def step(tt, carry):
            xr, xi = carry
            rows = pl.ds(pl.multiple_of(tt * p, p), p)
            nxr = ar * xr - ai * xi + bu_ref[rows, 0:nb]
            nxi = ar * xi + ai * xr + bu_ref[rows, nb:2 * nb]
            bu_ref[rows, 0:nb] = nxr
            bu_ref[rows, nb:2 * nb] = nxi
            return nxr, nxi

        xr, xi = lax.fori_loop(0, t_chunk, step, (state_ref[:, re_cols], state_ref[:, im_cols]), unroll=8)
        state_ref[:, re_cols] = xr
        state_ref[:, im_cols] = xi
        y_ref[:, s * cb:(s + 1) * cb] = jnp.dot(bu_ref[...].astype(BF16), cmat_ref[s],
                                                preferred_element_type=F32)

    y = _gelu_tanh(y_ref[...] + d_ref[...] * u_ref[...])
    gate = jax.nn.sigmoid(jnp.dot(y.astype(BF16), wglu_bf_ref[...], preferred_element_type=F32) + bglu_ref[...])
    o_ref[...] = (y * gate).astype(o_ref.dtype)


def _ssm(u_pad, layer, lam, bbar, cmat, d_skip, w_glu, b_glu, *, seq, t_chunk):
    p = SSM_BATCH_PAD
    n_state = lam.shape[2]
    nb = n_state // SSM_STATE_BLOCKS
    rows = t_chunk * p
    layer_block = lambda shape: pl.BlockSpec((None,) + shape[1:], lambda c: (layer,) + (0,) * (len(shape) - 1))
    kern = functools.partial(_ssm_kernel, t_chunk=t_chunk)
    d_rows = d_skip.reshape(d_skip.shape[0], 1, SSM_WIDTH)
    return pl.pallas_call(
        kern,
        grid=(seq // t_chunk,),
        in_specs=[
            pl.BlockSpec((rows, SSM_WIDTH), lambda c: (c, 0)),
            layer_block(lam.shape), layer_block(bbar.shape), layer_block(cmat.shape),
            layer_block(d_rows.shape), layer_block(w_glu.shape), layer_block((b_glu.shape[0], 1, SSM_WIDTH)),
        ],
        out_specs=pl.BlockSpec((rows, SSM_WIDTH), lambda c: (c, 0)),
        out_shape=jax.ShapeDtypeStruct((seq * p, SSM_WIDTH), BF16),
        scratch_shapes=[
            pltpu.VMEM((p, 2 * n_state), F32),
            pltpu.VMEM((rows, 2 * nb), F32),
            pltpu.VMEM((rows, SSM_WIDTH), F32),
            pltpu.VMEM((SSM_WIDTH, SSM_WIDTH), BF16),
        ],
        compiler_params=_params(1),
        name="ssm_scan",
    )(u_pad, lam, bbar, cmat, d_rows, w_glu, _as_rows(b_glu))


def _cast_kernel(x_ref, o_ref):
    o_ref[...] = x_ref[...].astype(o_ref.dtype)


def _cast_bf16(w, *, rows):
    nl, r, c = w.shape
    return pl.pallas_call(
        _cast_kernel,
        grid=(nl, r // rows),
        in_specs=[pl.BlockSpec((None, rows, c), lambda l, i: (l, i, 0))],
        out_specs=pl.BlockSpec((None, rows, c), lambda l, i: (l, i, 0)),
        out_shape=jax.ShapeDtypeStruct(w.shape, BF16),
        compiler_params=_params(2),
        name="cast_bf16",
    )(w)


def _outproj_kernel(h_ref, sb_ref, ssm_ref, df_ref, w_ref, g_ref, o_ref, mixed_ref):
    w_sb, w_ssm = sb_ref.shape[1], ssm_ref.shape[1]
    mixed_ref[:, 0:w_sb] = sb_ref[...]
    mixed_ref[:, w_sb:w_sb + w_ssm] = ssm_ref[...]
    mixed_ref[:, w_sb + w_ssm:] = df_ref[...]
    y = jnp.dot(mixed_ref[...], w_ref[...], preferred_element_type=F32)
    o_ref[...] = h_ref[...] + _rms(y, g_ref[...])


def _outproj(h, o_sb, o_ssm, o_diff, layer, w_out_bf, g, *, tm):
    m, d = h.shape
    k = w_out_bf.shape[1]
    row_block = lambda x: pl.BlockSpec((tm, x.shape[1]), lambda i: (i, 0))
    return pl.pallas_call(
        _outproj_kernel,
        grid=(m // tm,),
        in_specs=[
            row_block(h), row_block(o_sb), row_block(o_ssm), row_block(o_diff),
            pl.BlockSpec((None, k, d), lambda i: (layer, 0, 0), pipeline_mode=pl.Buffered(1)),
            _layer_vec_spec(layer, d, 1),
        ],
        out_specs=row_block(h),
        out_shape=jax.ShapeDtypeStruct((m, d), F32),
        scratch_shapes=[pltpu.VMEM((tm, k), BF16)],
        compiler_params=_params(1),
        name="outproj",
    )(h, o_sb, o_ssm, o_diff, w_out_bf, _as_rows(g))


def _trunk(x, ffn1_pre_g, ffn1_w_gate, ffn1_w_up, ffn1_w_down, ffn1_post_g, mix_pre_g, w_in, ssm_a_re, ssm_a_im,
           ssm_log_dt, ssm_b_re, ssm_b_im, ssm_c_re, ssm_c_im, ssm_d, ssm_w_glu, ssm_b_glu, diff_lq1, diff_lk1,
           diff_lq2, diff_lk2, diff_subln_g, rel_bias, w_out, mix_post_g, ffn2_pre_g, ffn2_w_gate, ffn2_w_up,
           ffn2_w_down, ffn2_post_g, *, tm, tf, tm_out, t_chunk):
    batch, seq, d = x.shape
    m = batch * seq
    depth = w_in.shape[0]
    h = x.reshape(m, d)
    bias_tiles = _t5_bias_tiles(rel_bias)
    lam, bbar, cmat = _ssm_prep(ssm_a_re, ssm_a_im, ssm_log_dt, ssm_b_re, ssm_b_im, ssm_c_re, ssm_c_im)
    w_out_bf = _cast_bf16(w_out, rows=512)
    for l in range(depth):
        h = _ffn(h, l, ffn1_pre_g, ffn1_w_gate, ffn1_w_up, ffn1_w_down, ffn1_post_g, tm=tm, tf=tf)

        proj, ssm_u = _inproj(h, l, mix_pre_g, w_in, tm=tm)
        proj = proj.reshape(batch, seq, PROJ_COLS)

        o_sb = _sb_attention(proj, batch=batch, seq=seq)

        u_pad = jnp.pad(jnp.swapaxes(ssm_u.reshape(batch, seq, SSM_WIDTH), 0, 1),
                        ((0, 0), (0, SSM_BATCH_PAD - batch), (0, 0))).reshape(seq * SSM_BATCH_PAD, SSM_WIDTH)
        o_ssm = _ssm(u_pad, l, lam, bbar, cmat, ssm_d, ssm_w_glu, ssm_b_glu, seq=seq, t_chunk=t_chunk)
        o_ssm = jnp.swapaxes(o_ssm.reshape(seq, SSM_BATCH_PAD, SSM_WIDTH)[:, :batch], 0, 1).reshape(m, SSM_WIDTH)

        lambda_init = 0.8 - 0.6 * math.exp(-0.3 * l)
        o_diff = _diff_attention(proj, bias_tiles, l, diff_lq1, diff_lk1, diff_lq2, diff_lk2, diff_subln_g,
                                 batch=batch, seq=seq, lambda_init=lambda_init)

        h = _outproj(h, o_sb.reshape(m, -1), o_ssm, o_diff.reshape(m, -1), l, w_out_bf, mix_post_g, tm=tm_out)

        h = _ffn(h, l, ffn2_pre_g, ffn2_w_gate, ffn2_w_up, ffn2_w_down, ffn2_post_g, tm=tm, tf=tf)
    return h.reshape(batch, seq, d)


def kernel(x, ffn1_pre_g, ffn1_w_gate, ffn1_w_up, ffn1_w_down, ffn1_post_g, mix_pre_g, w_in, ssm_a_re, ssm_a_im, ssm_log_dt, ssm_b_re, ssm_b_im, ssm_c_re, ssm_c_im, ssm_d, ssm_w_glu, ssm_b_glu, diff_lq1, diff_lk1, diff_lq2, diff_lk2, diff_subln_g, rel_bias, w_out, mix_post_g, ffn2_pre_g, ffn2_w_gate, ffn2_w_up, ffn2_w_down, ffn2_post_g):
    assert x.shape[0] <= SSM_BATCH_PAD and x.shape[1] % ATT_TILE == 0
    return _trunk(x, ffn1_pre_g, ffn1_w_gate, ffn1_w_up, ffn1_w_down, ffn1_post_g, mix_pre_g, w_in, ssm_a_re,
                  ssm_a_im, ssm_log_dt, ssm_b_re, ssm_b_im, ssm_c_re, ssm_c_im, ssm_d, ssm_w_glu, ssm_b_glu,
                  diff_lq1, diff_lk1, diff_lq2, diff_lk2, diff_subln_g, rel_bias, w_out, mix_post_g, ffn2_pre_g,
                  ffn2_w_gate, ffn2_w_up, ffn2_w_down, ffn2_post_g, tm=1024, tf=512, tm_out=512, t_chunk=128)
```

```python
import functools
import math

import jax
import jax.numpy as jnp
from jax import lax
from jax.experimental import pallas as pl
from jax.experimental.pallas import tpu as pltpu

F32 = jnp.float32
BF16 = jnp.bfloat16

HEAD_DIM = 128
SB_HEADS = 6
DIFF_HEADS = 6
DIFF_QK_DIM = HEAD_DIM // 2
SSM_GROUPS = 32
SSM_CH_PER_GROUP = 16
SSM_STATE = 64
SSM_WIDTH = SSM_GROUPS * SSM_CH_PER_GROUP
NUM_BUCKETS = 32
MAX_DISTANCE = 128
NORM_EPS = 1e-6
FFN_RESIDUAL_WEIGHT = 0.5
LOG2E = 1.0 / math.log(2.0)

SB_Q_BLK, SB_K_BLK, SB_V_BLK = 0, 6, 12
DF_Q_BLK, DF_K_BLK, DF_V_BLK = 18, 24, 30
PROJ_COLS = 36 * HEAD_DIM
SSM_COL0 = 3 * SB_HEADS * HEAD_DIM
W_IN_SUB = 256
PROJ_STEP_COLS = 3 * W_IN_SUB

V7X_VMEM_LIMIT = 60 * 1024 * 1024
FFN_SUB = 256
ROW_CHUNK = 128
ATT_TILE = 256
SSM_BATCH_PAD = 8
SSM_STATE_BLOCKS = 4
NEG_INF = float(jnp.finfo(jnp.float32).min)


def _params(n_grid, vmem=V7X_VMEM_LIMIT):
    return pltpu.CompilerParams(dimension_semantics=("arbitrary",) * n_grid, vmem_limit_bytes=vmem)


def _rms(x, g):
    ms = jnp.mean(x * x, axis=-1, keepdims=True)
    return x * lax.rsqrt(ms + NORM_EPS) * g


def _for_row_chunks(n_rows, chunk, body):
    def step(i, carry):
        body(pl.multiple_of(i * chunk, chunk))
        return carry
    lax.fori_loop(0, n_rows // chunk, step, 0)


def _layer_vec_spec(layer, n, n_grid):
    zeros = (0,) * 2
    return pl.BlockSpec((None, 1, n), lambda *_: (layer,) + zeros)


def _as_rows(p):
    return p.reshape(p.shape[0], 1, p.shape[1])


def _ffn_kernel(x_ref, pre_g_ref, wg_ref, wu_ref, wd_ref, post_g_ref, o_ref, xn_ref, a_ref, *, n_f, n_col_chunks):
    f = pl.program_id(1)
    tm, d = x_ref.shape

    @pl.when(f == 0)
    def _():
        def norm_rows(r0):
            rows = pl.ds(r0, ROW_CHUNK)
            xn_ref[rows, :] = _rms(x_ref[rows, :], pre_g_ref[...]).astype(BF16)
            o_ref[rows, :] = jnp.zeros((ROW_CHUNK, d), F32)
        _for_row_chunks(tm, ROW_CHUNK, norm_rows)

    xn = xn_ref[...]
    tf = wg_ref.shape[1]
    for hf in range(tf // FFN_SUB):
        sub = slice(hf * FFN_SUB, (hf + 1) * FFN_SUB)
        g = jnp.dot(xn, wg_ref[:, sub].astype(BF16), preferred_element_type=F32)
        u = jnp.dot(xn, wu_ref[:, sub].astype(BF16), preferred_element_type=F32)
        a_ref[:, sub] = (g * jax.nn.sigmoid(g) * u).astype(BF16)
    a = a_ref[...]
    cw = d // n_col_chunks
    for c in range(n_col_chunks):
        cols = slice(c * cw, (c + 1) * cw)
        o_ref[:, cols] += jnp.dot(a, wd_ref[:, cols].astype(BF16), preferred_element_type=F32)

    @pl.when(f == n_f - 1)
    def _():
        def finish_rows(r0):
            rows = pl.ds(r0, ROW_CHUNK)
            o_ref[rows, :] = x_ref[rows, :] + FFN_RESIDUAL_WEIGHT * _rms(o_ref[rows, :], post_g_ref[...])
        _for_row_chunks(tm, ROW_CHUNK, finish_rows)


def _ffn(h, layer, pre_g, w_gate, w_up, w_down, post_g, *, tm, tf):
    m, d = h.shape
    d_ff = w_gate.shape[2]
    n_f = d_ff // tf
    kern = functools.partial(_ffn_kernel, n_f=n_f, n_col_chunks=4)
    return pl.pallas_call(
        kern,
        grid=(m // tm, n_f),
        in_specs=[
            pl.BlockSpec((tm, d), lambda i, f: (i, 0)),
            _layer_vec_spec(layer, d, 2),
            pl.BlockSpec((None, d, tf), lambda i, f: (layer, 0, f)),
            pl.BlockSpec((None, d, tf), lambda i, f: (layer, 0, f)),
            pl.BlockSpec((None, tf, d), lambda i, f: (layer, f, 0)),
            _layer_vec_spec(layer, d, 2),
        ],
        out_specs=pl.BlockSpec((tm, d), lambda i, f: (i, 0), pipeline_mode=pl.Buffered(1)),
        out_shape=jax.ShapeDtypeStruct((m, d), F32),
        scratch_shapes=[pltpu.VMEM((tm, d), BF16), pltpu.VMEM((tm, tf), BF16)],
        compiler_params=_params(2),
        name="ffn",
    )(h, _as_rows(pre_g), w_gate, w_up, w_down, _as_rows(post_g))


def _inproj_kernel(x_ref, g_ref, w0_ref, w1_ref, w2_ref, proj_ref, u_ref, xn_ref, *, n_head_steps):
    j = pl.program_id(1)
    tm = x_ref.shape[0]
    sub = W_IN_SUB

    @pl.when(j == 0)
    def _():
        def norm_rows(r0):
            rows = pl.ds(r0, ROW_CHUNK)
            xn_ref[rows, :] = _rms(x_ref[rows, :], g_ref[...]).astype(BF16)
        _for_row_chunks(tm, ROW_CHUNK, norm_rows)

    @pl.when(j < n_head_steps)
    def _():
        xn = xn_ref[...]
        for c, w_ref in enumerate((w0_ref, w1_ref, w2_ref)):
            y = jnp.dot(xn, w_ref[...].astype(BF16), preferred_element_type=F32)
            proj_ref[:, c * sub:(c + 1) * sub] = y.astype(proj_ref.dtype)

    @pl.when(j == n_head_steps)
    def _():
        xn = xn_ref[...]
        for c, w_ref in enumerate((w0_ref, w1_ref)):
            u_ref[:, c * sub:(c + 1) * sub] = jnp.dot(xn, w_ref[...].astype(BF16), preferred_element_type=F32)


def _inproj(h, layer, g, w_in, *, tm):
    m, d = h.shape
    n_head_steps = PROJ_COLS // PROJ_STEP_COLS
    ssm_blk0 = SSM_COL0 // W_IN_SUB
    n_ssm_blks = SSM_WIDTH // W_IN_SUB
    last_blk = w_in.shape[2] // W_IN_SUB - 1

    def w_spec(c):
        def index(i, j):
            head_blk = 3 * j + c + jnp.where(3 * j + c >= ssm_blk0, n_ssm_blks, 0)
            blk = jnp.where(j < n_head_steps, head_blk, ssm_blk0 + c)
            return (layer, 0, jnp.minimum(blk, last_blk))
        return pl.BlockSpec((None, d, W_IN_SUB), index)

    kern = functools.partial(_inproj_kernel, n_head_steps=n_head_steps)
    return pl.pallas_call(
        kern,
        grid=(m // tm, n_head_steps + 1),
        in_specs=[
            pl.BlockSpec((tm, d), lambda i, j: (i, 0)),
            _layer_vec_spec(layer, d, 2),
            w_spec(0), w_spec(1), w_spec(2),
        ],
        out_specs=[
            pl.BlockSpec((tm, PROJ_STEP_COLS), lambda i, j: (i, jnp.minimum(j, n_head_steps - 1))),
            pl.BlockSpec((tm, SSM_WIDTH), lambda i, j: (i, 0)),
        ],
        out_shape=[jax.ShapeDtypeStruct((m, PROJ_COLS), BF16), jax.ShapeDtypeStruct((m, SSM_WIDTH), F32)],
        scratch_shapes=[pltpu.VMEM((tm, d), BF16)],
        compiler_params=_params(2),
        name="inproj",
    )(h, _as_rows(g), w_in, w_in, w_in)


def _split2_bf16(x):
    hi = x.astype(BF16)
    lo = (x - hi.astype(F32)).astype(BF16)
    return hi, lo


def _mask_first_tile(x, keep):
    t = ATT_TILE
    head = jnp.where(keep, x[0:t], 0.0)
    return head if x.shape[0] == t else jnp.concatenate([head, x[t:]], axis=0)


def _sb_kernel(q_ref, k_ref, v_ref, o_ref):
    t = ATT_TILE
    seq = q_ref.shape[0]
    n_tiles = seq // t
    scale2 = LOG2E / math.sqrt(HEAD_DIM)
    row = lax.broadcasted_iota(jnp.int32, (t, t), 0)
    col = lax.broadcasted_iota(jnp.int32, (t, t), 1)
    strictly_lower = row > col
    ones = jnp.where(strictly_lower, 1.0, 0.0).astype(BF16)
    suffix_ones = jnp.concatenate([ones, ones], axis=0)
    acc = [None] * n_tiles
    later = [None] * n_tiles

    for j in range(n_tiles - 1, -1, -1):
        r0 = j * t
        k = k_ref[r0:r0 + t, :]
        v = v_ref[r0:r0 + t, :]
        z = lax.dot_general(q_ref[r0:seq, :], k, (((1,), (1,)), ((), ())), preferred_element_type=F32) * scale2
        pos = jnp.maximum(z, 0.0)
        neg = jnp.minimum(z, 0.0)
        l2 = jnp.log2(1.0 + jnp.exp2(neg - pos))
        sp = pos + l2
        log_beta = neg - l2
        sp_in = _mask_first_tile(sp, strictly_lower)
        hi, lo = _split2_bf16(sp_in)
        suffix = jnp.dot(jnp.concatenate([hi, lo], axis=1), suffix_ones, preferred_element_type=F32)
        later_rows = jnp.concatenate([jnp.zeros((t, 1), F32)] + later[j + 1:], axis=0)
        w = jnp.exp2(log_beta - suffix - later_rows)
        w = _mask_first_tile(w, strictly_lower)
        pv = jnp.dot(w.astype(BF16), v, preferred_element_type=F32)
        row_sum = jnp.sum(sp_in, axis=1, keepdims=True)
        for i in range(j, n_tiles):
            rows = slice((i - j) * t, (i - j + 1) * t)
            acc[i] = pv[rows] if i == j else acc[i] + pv[rows]
            later[i] = row_sum[rows] if i == j else later[i] + row_sum[rows]

    for i in range(n_tiles):
        o_ref[i * t:(i + 1) * t, :] = acc[i].astype(o_ref.dtype)


def _sb_attention(proj, *, batch, seq):
    spec = lambda blk: pl.BlockSpec((None, seq, HEAD_DIM), lambda b, h: (b, 0, blk + h))
    return pl.pallas_call(
        _sb_kernel,
        grid=(batch, SB_HEADS),
        in_specs=[spec(SB_Q_BLK), spec(SB_K_BLK), spec(SB_V_BLK)],
        out_specs=spec(0),
        out_shape=jax.ShapeDtypeStruct((batch, seq, SB_HEADS * HEAD_DIM), BF16),
        compiler_params=_params(2),
        name="sb_attention",
    )(proj, proj, proj)


def _t5_bias_kernel(rel_ref, o_ref):
    t = ATT_TILE
    h = pl.program_id(0)
    row = lax.broadcasted_iota(jnp.int32, (t, t), 0)
    col = lax.broadcasted_iota(jnp.int32, (t, t), 1)
    max_exact = NUM_BUCKETS // 2
    for d in range(3):
        n = jnp.maximum(d * t + row - col, 0)
        nf = jnp.maximum(n, 1).astype(F32)
        large = max_exact + (jnp.log(nf / max_exact) / math.log(MAX_DISTANCE / max_exact)
                             * (NUM_BUCKETS - max_exact)).astype(jnp.int32)
        large = jnp.minimum(large, NUM_BUCKETS - 1)
        bucket = jnp.where(n < max_exact, n, large)
        bias = jnp.zeros((t, t), F32)
        for b in range(NUM_BUCKETS):
            bias = jnp.where(bucket == b, rel_ref[b, h], bias)
        o_ref[d] = bias * LOG2E


def _t5_bias_tiles(rel_bias):
    t = ATT_TILE
    return pl.pallas_call(
        _t5_bias_kernel,
        grid=(DIFF_HEADS,),
        in_specs=[pl.BlockSpec(memory_space=pltpu.SMEM)],
        out_specs=pl.BlockSpec((None, 3, t, t), lambda h: (h, 0, 0, 0)),
        out_shape=jax.ShapeDtypeStruct((DIFF_HEADS, 3, t, t), F32),
        compiler_params=_params(1),
        name="t5_bias_tiles",
    )(rel_bias)


def _diff_kernel(lq1_ref, lk1_ref, lq2_ref, lk2_ref, subln_g_ref, bias_ref, q_ref, k_ref, v_ref, o_ref,
                 *, lambda_init):
    t = ATT_TILE
    seq = q_ref.shape[0]
    n_tiles = seq // t
    scale2 = LOG2E / math.sqrt(DIFF_QK_DIM)
    lane = lax.broadcasted_iota(jnp.int32, (t, HEAD_DIM), 1)
    row = lax.broadcasted_iota(jnp.int32, (t, t), 0)
    col = lax.broadcasted_iota(jnp.int32, (t, t), 1)
    causal = col <= row
    lam = (jnp.exp(jnp.sum(lq1_ref[...] * lk1_ref[...], axis=1, keepdims=True))
           - jnp.exp(jnp.sum(lq2_ref[...] * lk2_ref[...], axis=1, keepdims=True)) + lambda_init)

    for i in range(n_tiles):
        n_keys = (i + 1) * t
        q = q_ref[i * t:(i + 1) * t, :]
        zero = jnp.zeros_like(q)
        q_maps = (jnp.where(lane < DIFF_QK_DIM, q, zero), jnp.where(lane >= DIFF_QK_DIM, q, zero))
        k = k_ref[0:n_keys, :]
        exps, sums = [], []
        for qm in q_maps:
            raw = lax.dot_general(qm, k, (((1,), (1,)), ((), ())), preferred_element_type=F32)
            parts = []
            for j in range(i + 1):
                s_j = raw[:, j * t:(j + 1) * t] * scale2 + bias_ref[min(i - j, 2)]
                parts.append(jnp.where(causal, s_j, NEG_INF) if j == i else s_j)
            s = parts[0] if i == 0 else jnp.concatenate(parts, axis=1)
            e = jnp.exp2(s - jnp.max(s, axis=1, keepdims=True))
            exps.append(e)
            sums.append(jnp.sum(e, axis=1, keepdims=True))
        p = exps[0] - exps[1] * (lam * sums[0] / sums[1])
        o = jnp.dot(p.astype(BF16), v_ref[0:n_keys, :], preferred_element_type=F32) * (1.0 / sums[0])
        o_ref[i * t:(i + 1) * t, :] = (_rms(o, subln_g_ref[...]) * (1.0 - lambda_init)).astype(o_ref.dtype)


def _diff_attention(proj, bias_tiles, layer, lq1, lk1, lq2, lk2, subln_g, *, batch, seq, lambda_init):
    t = ATT_TILE
    vec = lambda n: _layer_vec_spec(layer, n, 2)
    spec = lambda blk: pl.BlockSpec((None, seq, HEAD_DIM), lambda b, h: (b, 0, blk + h))
    kern = functools.partial(_diff_kernel, lambda_init=lambda_init)
    return pl.pallas_call(
        kern,
        grid=(batch, DIFF_HEADS),
        in_specs=[
            vec(DIFF_QK_DIM), vec(DIFF_QK_DIM), vec(DIFF_QK_DIM), vec(DIFF_QK_DIM), vec(HEAD_DIM),
            pl.BlockSpec((None, 3, t, t), lambda b, h: (h, 0, 0, 0)),
            spec(DF_Q_BLK), spec(DF_K_BLK), spec(DF_V_BLK),
        ],
        out_specs=spec(0),
        out_shape=jax.ShapeDtypeStruct((batch, seq, DIFF_HEADS * HEAD_DIM), BF16),
        compiler_params=_params(2),
        name="diff_attention",
    )(_as_rows(lq1), _as_rows(lk1), _as_rows(lq2), _as_rows(lk2), _as_rows(subln_g), bias_tiles, proj, proj, proj)


def _ssm_prep_kernel(a_re_ref, a_im_ref, log_dt_ref, b_re_ref, b_im_ref, c_re_ref, c_im_ref,
                     lam_ref, bbar_ref, cmat_ref):
    ar, ai = a_re_ref[...], a_im_ref[...]
    dt = jnp.exp(log_dt_ref[...])
    mag = jnp.exp(ar * dt)
    lb_re, lb_im = mag * jnp.cos(ai * dt), mag * jnp.sin(ai * dt)
    den = ar * ar + ai * ai
    f_re = ((lb_re - 1.0) * ar + lb_im * ai) / den
    f_im = (lb_im * ar - (lb_re - 1.0) * ai) / den
    lam_ref[0:1, :] = lb_re
    lam_ref[1:2, :] = lb_im
    br, bi = b_re_ref[...], b_im_ref[...]
    nb = br.shape[1]
    bbar_ref[:, 0:nb] = (f_re * br - f_im * bi).astype(BF16)
    bbar_ref[:, nb:2 * nb] = (f_re * bi + f_im * br).astype(BF16)
    cmat_ref[0:nb, :] = c_re_ref[...].astype(BF16)
    cmat_ref[nb:2 * nb, :] = (-c_im_ref[...]).astype(BF16)


def _block_diag_groups(x):
    gpb = SSM_GROUPS // SSM_STATE_BLOCKS
    nl, g, r, k = x.shape
    xb = x.reshape(nl, SSM_STATE_BLOCKS, gpb, r, k)
    eye = jnp.eye(gpb, dtype=x.dtype)
    return jnp.einsum("lagrk,gh->lagrhk", xb, eye).reshape(nl, SSM_STATE_BLOCKS, gpb * r, gpb * k)


def _ssm_prep(a_re, a_im, log_dt, b_re, b_im, c_re, c_im):
    n_layers = a_re.shape[0]
    n_state = SSM_GROUPS * SSM_STATE
    nb = n_state // SSM_STATE_BLOCKS
    cb = SSM_WIDTH // SSM_STATE_BLOCKS
    row = lambda x: x.reshape(n_layers, 1, n_state)
    log_dt_row = jnp.broadcast_to(log_dt[:, :, None], (n_layers, SSM_GROUPS, SSM_STATE))
    b_args = [_block_diag_groups(jnp.swapaxes(b, 2, 3)) for b in (b_re, b_im)]
    c_args = [_block_diag_groups(jnp.swapaxes(c, 2, 3)) for c in (c_re, c_im)]
    vec = pl.BlockSpec((None, 1, nb), lambda l, s: (l, 0, s))
    return pl.pallas_call(
        _ssm_prep_kernel,
        grid=(n_layers, SSM_STATE_BLOCKS),
        in_specs=[vec, vec, vec,
                  pl.BlockSpec((None, None, cb, nb), lambda l, s: (l, s, 0, 0)),
                  pl.BlockSpec((None, None, cb, nb), lambda l, s: (l, s, 0, 0)),
                  pl.BlockSpec((None, None, nb, cb), lambda l, s: (l, s, 0, 0)),
                  pl.BlockSpec((None, None, nb, cb), lambda l, s: (l, s, 0, 0))],
        out_specs=[pl.BlockSpec((None, 2, nb), lambda l, s: (l, 0, s)),
                   pl.BlockSpec((None, None, cb, 2 * nb), lambda l, s: (l, s, 0, 0)),
                   pl.BlockSpec((None, None, 2 * nb, cb), lambda l, s: (l, s, 0, 0))],
        out_shape=[jax.ShapeDtypeStruct((n_layers, 2, n_state), F32),
                   jax.ShapeDtypeStruct((n_layers, SSM_STATE_BLOCKS, cb, 2 * nb), BF16),
                   jax.ShapeDtypeStruct((n_layers, SSM_STATE_BLOCKS, 2 * nb, cb), BF16)],
        compiler_params=_params(2),
        name="ssm_prep",
    )(row(a_re), row(a_im), row(log_dt_row), *b_args, *c_args)


def _gelu_tanh(x):
    return 0.5 * x * (1.0 + jnp.tanh(math.sqrt(2.0 / math.pi) * (x + 0.044715 * (x * x * x))))


def _ssm_kernel(u_ref, lam_ref, bbar_ref, cmat_ref, d_ref, wglu_ref, bglu_ref, o_ref,
                state_ref, bu_ref, y_ref, wglu_bf_ref, *, t_chunk):
    c = pl.program_id(0)
    p = SSM_BATCH_PAD
    nb = lam_ref.shape[1] // SSM_STATE_BLOCKS
    cb = SSM_WIDTH // SSM_STATE_BLOCKS

    @pl.when(c == 0)
    def _():
        state_ref[...] = jnp.zeros_like(state_ref)
        wglu_bf_ref[...] = wglu_ref[...].astype(BF16)

    for s in range(SSM_STATE_BLOCKS):
        u_s = u_ref[:, s * cb:(s + 1) * cb].astype(BF16)
        bu_ref[...] = jnp.dot(u_s, bbar_ref[s], preferred_element_type=F32)
        ar = jnp.broadcast_to(lam_ref[0:1, s * nb:(s + 1) * nb], (p, nb))
        ai = jnp.broadcast_to(lam_ref[1:2, s * nb:(s + 1) * nb], (p, nb))
        re_cols = slice(s * 2 * nb, s * 2 * nb + nb)
        im_cols = slice(s * 2 * nb + nb, (s + 1) * 2 * nb)

        def step(tt, carry):
            xr, xi = carry
            rows = pl.ds(pl.multiple_of(tt * p, p), p)
            nxr = ar * xr - ai * xi + bu_ref[rows, 0:nb]
            nxi = ar * xi + ai * xr + bu_ref[rows, nb:2 * nb]
            bu_ref[rows, 0:nb] = nxr
            bu_ref[rows, nb:2 * nb] = nxi
            return nxr, nxi

        xr, xi = lax.fori_loop(0, t_chunk, step, (state_ref[:, re_cols], state_ref[:, im_cols]), unroll=8)
        state_ref[:, re_cols] = xr
        state_ref[:, im_cols] = xi
        y_ref[:, s * cb:(s + 1) * cb] = jnp.dot(bu_ref[...].astype(BF16), cmat_ref[s],
                                                preferred_element_type=F32)

    y = _gelu_tanh(y_ref[...] + d_ref[...] * u_ref[...])
    gate = jax.nn.sigmoid(jnp.dot(y.astype(BF16), wglu_bf_ref[...], preferred_element_type=F32) + bglu_ref[...])
    o_ref[...] = (y * gate).astype(o_ref.dtype)


def _ssm(u_pad, layer, lam, bbar, cmat, d_skip, w_glu, b_glu, *, seq, t_chunk):
    p = SSM_BATCH_PAD
    n_state = lam.shape[2]
    nb = n_state // SSM_STATE_BLOCKS
    rows = t_chunk * p
    layer_block = lambda shape: pl.BlockSpec((None,) + shape[1:], lambda c: (layer,) + (0,) * (len(shape) - 1))
    kern = functools.partial(_ssm_kernel, t_chunk=t_chunk)
    d_rows = d_skip.reshape(d_skip.shape[0], 1, SSM_WIDTH)
    return pl.pallas_call(
        kern,
        grid=(seq // t_chunk,),
        in_specs=[
            pl.BlockSpec((rows, SSM_WIDTH), lambda c: (c, 0)),
            layer_block(lam.shape), layer_block(bbar.shape), layer_block(cmat.shape),
            layer_block(d_rows.shape), layer_block(w_glu.shape), layer_block((b_glu.shape[0], 1, SSM_WIDTH)),
        ],
        out_specs=pl.BlockSpec((rows, SSM_WIDTH), lambda c: (c, 0)),
        out_shape=jax.ShapeDtypeStruct((seq * p, SSM_WIDTH), BF16),
        scratch_shapes=[
            pltpu.VMEM((p, 2 * n_state), F32),
            pltpu.VMEM((rows, 2 * nb), F32),
            pltpu.VMEM((rows, SSM_WIDTH), F32),
            pltpu.VMEM((SSM_WIDTH, SSM_WIDTH), BF16),
        ],
        compiler_params=_params(1),
        name="ssm_scan",
    )(u_pad, lam, bbar, cmat, d_rows, w_glu, _as_rows(b_glu))


def _cast_kernel(x_ref, o_ref):
    o_ref[...] = x_ref[...].astype(o_ref.dtype)


def _cast_bf16(w, *, rows):
    nl, r, c = w.shape
    return pl.pallas_call(
        _cast_kernel,
        grid=(nl, r // rows),
        in_specs=[pl.BlockSpec((None, rows, c), lambda l, i: (l, i, 0))],
        out_specs=pl.BlockSpec((None, rows, c), lambda l, i: (l, i, 0)),
        out_shape=jax.ShapeDtypeStruct(w.shape, BF16),
        compiler_params=_params(2),
        name="cast_bf16",
    )(w)


def _outproj_kernel(h_ref, sb_ref, ssm_ref, df_ref, w_ref, g_ref, o_ref, mixed_ref):
    w_sb, w_ssm = sb_ref.shape[1], ssm_ref.shape[1]
    mixed_ref[:, 0:w_sb] = sb_ref[...]
    mixed_ref[:, w_sb:w_sb + w_ssm] = ssm_ref[...]
    mixed_ref[:, w_sb + w_ssm:] = df_ref[...]
    y = jnp.dot(mixed_ref[...], w_ref[...], preferred_element_type=F32)
    o_ref[...] = h_ref[...] + _rms(y, g_ref[...])


def _outproj(h, o_sb, o_ssm, o_diff, layer, w_out_bf, g, *, tm):
    m, d = h.shape
    k = w_out_bf.shape[1]
    row_block = lambda x: pl.BlockSpec((tm, x.shape[1]), lambda i: (i, 0))
    return pl.pallas_call(
        _outproj_kernel,
        grid=(m // tm,),
        in_specs=[
            row_block(h), row_block(o_sb), row_block(o_ssm), row_block(o_diff),
            pl.BlockSpec((None, k, d), lambda i: (layer, 0, 0), pipeline_mode=pl.Buffered(1)),
            _layer_vec_spec(layer, d, 1),
        ],
        out_specs=row_block(h),
        out_shape=jax.ShapeDtypeStruct((m, d), F32),
        scratch_shapes=[pltpu.VMEM((tm, k), BF16)],
        compiler_params=_params(1),
        name="outproj",
    )(h, o_sb, o_ssm, o_diff, w_out_bf, _as_rows(g))


def _trunk(x, ffn1_pre_g, ffn1_w_gate, ffn1_w_up, ffn1_w_down, ffn1_post_g, mix_pre_g, w_in, ssm_a_re, ssm_a_im,
           ssm_log_dt, ssm_b_re, ssm_b_im, ssm_c_re, ssm_c_im, ssm_d, ssm_w_glu, ssm_b_glu, diff_lq1, diff_lk1,
           diff_lq2, diff_lk2, diff_subln_g, rel_bias, w_out, mix_post_g, ffn2_pre_g, ffn2_w_gate, ffn2_w_up,
           ffn2_w_down, ffn2_post_g, *, tm, tf, tm_out, t_chunk):
    batch, seq, d = x.shape
    m = batch * seq
    depth = w_in.shape[0]
    h = x.reshape(m, d)
    bias_tiles = _t5_bias_tiles(rel_bias)
    lam, bbar, cmat = _ssm_prep(ssm_a_re, ssm_a_im, ssm_log_dt, ssm_b_re, ssm_b_im, ssm_c_re, ssm_c_im)
    w_out_bf = _cast_bf16(w_out, rows=512)
    for l in range(depth):
        h = _ffn(h, l, ffn1_pre_g, ffn1_w_gate, ffn1_w_up, ffn1_w_down, ffn1_post_g, tm=tm, tf=tf)

        proj, ssm_u = _inproj(h, l, mix_pre_g, w_in, tm=tm)
        proj = proj.reshape(batch, seq, PROJ_COLS)

        o_sb = _sb_attention(proj, batch=batch, seq=seq)

        u_pad = jnp.pad(jnp.swapaxes(ssm_u.reshape(batch, seq, SSM_WIDTH), 0, 1),
                        ((0, 0), (0, SSM_BATCH_PAD - batch), (0, 0))).reshape(seq * SSM_BATCH_PAD, SSM_WIDTH)
        o_ssm = _ssm(u_pad, l, lam, bbar, cmat, ssm_d, ssm_w_glu, ssm_b_glu, seq=seq, t_chunk=t_chunk)
        o_ssm = jnp.swapaxes(o_ssm.reshape(seq, SSM_BATCH_PAD, SSM_WIDTH)[:, :batch], 0, 1).reshape(m, SSM_WIDTH)

        lambda_init = 0.8 - 0.6 * math.exp(-0.3 * l)
        o_diff = _diff_attention(proj, bias_tiles, l, diff_lq1, diff_lk1, diff_lq2, diff_lk2, diff_subln_g,
                                 batch=batch, seq=seq, lambda_init=lambda_init)

        h = _outproj(h, o_sb.reshape(m, -1), o_ssm, o_diff.reshape(m, -1), l, w_out_bf, mix_post_g, tm=tm_out)

        h = _ffn(h, l, ffn2_pre_g, ffn2_w_gate, ffn2_w_up, ffn2_w_down, ffn2_post_g, tm=tm, tf=tf)
    return h.reshape(batch, seq, d)


def kernel(x, ffn1_pre_g, ffn1_w_gate, ffn1_w_up, ffn1_w_down, ffn1_post_g, mix_pre_g, w_in, ssm_a_re, ssm_a_im, ssm_log_dt, ssm_b_re, ssm_b_im, ssm_c_re, ssm_c_im, ssm_d, ssm_w_glu, ssm_b_glu, diff_lq1, diff_lk1, diff_lq2, diff_lk2, diff_subln_g, rel_bias, w_out, mix_post_g, ffn2_pre_g, ffn2_w_gate, ffn2_w_up, ffn2_w_down, ffn2_post_g):
    assert x.shape[0] <= SSM_BATCH_PAD and x.shape[1] % ATT_TILE == 0
    return _trunk(x, ffn1_pre_g, ffn1_w_gate, ffn1_w_up, ffn1_w_down, ffn1_post_g, mix_pre_g, w_in, ssm_a_re,
                  ssm_a_im, ssm_log_dt, ssm_b_re, ssm_b_im, ssm_c_re, ssm_c_im, ssm_d, ssm_w_glu, ssm_b_glu,
                  diff_lq1, diff_lk1, diff_lq2, diff_lk2, diff_subln_g, rel_bias, w_out, mix_post_g, ffn2_pre_g,
                  ffn2_w_gate, ffn2_w_up, ffn2_w_down, ffn2_post_g, tm=1024, tf=512, tm_out=512, t_chunk=128)
```

```python
import functools
import math

import jax
import jax.numpy as jnp
from jax import lax
from jax.experimental import pallas as pl
from jax.experimental.pallas import tpu as pltpu

F32 = jnp.float32
BF16 = jnp.bfloat16

HEAD_DIM = 128
SB_HEADS = 6
DIFF_HEADS = 6
DIFF_QK_DIM = HEAD_DIM // 2
SSM_GROUPS = 32
SSM_CH_PER_GROUP = 16
SSM_STATE = 64
SSM_WIDTH = SSM_GROUPS * SSM_CH_PER_GROUP
NUM_BUCKETS = 32
MAX_DISTANCE = 128
NORM_EPS = 1e-6
FFN_RESIDUAL_WEIGHT = 0.5
LOG2E = 1.0 / math.log(2.0)

SB_Q_BLK, SB_K_BLK, SB_V_BLK = 0, 6, 12
DF_Q_BLK, DF_K_BLK, DF_V_BLK = 18, 24, 30
PROJ_COLS = 36 * HEAD_DIM
SSM_COL0 = 3 * SB_HEADS * HEAD_DIM
W_IN_SUB = 256
PROJ_STEP_COLS = 3 * W_IN_SUB

V7X_VMEM_LIMIT = 60 * 1024 * 1024
FFN_SUB = 256
ROW_CHUNK = 128
ATT_TILE = 256
SSM_CHUNK = 16
SSM_SCAN_ROWS = 8
NEG_INF = float(jnp.finfo(jnp.float32).min)


def _params(n_grid, vmem=V7X_VMEM_LIMIT):
    return pltpu.CompilerParams(dimension_semantics=("arbitrary",) * n_grid, vmem_limit_bytes=vmem)


def _rms(x, g):
    ms = jnp.mean(x * x, axis=-1, keepdims=True)
    return x * lax.rsqrt(ms + NORM_EPS) * g


def _for_row_chunks(n_rows, chunk, body):
    def step(i, carry):
        body(pl.multiple_of(i * chunk, chunk))
        return carry
    lax.fori_loop(0, n_rows // chunk, step, 0)


def _layer_vec_spec(layer, n):
    return pl.BlockSpec((None, 1, n), lambda *_: (layer, 0, 0))


def _as_rows(p):
    return p.reshape(p.shape[0], 1, p.shape[1])


def _ffn_kernel(x_ref, pre_g_ref, wg_ref, wu_ref, wd_ref, post_g_ref, o_ref, xn_ref, a_ref, *, n_f, n_col_chunks):
    f = pl.program_id(1)
    tm, d = x_ref.shape

    @pl.when(f == 0)
    def _():
        def norm_rows(r0):
            rows = pl.ds(r0, ROW_CHUNK)
            xn_ref[rows, :] = _rms(x_ref[rows, :], pre_g_ref[...]).astype(BF16)
            o_ref[rows, :] = jnp.zeros((ROW_CHUNK, d), F32)
        _for_row_chunks(tm, ROW_CHUNK, norm_rows)

    xn = xn_ref[...]
    tf = wg_ref.shape[1]
    for hf in range(tf // FFN_SUB):
        sub = slice(hf * FFN_SUB, (hf + 1) * FFN_SUB)
        g = jnp.dot(xn, wg_ref[:, sub].astype(BF16), preferred_element_type=F32)
        u = jnp.dot(xn, wu_ref[:, sub].astype(BF16), preferred_element_type=F32)
        a_ref[:, sub] = (g * jax.nn.sigmoid(g) * u).astype(BF16)
    a = a_ref[...]
    cw = d // n_col_chunks
    for c in range(n_col_chunks):
        cols = slice(c * cw, (c + 1) * cw)
        o_ref[:, cols] += jnp.dot(a, wd_ref[:, cols].astype(BF16), preferred_element_type=F32)

    @pl.when(f == n_f - 1)
    def _():
        def finish_rows(r0):
            rows = pl.ds(r0, ROW_CHUNK)
            o_ref[rows, :] = x_ref[rows, :] + FFN_RESIDUAL_WEIGHT * _rms(o_ref[rows, :], post_g_ref[...])
        _for_row_chunks(tm, ROW_CHUNK, finish_rows)


def _ffn(h, layer, pre_g, w_gate, w_up, w_down, post_g, *, tm, tf):
    m, d = h.shape
    d_ff = w_gate.shape[2]
    n_f = d_ff // tf
    kern = functools.partial(_ffn_kernel, n_f=n_f, n_col_chunks=4)
    return pl.pallas_call(
        kern,
        grid=(m // tm, n_f),
        in_specs=[
            pl.BlockSpec((tm, d), lambda i, f: (i, 0)),
            _layer_vec_spec(layer, d),
            pl.BlockSpec((None, d, tf), lambda i, f: (layer, 0, f)),
            pl.BlockSpec((None, d, tf), lambda i, f: (layer, 0, f)),
            pl.BlockSpec((None, tf, d), lambda i, f: (layer, f, 0)),
            _layer_vec_spec(layer, d),
        ],
        out_specs=pl.BlockSpec((tm, d), lambda i, f: (i, 0), pipeline_mode=pl.Buffered(1)),
        out_shape=jax.ShapeDtypeStruct((m, d), F32),
        scratch_shapes=[pltpu.VMEM((tm, d), BF16), pltpu.VMEM((tm, tf), BF16)],
        compiler_params=_params(2),
        name="ffn",
    )(h, _as_rows(pre_g), w_gate, w_up, w_down, _as_rows(post_g))


def _inproj_kernel(x_ref, g_ref, w0_ref, w1_ref, w2_ref, proj_ref, u_ref, xn_ref, *, n_head_steps):
    j = pl.program_id(1)
    tm = x_ref.shape[0]
    sub = W_IN_SUB

    @pl.when(j == 0)
    def _():
        def norm_rows(r0):
            rows = pl.ds(r0, ROW_CHUNK)
            xn_ref[rows, :] = _rms(x_ref[rows, :], g_ref[...]).astype(BF16)
        _for_row_chunks(tm, ROW_CHUNK, norm_rows)

    @pl.when(j < n_head_steps)
    def _():
        xn = xn_ref[...]
        for c, w_ref in enumerate((w0_ref, w1_ref, w2_ref)):
            y = jnp.dot(xn, w_ref[...].astype(BF16), preferred_element_type=F32)
            proj_ref[:, c * sub:(c + 1) * sub] = y.astype(proj_ref.dtype)

    @pl.when(j == n_head_steps)
    def _():
        xn = xn_ref[...]
        for c, w_ref in enumerate((w0_ref, w1_ref)):
            u_ref[:, c * sub:(c + 1) * sub] = jnp.dot(xn, w_ref[...].astype(BF16), preferred_element_type=F32)


def _inproj(h, layer, g, w_in, *, tm):
    m, d = h.shape
    n_head_steps = PROJ_COLS // PROJ_STEP_COLS
    ssm_blk0 = SSM_COL0 // W_IN_SUB
    n_ssm_blks = SSM_WIDTH // W_IN_SUB
    last_blk = w_in.shape[2] // W_IN_SUB - 1

    def w_spec(c):
        def index(i, j):
            head_blk = 3 * j + c + jnp.where(3 * j + c >= ssm_blk0, n_ssm_blks, 0)
            blk = jnp.where(j < n_head_steps, head_blk, ssm_blk0 + c)
            return (layer, 0, jnp.minimum(blk, last_blk))
        return pl.BlockSpec((None, d, W_IN_SUB), index)

    kern = functools.partial(_inproj_kernel, n_head_steps=n_head_steps)
    return pl.pallas_call(
        kern,
        grid=(m // tm, n_head_steps + 1),
        in_specs=[
            pl.BlockSpec((tm, d), lambda i, j: (i, 0)),
            _layer_vec_spec(layer, d),
            w_spec(0), w_spec(1), w_spec(2),
        ],
        out_specs=[
            pl.BlockSpec((tm, PROJ_STEP_COLS), lambda i, j: (i, jnp.minimum(j, n_head_steps - 1))),
            pl.BlockSpec((tm, SSM_WIDTH), lambda i, j: (i, 0)),
        ],
        out_shape=[jax.ShapeDtypeStruct((m, PROJ_COLS), BF16), jax.ShapeDtypeStruct((m, SSM_WIDTH), F32)],
        scratch_shapes=[pltpu.VMEM((tm, d), BF16)],
        compiler_params=_params(2),
        name="inproj",
    )(h, _as_rows(g), w_in, w_in, w_in)


def _split2_bf16(x):
    hi = x.astype(BF16)
    lo = (x - hi.astype(F32)).astype(BF16)
    return hi, lo


def _mask_first_tile(x, keep):
    t = ATT_TILE
    head = jnp.where(keep, x[0:t], 0.0)
    return head if x.shape[0] == t else jnp.concatenate([head, x[t:]], axis=0)


def _sb_kernel(q_ref, k_ref, v_ref, o_ref):
    t = ATT_TILE
    seq = q_ref.shape[0]
    n_tiles = seq // t
    scale2 = LOG2E / math.sqrt(HEAD_DIM)
    row = lax.broadcasted_iota(jnp.int32, (t, t), 0)
    col = lax.broadcasted_iota(jnp.int32, (t, t), 1)
    strictly_lower = row > col
    ones = jnp.where(strictly_lower, 1.0, 0.0).astype(BF16)
    suffix_ones = jnp.concatenate([ones, ones], axis=0)
    acc = [None] * n_tiles
    later = [None] * n_tiles

    for j in range(n_tiles - 1, -1, -1):
        r0 = j * t
        k = k_ref[r0:r0 + t, :]
        v = v_ref[r0:r0 + t, :]
        z = lax.dot_general(q_ref[r0:seq, :], k, (((1,), (1,)), ((), ())), preferred_element_type=F32) * scale2
        pos = jnp.maximum(z, 0.0)
        neg = jnp.minimum(z, 0.0)
        l2 = jnp.log2(1.0 + jnp.exp2(neg - pos))
        sp = pos + l2
        log_beta = neg - l2
        sp_in = _mask_first_tile(sp, strictly_lower)
        hi, lo = _split2_bf16(sp_in)
        suffix = jnp.dot(jnp.concatenate([hi, lo], axis=1), suffix_ones, preferred_element_type=F32)
        later_rows = jnp.concatenate([jnp.zeros((t, 1), F32)] + later[j + 1:], axis=0)
        w = jnp.exp2(log_beta - suffix - later_rows)
        w = _mask_first_tile(w, strictly_lower)
        pv = jnp.dot(w.astype(BF16), v, preferred_element_type=F32)
        row_sum = jnp.sum(sp_in, axis=1, keepdims=True)
        for i in range(j, n_tiles):
            rows = slice((i - j) * t, (i - j + 1) * t)
            acc[i] = pv[rows] if i == j else acc[i] + pv[rows]
            later[i] = row_sum[rows] if i == j else later[i] + row_sum[rows]

    for i in range(n_tiles):
        o_ref[i * t:(i + 1) * t, :] = acc[i].astype(o_ref.dtype)


def _sb_attention(proj, *, batch, seq):
    spec = lambda blk: pl.BlockSpec((None, seq, HEAD_DIM), lambda b, h: (b, 0, blk + h))
    return pl.pallas_call(
        _sb_kernel,
        grid=(batch, SB_HEADS),
        in_specs=[spec(SB_Q_BLK), spec(SB_K_BLK), spec(SB_V_BLK)],
        out_specs=spec(0),
        out_shape=jax.ShapeDtypeStruct((batch, seq, SB_HEADS * HEAD_DIM), BF16),
        compiler_params=_params(2),
        name="sb_attention",
    )(proj, proj, proj)


def _t5_bias_kernel(rel_ref, o_ref):
    t = ATT_TILE
    h = pl.program_id(0)
    row = lax.broadcasted_iota(jnp.int32, (t, t), 0)
    col = lax.broadcasted_iota(jnp.int32, (t, t), 1)
    max_exact = NUM_BUCKETS // 2
    for d in range(3):
        n = jnp.maximum(d * t + row - col, 0)
        nf = jnp.maximum(n, 1).astype(F32)
        large = max_exact + (jnp.log(nf / max_exact) / math.log(MAX_DISTANCE / max_exact)
                             * (NUM_BUCKETS - max_exact)).astype(jnp.int32)
        large = jnp.minimum(large, NUM_BUCKETS - 1)
        bucket = jnp.where(n < max_exact, n, large)
        bias = jnp.zeros((t, t), F32)
        for b in range(NUM_BUCKETS):
            bias = jnp.where(bucket == b, rel_ref[b, h], bias)
        o_ref[d] = bias * LOG2E


def _t5_bias_tiles(rel_bias):
    t = ATT_TILE
    return pl.pallas_call(
        _t5_bias_kernel,
        grid=(DIFF_HEADS,),
        in_specs=[pl.BlockSpec(memory_space=pltpu.SMEM)],
        out_specs=pl.BlockSpec((None, 3, t, t), lambda h: (h, 0, 0, 0)),
        out_shape=jax.ShapeDtypeStruct((DIFF_HEADS, 3, t, t), F32),
        compiler_params=_params(1),
        name="t5_bias_tiles",
    )(rel_bias)


def _diff_kernel(lq1_ref, lk1_ref, lq2_ref, lk2_ref, subln_g_ref, bias_ref, q_ref, k_ref, v_ref, o_ref,
                 *, lambda_init):
    t = ATT_TILE
    seq = q_ref.shape[0]
    n_tiles = seq // t
    scale2 = LOG2E / math.sqrt(DIFF_QK_DIM)
    lane = lax.broadcasted_iota(jnp.int32, (t, HEAD_DIM), 1)
    row = lax.broadcasted_iota(jnp.int32, (t, t), 0)
    col = lax.broadcasted_iota(jnp.int32, (t, t), 1)
    causal = col <= row
    lam = (jnp.exp(jnp.sum(lq1_ref[...] * lk1_ref[...], axis=1, keepdims=True))
           - jnp.exp(jnp.sum(lq2_ref[...] * lk2_ref[...], axis=1, keepdims=True)) + lambda_init)

    for i in range(n_tiles):
        n_keys = (i + 1) * t
        q = q_ref[i * t:(i + 1) * t, :]
        zero = jnp.zeros_like(q)
        q_maps = (jnp.where(lane < DIFF_QK_DIM, q, zero), jnp.where(lane >= DIFF_QK_DIM, q, zero))
        k = k_ref[0:n_keys, :]
        exps, sums = [], []
        for qm in q_maps:
            raw = lax.dot_general(qm, k, (((1,), (1,)), ((), ())), preferred_element_type=F32)
            parts = []
            for j in range(i + 1):
                s_j = raw[:, j * t:(j + 1) * t] * scale2 + bias_ref[min(i - j, 2)]
                parts.append(jnp.where(causal, s_j, NEG_INF) if j == i else s_j)
            s = parts[0] if i == 0 else jnp.concatenate(parts, axis=1)
            e = jnp.exp2(s - jnp.max(s, axis=1, keepdims=True))
            exps.append(e)
            sums.append(jnp.sum(e, axis=1, keepdims=True))
        p = exps[0] - exps[1] * (lam * sums[0] / sums[1])
        o = jnp.dot(p.astype(BF16), v_ref[0:n_keys, :], preferred_element_type=F32) * (1.0 / sums[0])
        o_ref[i * t:(i + 1) * t, :] = (_rms(o, subln_g_ref[...]) * (1.0 - lambda_init)).astype(o_ref.dtype)


def _diff_attention(proj, bias_tiles, layer, lq1, lk1, lq2, lk2, subln_g, *, batch, seq, lambda_init):
    t = ATT_TILE
    vec = lambda n: _layer_vec_spec(layer, n)
    spec = lambda blk: pl.BlockSpec((None, seq, HEAD_DIM), lambda b, h: (b, 0, blk + h))
    kern = functools.partial(_diff_kernel, lambda_init=lambda_init)
    return pl.pallas_call(
        kern,
        grid=(batch, DIFF_HEADS),
        in_specs=[
            vec(DIFF_QK_DIM), vec(DIFF_QK_DIM), vec(DIFF_QK_DIM), vec(DIFF_QK_DIM), vec(HEAD_DIM),
            pl.BlockSpec((None, 3, t, t), lambda b, h: (h, 0, 0, 0)),
            spec(DF_Q_BLK), spec(DF_K_BLK), spec(DF_V_BLK),
        ],
        out_specs=spec(0),
        out_shape=jax.ShapeDtypeStruct((batch, seq, DIFF_HEADS * HEAD_DIM), BF16),
        compiler_params=_params(2),
        name="diff_attention",
    )(_as_rows(lq1), _as_rows(lk1), _as_rows(lq2), _as_rows(lk2), _as_rows(subln_g), bias_tiles, proj, proj, proj)


def _cpow(n, ar, ai, dt):
    mag = jnp.exp(n * (ar * dt))
    ang = n * (ai * dt)
    return mag * jnp.cos(ang), mag * jnp.sin(ang)


def _repeat_rows(x, reps):
    n, w = x.shape
    return jnp.broadcast_to(x[:, None, :], (n, reps, w)).reshape(n * reps, w)


def _ssm_prep_kernel(ar_ref, ai_ref, ldt_ref, bt_re_ref, bt_im_ref, ct_re_ref, ct_im_ref,
                     m_intra_ref, m_in_ref, m_out_ref, p1_ref, p2_ref):
    t = SSM_CHUNK
    cpg = SSM_CH_PER_GROUP
    ar, ai = ar_ref[...], ai_ref[...]
    dt = jnp.exp(ldt_ref[...])
    lb_re, lb_im = _cpow(1.0, ar, ai, dt)
    den = ar * ar + ai * ai
    f_re = ((lb_re - 1.0) * ar + lb_im * ai) / den
    f_im = (lb_im * ar - (lb_re - 1.0) * ai) / den
    bt_re, bt_im = bt_re_ref[...], bt_im_ref[...]
    bb_re = f_re * bt_re - f_im * bt_im
    bb_im = f_re * bt_im + f_im * bt_re
    ct_re, ct_im = ct_re_ref[...], ct_im_ref[...]
    pos = lax.broadcasted_iota(jnp.int32, (t, 2 * SSM_STATE), 0).astype(F32)
    lane_is_re = lax.broadcasted_iota(jnp.int32, (t * cpg, 2 * SSM_STATE), 1) < SSM_STATE

    def times_power(n, x_re, x_im, im_sign):
        e_re, e_im = _cpow(n, ar, ai, dt)
        e_re, e_im = _repeat_rows(e_re, cpg), _repeat_rows(e_im, cpg)
        return jnp.where(lane_is_re, e_re * x_re - e_im * x_im, im_sign * (e_re * x_im + e_im * x_re))

    m_in_ref[...] = times_power((t - 1.0) - pos, bb_re, bb_im, 1.0).astype(BF16)
    m_out_ref[...] = times_power(pos + 1.0, ct_re, ct_im, -1.0).astype(BF16)
    p_packed = times_power(-pos, bb_re, bb_im, -1.0)
    q_packed = times_power(pos, ct_re, ct_im, 1.0)
    k_full = lax.dot_general(p_packed, q_packed, (((1,), (1,)), ((), ())), preferred_element_type=F32,
                             precision=lax.Precision.HIGHEST)
    shift = cpg.bit_length() - 1
    row_k = lax.broadcasted_iota(jnp.int32, k_full.shape, 0)
    lane_k = lax.broadcasted_iota(jnp.int32, k_full.shape, 1)
    causal = jnp.right_shift(lane_k, shift) >= jnp.right_shift(row_k, shift)
    m_intra_ref[...] = jnp.where(causal, k_full, 0.0).astype(BF16)

    row_j = lax.broadcasted_iota(jnp.int32, (SSM_SCAN_ROWS, 2 * SSM_STATE), 0)
    lane_j = lax.broadcasted_iota(jnp.int32, (SSM_SCAN_ROWS, 2 * SSM_STATE), 1)
    pr, pi = _cpow(jnp.left_shift(t, row_j).astype(F32), ar, ai, dt)
    p1_ref[...] = pr
    p2_ref[...] = jnp.where(lane_j < SSM_STATE, -pi, pi)


def _ssm_prep(a_re, a_im, log_dt, b_re, b_im, c_re, c_im):
    nl, g, p = a_re.shape
    t, cpg = SSM_CHUNK, SSM_CH_PER_GROUP
    tc = t * cpg
    twice = lambda x: jnp.concatenate([x, x], axis=-1)
    rows = [twice(x)[:, :, None, :] for x in (a_re, a_im, jnp.broadcast_to(log_dt[:, :, None], a_re.shape))]
    bt = [jnp.tile(twice(jnp.swapaxes(b, 2, 3)), (1, 1, t, 1)) for b in (b_re, b_im)]
    ct = [jnp.tile(twice(c), (1, 1, t, 1)) for c in (c_re, c_im)]
    blk = lambda shape: pl.BlockSpec((None, None) + shape, lambda l, gi: (l, gi, 0, 0))
    return pl.pallas_call(
        _ssm_prep_kernel,
        grid=(nl, g),
        in_specs=[blk((1, 2 * p))] * 3 + [blk((tc, 2 * p))] * 4,
        out_specs=[blk((tc, tc)), blk((tc, 2 * p)), blk((tc, 2 * p)),
                   blk((SSM_SCAN_ROWS, 2 * p)), blk((SSM_SCAN_ROWS, 2 * p))],
        out_shape=[jax.ShapeDtypeStruct((nl, g, tc, tc), BF16),
                   jax.ShapeDtypeStruct((nl, g, tc, 2 * p), BF16),
                   jax.ShapeDtypeStruct((nl, g, tc, 2 * p), BF16),
                   jax.ShapeDtypeStruct((nl, g, SSM_SCAN_ROWS, 2 * p), F32),
                   jax.ShapeDtypeStruct((nl, g, SSM_SCAN_ROWS, 2 * p), F32)],
        compiler_params=_params(2),
        name="ssm_prep",
    )(*rows, *bt, *ct)


def _gelu_tanh(x):
    return 0.5 * x * (1.0 + jnp.tanh(math.sqrt(2.0 / math.pi) * (x + 0.044715 * (x * x * x))))


def _ssm_kernel(u_ref, m_intra_ref, m_in_ref, m_out_ref, p1_ref, p2_ref, d_ref, o_ref, *, n_batch):
    u = u_ref[...]
    ub = u.astype(BF16)
    n_rows = u.shape[0]
    y = jnp.dot(ub, m_intra_ref[...], preferred_element_type=F32)
    s_in = jnp.dot(ub, m_in_ref[...], preferred_element_type=F32)
    row = lax.broadcasted_iota(jnp.int32, s_in.shape, 0)

    def rows_back(x, n):
        return jnp.where(row >= n, pltpu.roll(x, n, axis=0), 0.0)

    x = rows_back(s_in, n_batch)
    n_chunks = n_rows // n_batch
    for j in range((n_chunks - 1).bit_length()):
        prev = rows_back(x, n_batch << j)
        x = x + p1_ref[j:j + 1, :] * prev + p2_ref[j:j + 1, :] * pltpu.roll(prev, SSM_STATE, axis=1)
    y_state = lax.dot_general(x.astype(BF16), m_out_ref[...], (((1,), (1,)), ((), ())), preferred_element_type=F32)
    y = y + y_state + d_ref[...] * u
    o_ref[...] = _gelu_tanh(y)


def _ssm(u_groups, layer, mats, d_skip, *, n_batch):
    g, n_rows, tc = u_groups.shape
    m_intra, m_in, m_out, p1, p2 = mats
    lg = lambda x: pl.BlockSpec((None, None) + x.shape[2:], lambda gi: (layer, gi, 0, 0))
    d_tiled = jnp.tile(d_skip, (1, 1, SSM_CHUNK))[:, :, None, :]
    kern = functools.partial(_ssm_kernel, n_batch=n_batch)
    return pl.pallas_call(
        kern,
        grid=(g,),
        in_specs=[pl.BlockSpec((None, n_rows, tc), lambda gi: (gi, 0, 0)),
                  lg(m_intra), lg(m_in), lg(m_out), lg(p1), lg(p2), lg(d_tiled)],
        out_specs=pl.BlockSpec((None, n_rows, tc), lambda gi: (gi, 0, 0)),
        out_shape=jax.ShapeDtypeStruct(u_groups.shape, F32),
        compiler_params=_params(1),
        name="ssm_chunks",
    )(u_groups, m_intra, m_in, m_out, p1, p2, d_tiled)


def _to_group_chunks(u, batch, seq):
    t, cpg = SSM_CHUNK, SSM_CH_PER_GROUP
    x = u.reshape(batch, seq // t, t, SSM_GROUPS, cpg)
    return jnp.transpose(x, (3, 1, 0, 2, 4)).reshape(SSM_GROUPS, (seq // t) * batch, t * cpg)


def _from_group_chunks(y, batch, seq):
    t, cpg = SSM_CHUNK, SSM_CH_PER_GROUP
    x = y.reshape(SSM_GROUPS, seq // t, batch, t, cpg)
    return jnp.transpose(x, (2, 1, 3, 0, 4)).reshape(batch * seq, SSM_WIDTH)


def _cast_kernel(x_ref, o_ref):
    o_ref[...] = x_ref[...].astype(o_ref.dtype)


def _cast_bf16(w, *, rows):
    nl, r, c = w.shape
    return pl.pallas_call(
        _cast_kernel,
        grid=(nl, r // rows),
        in_specs=[pl.BlockSpec((None, rows, c), lambda l, i: (l, i, 0))],
        out_specs=pl.BlockSpec((None, rows, c), lambda l, i: (l, i, 0)),
        out_shape=jax.ShapeDtypeStruct(w.shape, BF16),
        compiler_params=_params(2),
        name="cast_bf16",
    )(w)


def _outproj_kernel(h_ref, sb_ref, ssm_ref, df_ref, wglu_ref, bglu_ref, w_ref, g_ref, o_ref, mixed_ref):
    w_sb, w_ssm = sb_ref.shape[1], ssm_ref.shape[1]
    y = ssm_ref[...]
    gate = jax.nn.sigmoid(jnp.dot(y.astype(BF16), wglu_ref[...].astype(BF16), preferred_element_type=F32)
                          + bglu_ref[...])
    mixed_ref[:, 0:w_sb] = sb_ref[...]
    mixed_ref[:, w_sb:w_sb + w_ssm] = (y * gate).astype(BF16)
    mixed_ref[:, w_sb + w_ssm:] = df_ref[...]
    out = jnp.dot(mixed_ref[...], w_ref[...], preferred_element_type=F32)
    o_ref[...] = h_ref[...] + _rms(out, g_ref[...])


def _outproj(h, o_sb, y_ssm, o_diff, layer, w_glu, b_glu, w_out_bf, g, *, tm):
    m, d = h.shape
    k = w_out_bf.shape[1]
    row_block = lambda x: pl.BlockSpec((tm, x.shape[1]), lambda i: (i, 0))
    return pl.pallas_call(
        _outproj_kernel,
        grid=(m // tm,),
        in_specs=[
            row_block(h), row_block(o_sb), row_block(y_ssm), row_block(o_diff),
            pl.BlockSpec((None,) + w_glu.shape[1:], lambda i: (layer, 0, 0)),
            _layer_vec_spec(layer, SSM_WIDTH),
            pl.BlockSpec((None, k, d), lambda i: (layer, 0, 0), pipeline_mode=pl.Buffered(1)),
            _layer_vec_spec(layer, d),
        ],
        out_specs=row_block(h),
        out_shape=jax.ShapeDtypeStruct((m, d), F32),
        scratch_shapes=[pltpu.VMEM((tm, k), BF16)],
        compiler_params=_params(1),
        name="outproj",
    )(h, o_sb, y_ssm, o_diff, w_glu, _as_rows(b_glu), w_out_bf, _as_rows(g))


def _trunk(x, ffn1_pre_g, ffn1_w_gate, ffn1_w_up, ffn1_w_down, ffn1_post_g, mix_pre_g, w_in, ssm_a_re, ssm_a_im,
           ssm_log_dt, ssm_b_re, ssm_b_im, ssm_c_re, ssm_c_im, ssm_d, ssm_w_glu, ssm_b_glu, diff_lq1, diff_lk1,
           diff_lq2, diff_lk2, diff_subln_g, rel_bias, w_out, mix_post_g, ffn2_pre_g, ffn2_w_gate, ffn2_w_up,
           ffn2_w_down, ffn2_post_g, *, tm, tf, tm_out):
    batch, seq, d = x.shape
    m = batch * seq
    depth = w_in.shape[0]
    h = x.reshape(m, d)
    bias_tiles = _t5_bias_tiles(rel_bias)
    ssm_mats = _ssm_prep(ssm_a_re, ssm_a_im, ssm_log_dt, ssm_b_re, ssm_b_im, ssm_c_re, ssm_c_im)
    w_out_bf = _cast_bf16(w_out, rows=512)
    for l in range(depth):
        h = _ffn(h, l, ffn1_pre_g, ffn1_w_gate, ffn1_w_up, ffn1_w_down, ffn1_post_g, tm=tm, tf=tf)

        proj, ssm_u = _inproj(h, l, mix_pre_g, w_in, tm=tm)
        proj = proj.reshape(batch, seq, PROJ_COLS)

        o_sb = _sb_attention(proj, batch=batch, seq=seq)

        y_ssm = _ssm(_to_group_chunks(ssm_u, batch, seq), l, ssm_mats, ssm_d, n_batch=batch)
        y_ssm = _from_group_chunks(y_ssm, batch, seq)

        lambda_init = 0.8 - 0.6 * math.exp(-0.3 * l)
        o_diff = _diff_attention(proj, bias_tiles, l, diff_lq1, diff_lk1, diff_lq2, diff_lk2, diff_subln_g,
                                 batch=batch, seq=seq, lambda_init=lambda_init)

        h = _outproj(h, o_sb.reshape(m, -1), y_ssm, o_diff.reshape(m, -1), l, ssm_w_glu, ssm_b_glu, w_out_bf,
                     mix_post_g, tm=tm_out)

        h = _ffn(h, l, ffn2_pre_g, ffn2_w_gate, ffn2_w_up, ffn2_w_down, ffn2_post_g, tm=tm, tf=tf)
    return h.reshape(batch, seq, d)


def kernel(x, ffn1_pre_g, ffn1_w_gate, ffn1_w_up, ffn1_w_down, ffn1_post_g, mix_pre_g, w_in, ssm_a_re, ssm_a_im, ssm_log_dt, ssm_b_re, ssm_b_im, ssm_c_re, ssm_c_im, ssm_d, ssm_w_glu, ssm_b_glu, diff_lq1, diff_lk1, diff_lq2, diff_lk2, diff_subln_g, rel_bias, w_out, mix_post_g, ffn2_pre_g, ffn2_w_gate, ffn2_w_up, ffn2_w_down, ffn2_post_g):
    assert x.shape[1] % ATT_TILE == 0 and x.shape[1] // SSM_CHUNK <= 1 << SSM_SCAN_ROWS
    return _trunk(x, ffn1_pre_g, ffn1_w_gate, ffn1_w_up, ffn1_w_down, ffn1_post_g, mix_pre_g, w_in, ssm_a_re,
                  ssm_a_im, ssm_log_dt, ssm_b_re, ssm_b_im, ssm_c_re, ssm_c_im, ssm_d, ssm_w_glu, ssm_b_glu,
                  diff_lq1, diff_lk1, diff_lq2, diff_lk2, diff_subln_g, rel_bias, w_out, mix_post_g, ffn2_pre_g,
                  ffn2_w_gate, ffn2_w_up, ffn2_w_down, ffn2_post_g, tm=1024, tf=512, tm_out=512)
```

```python
import functools
import math

import jax
import jax.numpy as jnp
from jax import lax
from jax.experimental import pallas as pl
from jax.experimental.pallas import tpu as pltpu

F32 = jnp.float32
BF16 = jnp.bfloat16

HEAD_DIM = 128
SB_HEADS = 6
DIFF_HEADS = 6
DIFF_QK_DIM = HEAD_DIM // 2
SSM_GROUPS = 32
SSM_CH_PER_GROUP = 16
SSM_STATE = 64
SSM_WIDTH = SSM_GROUPS * SSM_CH_PER_GROUP
NUM_BUCKETS = 32
MAX_DISTANCE = 128
NORM_EPS = 1e-6
FFN_RESIDUAL_WEIGHT = 0.5
LOG2E = 1.0 / math.log(2.0)

SB_Q_BLK, SB_K_BLK, SB_V_BLK = 0, 6, 12
DF_Q_BLK, DF_K_BLK, DF_V_BLK = 18, 24, 30
PROJ_COLS = 36 * HEAD_DIM
SSM_COL0 = 3 * SB_HEADS * HEAD_DIM
W_IN_SUB = 256
PROJ_STEP_COLS = 3 * W_IN_SUB

V7X_VMEM_LIMIT = 60 * 1024 * 1024
FFN_SUB = 256
ROW_CHUNK = 128
ATT_TILE = 256
SSM_CHUNK = 16
SSM_SCAN_ROWS = 8
NEG_INF = float(jnp.finfo(jnp.float32).min)


def _params(n_grid, vmem=V7X_VMEM_LIMIT):
    return pltpu.CompilerParams(dimension_semantics=("arbitrary",) * n_grid, vmem_limit_bytes=vmem)


def _rms(x, g):
    ms = jnp.mean(x * x, axis=-1, keepdims=True)
    return x * lax.rsqrt(ms + NORM_EPS) * g


def _for_row_chunks(n_rows, chunk, body):
    def step(i, carry):
        body(pl.multiple_of(i * chunk, chunk))
        return carry
    lax.fori_loop(0, n_rows // chunk, step, 0)


def _layer_vec_spec(layer, n):
    return pl.BlockSpec((None, 1, n), lambda *_: (layer, 0, 0))


def _as_rows(p):
    return p.reshape(p.shape[0], 1, p.shape[1])


def _ffn_kernel(x_ref, pre_g_ref, wg_ref, wu_ref, wd_ref, post_g_ref, o_ref, xn_ref, a_ref, *, n_f, n_col_chunks):
    f = pl.program_id(1)
    tm, d = x_ref.shape

    @pl.when(f == 0)
    def _():
        def norm_rows(r0):
            rows = pl.ds(r0, ROW_CHUNK)
            xn_ref[rows, :] = _rms(x_ref[rows, :], pre_g_ref[...]).astype(BF16)
            o_ref[rows, :] = jnp.zeros((ROW_CHUNK, d), F32)
        _for_row_chunks(tm, ROW_CHUNK, norm_rows)

    xn = xn_ref[...]
    tf = wg_ref.shape[1]
    for hf in range(tf // FFN_SUB):
        sub = slice(hf * FFN_SUB, (hf + 1) * FFN_SUB)
        g = jnp.dot(xn, wg_ref[:, sub].astype(BF16), preferred_element_type=F32)
        u = jnp.dot(xn, wu_ref[:, sub].astype(BF16), preferred_element_type=F32)
        a_ref[:, sub] = (g * jax.nn.sigmoid(g) * u).astype(BF16)
    a = a_ref[...]
    cw = d // n_col_chunks
    for c in range(n_col_chunks):
        cols = slice(c * cw, (c + 1) * cw)
        o_ref[:, cols] += jnp.dot(a, wd_ref[:, cols].astype(BF16), preferred_element_type=F32)

    @pl.when(f == n_f - 1)
    def _():
        def finish_rows(r0):
            rows = pl.ds(r0, ROW_CHUNK)
            o_ref[rows, :] = x_ref[rows, :] + FFN_RESIDUAL_WEIGHT * _rms(o_ref[rows, :], post_g_ref[...])
        _for_row_chunks(tm, ROW_CHUNK, finish_rows)


def _ffn(h, layer, pre_g, w_gate, w_up, w_down, post_g, *, tm, tf):
    m, d = h.shape
    d_ff = w_gate.shape[2]
    n_f = d_ff // tf
    kern = functools.partial(_ffn_kernel, n_f=n_f, n_col_chunks=4)
    return pl.pallas_call(
        kern,
        grid=(m // tm, n_f),
        in_specs=[
            pl.BlockSpec((tm, d), lambda i, f: (i, 0)),
            _layer_vec_spec(layer, d),
            pl.BlockSpec((None, d, tf), lambda i, f: (layer, 0, f)),
            pl.BlockSpec((None, d, tf), lambda i, f: (layer, 0, f)),
            pl.BlockSpec((None, tf, d), lambda i, f: (layer, f, 0)),
            _layer_vec_spec(layer, d),
        ],
        out_specs=pl.BlockSpec((tm, d), lambda i, f: (i, 0), pipeline_mode=pl.Buffered(1)),
        out_shape=jax.ShapeDtypeStruct((m, d), F32),
        scratch_shapes=[pltpu.VMEM((tm, d), BF16), pltpu.VMEM((tm, tf), BF16)],
        compiler_params=_params(2),
        name="ffn",
    )(h, _as_rows(pre_g), w_gate, w_up, w_down, _as_rows(post_g))


def _inproj_kernel(x_ref, g_ref, w0_ref, w1_ref, w2_ref, proj_ref, u_ref, xn_ref, *, n_head_steps):
    j = pl.program_id(1)
    tm = x_ref.shape[0]
    sub = W_IN_SUB

    @pl.when(j == 0)
    def _():
        def norm_rows(r0):
            rows = pl.ds(r0, ROW_CHUNK)
            xn_ref[rows, :] = _rms(x_ref[rows, :], g_ref[...]).astype(BF16)
        _for_row_chunks(tm, ROW_CHUNK, norm_rows)

    @pl.when(j < n_head_steps)
    def _():
        xn = xn_ref[...]
        for c, w_ref in enumerate((w0_ref, w1_ref, w2_ref)):
            y = jnp.dot(xn, w_ref[...].astype(BF16), preferred_element_type=F32)
            proj_ref[:, c * sub:(c + 1) * sub] = y.astype(proj_ref.dtype)

    @pl.when(j == n_head_steps)
    def _():
        xn = xn_ref[...]
        for c, w_ref in enumerate((w0_ref, w1_ref)):
            u_ref[:, c * sub:(c + 1) * sub] = jnp.dot(xn, w_ref[...].astype(BF16), preferred_element_type=F32)


def _inproj(h, layer, g, w_in, *, tm):
    m, d = h.shape
    n_head_steps = PROJ_COLS // PROJ_STEP_COLS
    ssm_blk0 = SSM_COL0 // W_IN_SUB
    n_ssm_blks = SSM_WIDTH // W_IN_SUB
    last_blk = w_in.shape[2] // W_IN_SUB - 1

    def w_spec(c):
        def index(i, j):
            head_blk = 3 * j + c + jnp.where(3 * j + c >= ssm_blk0, n_ssm_blks, 0)
            blk = jnp.where(j < n_head_steps, head_blk, ssm_blk0 + c)
            return (layer, 0, jnp.minimum(blk, last_blk))
        return pl.BlockSpec((None, d, W_IN_SUB), index)

    kern = functools.partial(_inproj_kernel, n_head_steps=n_head_steps)
    return pl.pallas_call(
        kern,
        grid=(m // tm, n_head_steps + 1),
        in_specs=[
            pl.BlockSpec((tm, d), lambda i, j: (i, 0)),
            _layer_vec_spec(layer, d),
            w_spec(0), w_spec(1), w_spec(2),
        ],
        out_specs=[
            pl.BlockSpec((tm, PROJ_STEP_COLS), lambda i, j: (i, jnp.minimum(j, n_head_steps - 1))),
            pl.BlockSpec((tm, SSM_WIDTH), lambda i, j: (i, 0)),
        ],
        out_shape=[jax.ShapeDtypeStruct((m, PROJ_COLS), BF16), jax.ShapeDtypeStruct((m, SSM_WIDTH), F32)],
        scratch_shapes=[pltpu.VMEM((tm, d), BF16)],
        compiler_params=_params(2),
        name="inproj",
    )(h, _as_rows(g), w_in, w_in, w_in)


def _split2_bf16(x):
    hi = x.astype(BF16)
    lo = (x - hi.astype(F32)).astype(BF16)
    return hi, lo


def _mask_first_tile(x, keep):
    t = ATT_TILE
    head = jnp.where(keep, x[0:t], 0.0)
    return head if x.shape[0] == t else jnp.concatenate([head, x[t:]], axis=0)


def _sb_kernel(q_ref, k_ref, v_ref, o_ref):
    t = ATT_TILE
    seq = q_ref.shape[0]
    n_tiles = seq // t
    scale2 = LOG2E / math.sqrt(HEAD_DIM)
    row = lax.broadcasted_iota(jnp.int32, (t, t), 0)
    col = lax.broadcasted_iota(jnp.int32, (t, t), 1)
    strictly_lower = row > col
    ones = jnp.where(strictly_lower, 1.0, 0.0).astype(BF16)
    suffix_ones = jnp.concatenate([ones, ones], axis=0)
    acc = [None] * n_tiles
    later = [None] * n_tiles

    for j in range(n_tiles - 1, -1, -1):
        r0 = j * t
        k = k_ref[r0:r0 + t, :]
        v = v_ref[r0:r0 + t, :]
        z = lax.dot_general(q_ref[r0:seq, :], k, (((1,), (1,)), ((), ())), preferred_element_type=F32) * scale2
        pos = jnp.maximum(z, 0.0)
        neg = jnp.minimum(z, 0.0)
        l2 = jnp.log2(1.0 + jnp.exp2(neg - pos))
        sp = pos + l2
        log_beta = neg - l2
        sp_in = _mask_first_tile(sp, strictly_lower)
        hi, lo = _split2_bf16(sp_in)
        suffix = jnp.dot(jnp.concatenate([hi, lo], axis=1), suffix_ones, preferred_element_type=F32)
        later_rows = jnp.concatenate([jnp.zeros((t, 1), F32)] + later[j + 1:], axis=0)
        w = jnp.exp2(log_beta - suffix - later_rows)
        w = _mask_first_tile(w, strictly_lower)
        pv = jnp.dot(w.astype(BF16), v, preferred_element_type=F32)
        row_sum = jnp.sum(sp_in, axis=1, keepdims=True)
        for i in range(j, n_tiles):
            rows = slice((i - j) * t, (i - j + 1) * t)
            acc[i] = pv[rows] if i == j else acc[i] + pv[rows]
            later[i] = row_sum[rows] if i == j else later[i] + row_sum[rows]

    for i in range(n_tiles):
        o_ref[i * t:(i + 1) * t, :] = acc[i].astype(o_ref.dtype)


def _sb_attention(proj, *, batch, seq):
    spec = lambda blk: pl.BlockSpec((None, seq, HEAD_DIM), lambda b, h: (b, 0, blk + h))
    return pl.pallas_call(
        _sb_kernel,
        grid=(batch, SB_HEADS),
        in_specs=[spec(SB_Q_BLK), spec(SB_K_BLK), spec(SB_V_BLK)],
        out_specs=spec(0),
        out_shape=jax.ShapeDtypeStruct((batch, seq, SB_HEADS * HEAD_DIM), BF16),
        compiler_params=_params(2),
        name="sb_attention",
    )(proj, proj, proj)


def _t5_bias_kernel(rel_ref, o_ref):
    t = ATT_TILE
    h = pl.program_id(0)
    row = lax.broadcasted_iota(jnp.int32, (t, t), 0)
    col = lax.broadcasted_iota(jnp.int32, (t, t), 1)
    max_exact = NUM_BUCKETS // 2
    for d in range(3):
        n = jnp.maximum(d * t + row - col, 0)
        nf = jnp.maximum(n, 1).astype(F32)
        large = max_exact + (jnp.log(nf / max_exact) / math.log(MAX_DISTANCE / max_exact)
                             * (NUM_BUCKETS - max_exact)).astype(jnp.int32)
        large = jnp.minimum(large, NUM_BUCKETS - 1)
        bucket = jnp.where(n < max_exact, n, large)
        bias = jnp.zeros((t, t), F32)
        for b in range(NUM_BUCKETS):
            bias = jnp.where(bucket == b, rel_ref[b, h], bias)
        o_ref[d] = bias * LOG2E


def _t5_bias_tiles(rel_bias):
    t = ATT_TILE
    return pl.pallas_call(
        _t5_bias_kernel,
        grid=(DIFF_HEADS,),
        in_specs=[pl.BlockSpec(memory_space=pltpu.SMEM)],
        out_specs=pl.BlockSpec((None, 3, t, t), lambda h: (h, 0, 0, 0)),
        out_shape=jax.ShapeDtypeStruct((DIFF_HEADS, 3, t, t), F32),
        compiler_params=_params(1),
        name="t5_bias_tiles",
    )(rel_bias)


def _diff_kernel(lq1_ref, lk1_ref, lq2_ref, lk2_ref, subln_g_ref, bias_ref, q_ref, k_ref, v_ref, o_ref,
                 *, lambda_init):
    t = ATT_TILE
    seq = q_ref.shape[0]
    n_tiles = seq // t
    scale2 = LOG2E / math.sqrt(DIFF_QK_DIM)
    lane = lax.broadcasted_iota(jnp.int32, (t, HEAD_DIM), 1)
    row = lax.broadcasted_iota(jnp.int32, (t, t), 0)
    col = lax.broadcasted_iota(jnp.int32, (t, t), 1)
    causal = col <= row
    lam = (jnp.exp(jnp.sum(lq1_ref[...] * lk1_ref[...], axis=1, keepdims=True))
           - jnp.exp(jnp.sum(lq2_ref[...] * lk2_ref[...], axis=1, keepdims=True)) + lambda_init)

    for i in range(n_tiles):
        n_keys = (i + 1) * t
        q = q_ref[i * t:(i + 1) * t, :]
        zero = jnp.zeros_like(q)
        q_maps = (jnp.where(lane < DIFF_QK_DIM, q, zero), jnp.where(lane >= DIFF_QK_DIM, q, zero))
        k = k_ref[0:n_keys, :]
        exps, sums = [], []
        for qm in q_maps:
            raw = lax.dot_general(qm, k, (((1,), (1,)), ((), ())), preferred_element_type=F32)
            parts = []
            for j in range(i + 1):
                s_j = raw[:, j * t:(j + 1) * t] * scale2 + bias_ref[min(i - j, 2)]
                parts.append(jnp.where(causal, s_j, NEG_INF) if j == i else s_j)
            s = parts[0] if i == 0 else jnp.concatenate(parts, axis=1)
            e = jnp.exp2(s - jnp.max(s, axis=1, keepdims=True))
            exps.append(e)
            sums.append(jnp.sum(e, axis=1, keepdims=True))
        p = exps[0] - exps[1] * (lam * sums[0] / sums[1])
        o = jnp.dot(p.astype(BF16), v_ref[0:n_keys, :], preferred_element_type=F32) * (1.0 / sums[0])
        o_ref[i * t:(i + 1) * t, :] = (_rms(o, subln_g_ref[...]) * (1.0 - lambda_init)).astype(o_ref.dtype)


def _diff_attention(proj, bias_tiles, layer, lq1, lk1, lq2, lk2, subln_g, *, batch, seq, lambda_init):
    t = ATT_TILE
    vec = lambda n: _layer_vec_spec(layer, n)
    spec = lambda blk: pl.BlockSpec((None, seq, HEAD_DIM), lambda b, h: (b, 0, blk + h))
    kern = functools.partial(_diff_kernel, lambda_init=lambda_init)
    return pl.pallas_call(
        kern,
        grid=(batch, DIFF_HEADS),
        in_specs=[
            vec(DIFF_QK_DIM), vec(DIFF_QK_DIM), vec(DIFF_QK_DIM), vec(DIFF_QK_DIM), vec(HEAD_DIM),
            pl.BlockSpec((None, 3, t, t), lambda b, h: (h, 0, 0, 0)),
            spec(DF_Q_BLK), spec(DF_K_BLK), spec(DF_V_BLK),
        ],
        out_specs=spec(0),
        out_shape=jax.ShapeDtypeStruct((batch, seq, DIFF_HEADS * HEAD_DIM), BF16),
        compiler_params=_params(2),
        name="diff_attention",
    )(_as_rows(lq1), _as_rows(lk1), _as_rows(lq2), _as_rows(lk2), _as_rows(subln_g), bias_tiles, proj, proj, proj)


def _cpow(n, ar, ai, dt):
    mag = jnp.exp(n * (ar * dt))
    ang = n * (ai * dt)
    return mag * jnp.cos(ang), mag * jnp.sin(ang)


def _repeat_rows(x, reps):
    n, w = x.shape
    return jnp.broadcast_to(x[:, None, :], (n, reps, w)).reshape(n * reps, w)


def _ssm_prep_kernel(ar_ref, ai_ref, ldt_ref, bt_re_ref, bt_im_ref, ct_re_ref, ct_im_ref,
                     m_intra_ref, m_in_ref, m_out_ref, p1_ref, p2_ref):
    t = SSM_CHUNK
    cpg = SSM_CH_PER_GROUP
    ar, ai = ar_ref[...], ai_ref[...]
    dt = jnp.exp(ldt_ref[...])
    lb_re, lb_im = _cpow(1.0, ar, ai, dt)
    den = ar * ar + ai * ai
    f_re = ((lb_re - 1.0) * ar + lb_im * ai) / den
    f_im = (lb_im * ar - (lb_re - 1.0) * ai) / den
    bt_re, bt_im = bt_re_ref[...], bt_im_ref[...]
    bb_re = f_re * bt_re - f_im * bt_im
    bb_im = f_re * bt_im + f_im * bt_re
    ct_re, ct_im = ct_re_ref[...], ct_im_ref[...]
    pos = lax.broadcasted_iota(jnp.int32, (t, 2 * SSM_STATE), 0).astype(F32)
    lane_is_re = lax.broadcasted_iota(jnp.int32, (t * cpg, 2 * SSM_STATE), 1) < SSM_STATE

    def times_power(n, x_re, x_im, im_sign):
        e_re, e_im = _cpow(n, ar, ai, dt)
        e_re, e_im = _repeat_rows(e_re, cpg), _repeat_rows(e_im, cpg)
        return jnp.where(lane_is_re, e_re * x_re - e_im * x_im, im_sign * (e_re * x_im + e_im * x_re))

    m_in_ref[...] = times_power((t - 1.0) - pos, bb_re, bb_im, 1.0).astype(BF16)
    m_out_ref[...] = times_power(pos + 1.0, ct_re, ct_im, -1.0).astype(BF16)
    p_packed = times_power(-pos, bb_re, bb_im, -1.0)
    q_packed = times_power(pos, ct_re, ct_im, 1.0)
    k_full = lax.dot_general(p_packed, q_packed, (((1,), (1,)), ((), ())), preferred_element_type=F32,
                             precision=lax.Precision.HIGHEST)
    shift = cpg.bit_length() - 1
    row_k = lax.broadcasted_iota(jnp.int32, k_full.shape, 0)
    lane_k = lax.broadcasted_iota(jnp.int32, k_full.shape, 1)
    causal = jnp.right_shift(lane_k, shift) >= jnp.right_shift(row_k, shift)
    m_intra_ref[...] = jnp.where(causal, k_full, 0.0).astype(BF16)

    row_j = lax.broadcasted_iota(jnp.int32, (SSM_SCAN_ROWS, 2 * SSM_STATE), 0)
    lane_j = lax.broadcasted_iota(jnp.int32, (SSM_SCAN_ROWS, 2 * SSM_STATE), 1)
    pr, pi = _cpow(jnp.left_shift(t, row_j).astype(F32), ar, ai, dt)
    p1_ref[...] = pr
    p2_ref[...] = jnp.where(lane_j < SSM_STATE, -pi, pi)


def _ssm_prep(a_re, a_im, log_dt, b_re, b_im, c_re, c_im):
    nl, g, p = a_re.shape
    t, cpg = SSM_CHUNK, SSM_CH_PER_GROUP
    tc = t * cpg
    twice = lambda x: jnp.concatenate([x, x], axis=-1)
    rows = [twice(x)[:, :, None, :] for x in (a_re, a_im, jnp.broadcast_to(log_dt[:, :, None], a_re.shape))]
    bt = [jnp.tile(twice(jnp.swapaxes(b, 2, 3)), (1, 1, t, 1)) for b in (b_re, b_im)]
    ct = [jnp.tile(twice(c), (1, 1, t, 1)) for c in (c_re, c_im)]
    blk = lambda shape: pl.BlockSpec((None, None) + shape, lambda l, gi: (l, gi, 0, 0))
    return pl.pallas_call(
        _ssm_prep_kernel,
        grid=(nl, g),
        in_specs=[blk((1, 2 * p))] * 3 + [blk((tc, 2 * p))] * 4,
        out_specs=[blk((tc, tc)), blk((tc, 2 * p)), blk((tc, 2 * p)),
                   blk((SSM_SCAN_ROWS, 2 * p)), blk((SSM_SCAN_ROWS, 2 * p))],
        out_shape=[jax.ShapeDtypeStruct((nl, g, tc, tc), BF16),
                   jax.ShapeDtypeStruct((nl, g, tc, 2 * p), BF16),
                   jax.ShapeDtypeStruct((nl, g, tc, 2 * p), BF16),
                   jax.ShapeDtypeStruct((nl, g, SSM_SCAN_ROWS, 2 * p), F32),
                   jax.ShapeDtypeStruct((nl, g, SSM_SCAN_ROWS, 2 * p), F32)],
        compiler_params=_params(2),
        name="ssm_prep",
    )(*rows, *bt, *ct)


def _gelu_tanh(x):
    return 0.5 * x * (1.0 + jnp.tanh(math.sqrt(2.0 / math.pi) * (x + 0.044715 * (x * x * x))))


def _ssm_kernel(u_ref, m_intra_ref, m_in_ref, m_out_ref, p1_ref, p2_ref, d_ref, o_ref, *, n_chunks):
    t, cpg = SSM_CHUNK, SSM_CH_PER_GROUP
    n_rows = u_ref.shape[0] // t
    groups = u_ref.shape[1] // cpg
    by_pos = [u_ref[pl.ds(tau, n_rows, stride=t), :] for tau in range(t)]
    row = lax.broadcasted_iota(jnp.int32, (n_rows, 2 * SSM_STATE), 0)
    chunk = jnp.bitwise_and(row, n_chunks - 1)

    def chunks_back(x, n):
        return jnp.where(chunk >= n, pltpu.roll(x, n, axis=0), 0.0)

    ys = []
    for g in range(groups):
        lanes = slice(g * cpg, (g + 1) * cpg)
        u = jnp.concatenate([x[:, lanes] for x in by_pos], axis=1)
        ub = u.astype(BF16)
        y = jnp.dot(ub, m_intra_ref[g], preferred_element_type=F32)
        s_in = jnp.dot(ub, m_in_ref[g], preferred_element_type=F32)
        x = chunks_back(s_in, 1)
        for j in range((n_chunks - 1).bit_length()):
            prev = chunks_back(x, 1 << j)
            x = x + p1_ref[g, j:j + 1, :] * prev + p2_ref[g, j:j + 1, :] * pltpu.roll(prev, SSM_STATE, axis=1)
        y_state = lax.dot_general(x.astype(BF16), m_out_ref[g], (((1,), (1,)), ((), ())),
                                  preferred_element_type=F32)
        ys.append(_gelu_tanh(y + y_state + d_ref[g] * u))
    for tau in range(t):
        lanes = slice(tau * cpg, (tau + 1) * cpg)
        o_ref[pl.ds(tau, n_rows, stride=t), :] = jnp.concatenate([y[:, lanes] for y in ys], axis=1)


def _ssm(u, layer, mats, d_skip, *, n_chunks):
    m, width = u.shape
    m_intra, m_in, m_out, p1, p2 = mats
    gps = 128 // SSM_CH_PER_GROUP
    lg = lambda x: pl.BlockSpec((None, gps) + x.shape[2:], lambda gi: (layer, gi, 0, 0))
    d_tiled = jnp.tile(d_skip, (1, 1, SSM_CHUNK))[:, :, None, :]
    kern = functools.partial(_ssm_kernel, n_chunks=n_chunks)
    return pl.pallas_call(
        kern,
        grid=(width // 128,),
        in_specs=[pl.BlockSpec((m, 128), lambda gi: (0, gi)),
                  lg(m_intra), lg(m_in), lg(m_out), lg(p1), lg(p2), lg(d_tiled)],
        out_specs=pl.BlockSpec((m, 128), lambda gi: (0, gi)),
        out_shape=jax.ShapeDtypeStruct(u.shape, F32),
        compiler_params=_params(1),
        name="ssm_chunks",
    )(u, m_intra, m_in, m_out, p1, p2, d_tiled)


def _cast_kernel(x_ref, o_ref):
    o_ref[...] = x_ref[...].astype(o_ref.dtype)


def _cast_bf16(w, *, rows):
    nl, r, c = w.shape
    return pl.pallas_call(
        _cast_kernel,
        grid=(nl, r // rows),
        in_specs=[pl.BlockSpec((None, rows, c), lambda l, i: (l, i, 0))],
        out_specs=pl.BlockSpec((None, rows, c), lambda l, i: (l, i, 0)),
        out_shape=jax.ShapeDtypeStruct(w.shape, BF16),
        compiler_params=_params(2),
        name="cast_bf16",
    )(w)


def _outproj_kernel(h_ref, sb_ref, ssm_ref, df_ref, wglu_ref, bglu_ref, w_ref, g_ref, o_ref, mixed_ref):
    w_sb, w_ssm = sb_ref.shape[1], ssm_ref.shape[1]
    y = ssm_ref[...]
    gate = jax.nn.sigmoid(jnp.dot(y.astype(BF16), wglu_ref[...].astype(BF16), preferred_element_type=F32)
                          + bglu_ref[...])
    mixed_ref[:, 0:w_sb] = sb_ref[...]
    mixed_ref[:, w_sb:w_sb + w_ssm] = (y * gate).astype(BF16)
    mixed_ref[:, w_sb + w_ssm:] = df_ref[...]
    out = jnp.dot(mixed_ref[...], w_ref[...], preferred_element_type=F32)
    o_ref[...] = h_ref[...] + _rms(out, g_ref[...])


def _outproj(h, o_sb, y_ssm, o_diff, layer, w_glu, b_glu, w_out_bf, g, *, tm):
    m, d = h.shape
    k = w_out_bf.shape[1]
    row_block = lambda x: pl.BlockSpec((tm, x.shape[1]), lambda i: (i, 0))
    return pl.pallas_call(
        _outproj_kernel,
        grid=(m // tm,),
        in_specs=[
            row_block(h), row_block(o_sb), row_block(y_ssm), row_block(o_diff),
            pl.BlockSpec((None,) + w_glu.shape[1:], lambda i: (layer, 0, 0)),
            _layer_vec_spec(layer, SSM_WIDTH),
            pl.BlockSpec((None, k, d), lambda i: (layer, 0, 0), pipeline_mode=pl.Buffered(1)),
            _layer_vec_spec(layer, d),
        ],
        out_specs=row_block(h),
        out_shape=jax.ShapeDtypeStruct((m, d), F32),
        scratch_shapes=[pltpu.VMEM((tm, k), BF16)],
        compiler_params=_params(1),
        name="outproj",
    )(h, o_sb, y_ssm, o_diff, w_glu, _as_rows(b_glu), w_out_bf, _as_rows(g))


def _trunk(x, ffn1_pre_g, ffn1_w_gate, ffn1_w_up, ffn1_w_down, ffn1_post_g, mix_pre_g, w_in, ssm_a_re, ssm_a_im,
           ssm_log_dt, ssm_b_re, ssm_b_im, ssm_c_re, ssm_c_im, ssm_d, ssm_w_glu, ssm_b_glu, diff_lq1, diff_lk1,
           diff_lq2, diff_lk2, diff_subln_g, rel_bias, w_out, mix_post_g, ffn2_pre_g, ffn2_w_gate, ffn2_w_up,
           ffn2_w_down, ffn2_post_g, *, tm, tf, tm_out):
    batch, seq, d = x.shape
    m = batch * seq
    depth = w_in.shape[0]
    h = x.reshape(m, d)
    bias_tiles = _t5_bias_tiles(rel_bias)
    ssm_mats = _ssm_prep(ssm_a_re, ssm_a_im, ssm_log_dt, ssm_b_re, ssm_b_im, ssm_c_re, ssm_c_im)
    w_out_bf = _cast_bf16(w_out, rows=512)
    for l in range(depth):
        h = _ffn(h, l, ffn1_pre_g, ffn1_w_gate, ffn1_w_up, ffn1_w_down, ffn1_post_g, tm=tm, tf=tf)

        proj, ssm_u = _inproj(h, l, mix_pre_g, w_in, tm=tm)
        proj = proj.reshape(batch, seq, PROJ_COLS)

        o_sb = _sb_attention(proj, batch=batch, seq=seq)

        y_ssm = _ssm(ssm_u, l, ssm_mats, ssm_d, n_chunks=seq // SSM_CHUNK)

        lambda_init = 0.8 - 0.6 * math.exp(-0.3 * l)
        o_diff = _diff_attention(proj, bias_tiles, l, diff_lq1, diff_lk1, diff_lq2, diff_lk2, diff_subln_g,
                                 batch=batch, seq=seq, lambda_init=lambda_init)

        h = _outproj(h, o_sb.reshape(m, -1), y_ssm, o_diff.reshape(m, -1), l, ssm_w_glu, ssm_b_glu, w_out_bf,
                     mix_post_g, tm=tm_out)

        h = _ffn(h, l, ffn2_pre_g, ffn2_w_gate, ffn2_w_up, ffn2_w_down, ffn2_post_g, tm=tm, tf=tf)
    return h.reshape(batch, seq, d)


def kernel(x, ffn1_pre_g, ffn1_w_gate, ffn1_w_up, ffn1_w_down, ffn1_post_g, mix_pre_g, w_in, ssm_a_re, ssm_a_im, ssm_log_dt, ssm_b_re, ssm_b_im, ssm_c_re, ssm_c_im, ssm_d, ssm_w_glu, ssm_b_glu, diff_lq1, diff_lk1, diff_lq2, diff_lk2, diff_subln_g, rel_bias, w_out, mix_post_g, ffn2_pre_g, ffn2_w_gate, ffn2_w_up, ffn2_w_down, ffn2_post_g):
    n_chunks = x.shape[1] // SSM_CHUNK
    assert x.shape[1] % ATT_TILE == 0 and n_chunks <= 1 << SSM_SCAN_ROWS and n_chunks & (n_chunks - 1) == 0
    return _trunk(x, ffn1_pre_g, ffn1_w_gate, ffn1_w_up, ffn1_w_down, ffn1_post_g, mix_pre_g, w_in, ssm_a_re,
                  ssm_a_im, ssm_log_dt, ssm_b_re, ssm_b_im, ssm_c_re, ssm_c_im, ssm_d, ssm_w_glu, ssm_b_glu,
                  diff_lq1, diff_lk1, diff_lq2, diff_lk2, diff_subln_g, rel_bias, w_out, mix_post_g, ffn2_pre_g,
                  ffn2_w_gate, ffn2_w_up, ffn2_w_down, ffn2_post_g, tm=1024, tf=512, tm_out=512)
```

```python
import functools
import math

import jax
import jax.numpy as jnp
from jax import lax
from jax.experimental import pallas as pl
from jax.experimental.pallas import tpu as pltpu

F32 = jnp.float32
BF16 = jnp.bfloat16

HEAD_DIM = 128
SB_HEADS = 6
DIFF_HEADS = 6
DIFF_QK_DIM = HEAD_DIM // 2
SSM_GROUPS = 32
SSM_CH_PER_GROUP = 16
SSM_STATE = 64
SSM_WIDTH = SSM_GROUPS * SSM_CH_PER_GROUP
NUM_BUCKETS = 32
MAX_DISTANCE = 128
NORM_EPS = 1e-6
FFN_RESIDUAL_WEIGHT = 0.5
LOG2E = 1.0 / math.log(2.0)

SB_Q_BLK, SB_K_BLK, SB_V_BLK = 0, 6, 12
DF_Q_BLK, DF_K_BLK, DF_V_BLK = 18, 24, 30
PROJ_COLS = 36 * HEAD_DIM
SSM_COL0 = 3 * SB_HEADS * HEAD_DIM
W_IN_SUB = 256
PROJ_STEP_COLS = 3 * W_IN_SUB

V7X_VMEM_LIMIT = 60 * 1024 * 1024
FFN_SUB = 256
ROW_CHUNK = 128
ATT_TILE = 256
SB_NEAR_TILES = 2
SB_SATURATED = 151.0
SSM_CHUNK = 16
SSM_SCAN_ROWS = 8
NEG_INF = float(jnp.finfo(jnp.float32).min)


def _params(n_grid, vmem=V7X_VMEM_LIMIT):
    return pltpu.CompilerParams(dimension_semantics=("arbitrary",) * n_grid, vmem_limit_bytes=vmem)


def _rms(x, g):
    ms = jnp.mean(x * x, axis=-1, keepdims=True)
    return x * lax.rsqrt(ms + NORM_EPS) * g


def _for_row_chunks(n_rows, chunk, body):
    def step(i, carry):
        body(pl.multiple_of(i * chunk, chunk))
        return carry
    lax.fori_loop(0, n_rows // chunk, step, 0)


def _layer_vec_spec(layer, n):
    return pl.BlockSpec((None, 1, n), lambda *_: (layer, 0, 0))


def _as_rows(p):
    return p.reshape(p.shape[0], 1, p.shape[1])


def _ffn_kernel(x_ref, pre_g_ref, wg_ref, wu_ref, wd_ref, post_g_ref, o_ref, xn_ref, a_ref, *, n_f, n_col_chunks):
    f = pl.program_id(1)
    tm, d = x_ref.shape

    @pl.when(f == 0)
    def _():
        def norm_rows(r0):
            rows = pl.ds(r0, ROW_CHUNK)
            xn_ref[rows, :] = _rms(x_ref[rows, :], pre_g_ref[...]).astype(BF16)
            o_ref[rows, :] = jnp.zeros((ROW_CHUNK, d), F32)
        _for_row_chunks(tm, ROW_CHUNK, norm_rows)

    xn = xn_ref[...]
    tf = wg_ref.shape[1]
    for hf in range(tf // FFN_SUB):
        sub = slice(hf * FFN_SUB, (hf + 1) * FFN_SUB)
        g = jnp.dot(xn, wg_ref[:, sub].astype(BF16), preferred_element_type=F32)
        u = jnp.dot(xn, wu_ref[:, sub].astype(BF16), preferred_element_type=F32)
        a_ref[:, sub] = (g * jax.nn.sigmoid(g) * u).astype(BF16)
    a = a_ref[...]
    cw = d // n_col_chunks
    for c in range(n_col_chunks):
        cols = slice(c * cw, (c + 1) * cw)
        o_ref[:, cols] += jnp.dot(a, wd_ref[:, cols].astype(BF16), preferred_element_type=F32)

    @pl.when(f == n_f - 1)
    def _():
        def finish_rows(r0):
            rows = pl.ds(r0, ROW_CHUNK)
            o_ref[rows, :] = x_ref[rows, :] + FFN_RESIDUAL_WEIGHT * _rms(o_ref[rows, :], post_g_ref[...])
        _for_row_chunks(tm, ROW_CHUNK, finish_rows)


def _ffn(h, layer, pre_g, w_gate, w_up, w_down, post_g, *, tm, tf):
    m, d = h.shape
    d_ff = w_gate.shape[2]
    n_f = d_ff // tf
    kern = functools.partial(_ffn_kernel, n_f=n_f, n_col_chunks=4)
    return pl.pallas_call(
        kern,
        grid=(m // tm, n_f),
        in_specs=[
            pl.BlockSpec((tm, d), lambda i, f: (i, 0)),
            _layer_vec_spec(layer, d),
            pl.BlockSpec((None, d, tf), lambda i, f: (layer, 0, f)),
            pl.BlockSpec((None, d, tf), lambda i, f: (layer, 0, f)),
            pl.BlockSpec((None, tf, d), lambda i, f: (layer, f, 0)),
            _layer_vec_spec(layer, d),
        ],
        out_specs=pl.BlockSpec((tm, d), lambda i, f: (i, 0), pipeline_mode=pl.Buffered(1)),
        out_shape=jax.ShapeDtypeStruct((m, d), F32),
        scratch_shapes=[pltpu.VMEM((tm, d), BF16), pltpu.VMEM((tm, tf), BF16)],
        compiler_params=_params(2),
        name="ffn",
    )(h, _as_rows(pre_g), w_gate, w_up, w_down, _as_rows(post_g))


def _inproj_kernel(x_ref, g_ref, w0_ref, w1_ref, w2_ref, proj_ref, u_ref, xn_ref, *, n_head_steps):
    j = pl.program_id(1)
    tm = x_ref.shape[0]
    sub = W_IN_SUB

    @pl.when(j == 0)
    def _():
        def norm_rows(r0):
            rows = pl.ds(r0, ROW_CHUNK)
            xn_ref[rows, :] = _rms(x_ref[rows, :], g_ref[...]).astype(BF16)
        _for_row_chunks(tm, ROW_CHUNK, norm_rows)

    @pl.when(j < n_head_steps)
    def _():
        xn = xn_ref[...]
        for c, w_ref in enumerate((w0_ref, w1_ref, w2_ref)):
            y = jnp.dot(xn, w_ref[...].astype(BF16), preferred_element_type=F32)
            proj_ref[:, c * sub:(c + 1) * sub] = y.astype(proj_ref.dtype)

    @pl.when(j == n_head_steps)
    def _():
        xn = xn_ref[...]
        for c, w_ref in enumerate((w0_ref, w1_ref)):
            u_ref[:, c * sub:(c + 1) * sub] = jnp.dot(xn, w_ref[...].astype(BF16), preferred_element_type=F32)


def _inproj(h, layer, g, w_in, *, tm):
    m, d = h.shape
    n_head_steps = PROJ_COLS // PROJ_STEP_COLS
    ssm_blk0 = SSM_COL0 // W_IN_SUB
    n_ssm_blks = SSM_WIDTH // W_IN_SUB
    last_blk = w_in.shape[2] // W_IN_SUB - 1

    def w_spec(c):
        def index(i, j):
            head_blk = 3 * j + c + jnp.where(3 * j + c >= ssm_blk0, n_ssm_blks, 0)
            blk = jnp.where(j < n_head_steps, head_blk, ssm_blk0 + c)
            return (layer, 0, jnp.minimum(blk, last_blk))
        return pl.BlockSpec((None, d, W_IN_SUB), index)

    kern = functools.partial(_inproj_kernel, n_head_steps=n_head_steps)
    return pl.pallas_call(
        kern,
        grid=(m // tm, n_head_steps + 1),
        in_specs=[
            pl.BlockSpec((tm, d), lambda i, j: (i, 0)),
            _layer_vec_spec(layer, d),
            w_spec(0), w_spec(1), w_spec(2),
        ],
        out_specs=[
            pl.BlockSpec((tm, PROJ_STEP_COLS), lambda i, j: (i, jnp.minimum(j, n_head_steps - 1))),
            pl.BlockSpec((tm, SSM_WIDTH), lambda i, j: (i, 0)),
        ],
        out_shape=[jax.ShapeDtypeStruct((m, PROJ_COLS), BF16), jax.ShapeDtypeStruct((m, SSM_WIDTH), F32)],
        scratch_shapes=[pltpu.VMEM((tm, d), BF16)],
        compiler_params=_params(2),
        name="inproj",
    )(h, _as_rows(g), w_in, w_in, w_in)


def _split2_bf16(x):
    hi = x.astype(BF16)
    lo = (x - hi.astype(F32)).astype(BF16)
    return hi, lo


def _mask_first_tile(x, keep):
    t = ATT_TILE
    head = jnp.where(keep, x[0:t], 0.0)
    return head if x.shape[0] == t else jnp.concatenate([head, x[t:]], axis=0)


def _sb_tile(q, k, v, later_rows, suffix_ones, first_tile_mask):
    scale2 = LOG2E / math.sqrt(HEAD_DIM)
    z = lax.dot_general(q, k, (((1,), (1,)), ((), ())), preferred_element_type=F32) * scale2
    pos = jnp.maximum(z, 0.0)
    neg = jnp.minimum(z, 0.0)
    l2 = jnp.log2(1.0 + jnp.exp2(neg - pos))
    sp = pos + l2
    log_beta = neg - l2
    sp_in = sp if first_tile_mask is None else _mask_first_tile(sp, first_tile_mask)
    hi, lo = _split2_bf16(sp_in)
    suffix = jnp.dot(jnp.concatenate([hi, lo], axis=1), suffix_ones, preferred_element_type=F32)
    w = jnp.exp2(log_beta - suffix - later_rows)
    if first_tile_mask is not None:
        w = _mask_first_tile(w, first_tile_mask)
    pv = jnp.dot(w.astype(BF16), v, preferred_element_type=F32)
    return pv, jnp.sum(sp_in, axis=1, keepdims=True)


def _sb_kernel(q_ref, k_ref, v_ref, o_ref, acc_ref, later_ref):
    t = ATT_TILE
    seq = q_ref.shape[0]
    n_tiles = seq // t
    row = lax.broadcasted_iota(jnp.int32, (t, t), 0)
    col = lax.broadcasted_iota(jnp.int32, (t, t), 1)
    strictly_lower = row > col
    ones = jnp.where(strictly_lower, 1.0, 0.0).astype(BF16)
    suffix_ones = jnp.concatenate([ones, ones], axis=0)
    acc = [None] * n_tiles
    later = [None] * n_tiles

    for j in range(n_tiles - 1, -1, -1):
        r0 = j * t
        r1 = min(r0 + SB_NEAR_TILES * t, seq)
        later_rows = jnp.concatenate([jnp.zeros((t, 1), F32)] + later[j + 1:j + SB_NEAR_TILES], axis=0)
        pv, row_sum = _sb_tile(q_ref[r0:r1, :], k_ref[r0:r0 + t, :], v_ref[r0:r0 + t, :], later_rows, suffix_ones,
                               strictly_lower)
        for i in range(j, r1 // t):
            rows = slice((i - j) * t, (i - j + 1) * t)
            acc[i] = pv[rows] if i == j else acc[i] + pv[rows]
            later[i] = row_sum[rows] if i == j else later[i] + row_sum[rows]

    for i in range(n_tiles):
        acc_ref[i * t:(i + 1) * t, :] = acc[i]
        later_ref[i * t:(i + 1) * t, :] = later[i]

    def far_tiles_of(i, carry):
        rows = pl.ds(pl.multiple_of(i * t, t), t)
        n_far = i + 1 - SB_NEAR_TILES

        def far_tile(n, inner):
            k0 = pl.multiple_of((n_far - 1 - n) * t, t)
            pv, row_sum = _sb_tile(q_ref[rows, :], k_ref[pl.ds(k0, t), :], v_ref[pl.ds(k0, t), :], later_ref[rows, :],
                                   suffix_ones, None)
            acc_ref[rows, :] += pv
            later_ref[rows, :] += row_sum
            return inner

        lax.fori_loop(0, jnp.where(jnp.min(later_ref[rows, :]) < SB_SATURATED, n_far, 0), far_tile, 0)
        return carry

    lax.fori_loop(SB_NEAR_TILES, n_tiles, far_tiles_of, 0)
    o_ref[...] = acc_ref[...].astype(o_ref.dtype)


def _sb_attention(proj, *, batch, seq):
    spec = lambda blk: pl.BlockSpec((None, seq, HEAD_DIM), lambda b, h: (b, 0, blk + h))
    return pl.pallas_call(
        _sb_kernel,
        grid=(batch, SB_HEADS),
        in_specs=[spec(SB_Q_BLK), spec(SB_K_BLK), spec(SB_V_BLK)],
        out_specs=spec(0),
        out_shape=jax.ShapeDtypeStruct((batch, seq, SB_HEADS * HEAD_DIM), BF16),
        scratch_shapes=[pltpu.VMEM((seq, HEAD_DIM), F32), pltpu.VMEM((seq, 1), F32)],
        compiler_params=_params(2),
        name="sb_attention",
    )(proj, proj, proj)


def _t5_bias_kernel(rel_ref, o_ref):
    t = ATT_TILE
    h = pl.program_id(0)
    row = lax.broadcasted_iota(jnp.int32, (t, t), 0)
    col = lax.broadcasted_iota(jnp.int32, (t, t), 1)
    max_exact = NUM_BUCKETS // 2
    for d in range(3):
        n = jnp.maximum(d * t + row - col, 0)
        nf = jnp.maximum(n, 1).astype(F32)
        large = max_exact + (jnp.log(nf / max_exact) / math.log(MAX_DISTANCE / max_exact)
                             * (NUM_BUCKETS - max_exact)).astype(jnp.int32)
        large = jnp.minimum(large, NUM_BUCKETS - 1)
        bucket = jnp.where(n < max_exact, n, large)
        bias = jnp.zeros((t, t), F32)
        for b in range(NUM_BUCKETS):
            bias = jnp.where(bucket == b, rel_ref[b, h], bias)
        o_ref[d] = bias * LOG2E


def _t5_bias_tiles(rel_bias):
    t = ATT_TILE
    return pl.pallas_call(
        _t5_bias_kernel,
        grid=(DIFF_HEADS,),
        in_specs=[pl.BlockSpec(memory_space=pltpu.SMEM)],
        out_specs=pl.BlockSpec((None, 3, t, t), lambda h: (h, 0, 0, 0)),
        out_shape=jax.ShapeDtypeStruct((DIFF_HEADS, 3, t, t), F32),
        compiler_params=_params(1),
        name="t5_bias_tiles",
    )(rel_bias)


def _diff_kernel(lq1_ref, lk1_ref, lq2_ref, lk2_ref, subln_g_ref, bias_ref, q_ref, k_ref, v_ref, o_ref,
                 *, lambda_init):
    t = ATT_TILE
    seq = q_ref.shape[0]
    n_tiles = seq // t
    scale2 = LOG2E / math.sqrt(DIFF_QK_DIM)
    lane = lax.broadcasted_iota(jnp.int32, (t, HEAD_DIM), 1)
    row = lax.broadcasted_iota(jnp.int32, (t, t), 0)
    col = lax.broadcasted_iota(jnp.int32, (t, t), 1)
    causal = col <= row
    lam = (jnp.exp(jnp.sum(lq1_ref[...] * lk1_ref[...], axis=1, keepdims=True))
           - jnp.exp(jnp.sum(lq2_ref[...] * lk2_ref[...], axis=1, keepdims=True)) + lambda_init)

    for i in range(n_tiles):
        n_keys = (i + 1) * t
        q = q_ref[i * t:(i + 1) * t, :]
        zero = jnp.zeros_like(q)
        q_maps = (jnp.where(lane < DIFF_QK_DIM, q, zero), jnp.where(lane >= DIFF_QK_DIM, q, zero))
        k = k_ref[0:n_keys, :]
        exps, sums = [], []
        for qm in q_maps:
            raw = lax.dot_general(qm, k, (((1,), (1,)), ((), ())), preferred_element_type=F32)
            parts = []
            for j in range(i + 1):
                s_j = raw[:, j * t:(j + 1) * t] * scale2 + bias_ref[min(i - j, 2)]
                parts.append(jnp.where(causal, s_j, NEG_INF) if j == i else s_j)
            s = parts[0] if i == 0 else jnp.concatenate(parts, axis=1)
            e = jnp.exp2(s - jnp.max(s, axis=1, keepdims=True))
            exps.append(e)
            sums.append(jnp.sum(e, axis=1, keepdims=True))
        p = exps[0] - exps[1] * (lam * sums[0] / sums[1])
        o = jnp.dot(p.astype(BF16), v_ref[0:n_keys, :], preferred_element_type=F32) * (1.0 / sums[0])
        o_ref[i * t:(i + 1) * t, :] = (_rms(o, subln_g_ref[...]) * (1.0 - lambda_init)).astype(o_ref.dtype)


def _diff_attention(proj, bias_tiles, layer, lq1, lk1, lq2, lk2, subln_g, *, batch, seq, lambda_init):
    t = ATT_TILE
    vec = lambda n: _layer_vec_spec(layer, n)
    spec = lambda blk: pl.BlockSpec((None, seq, HEAD_DIM), lambda b, h: (b, 0, blk + h))
    kern = functools.partial(_diff_kernel, lambda_init=lambda_init)
    return pl.pallas_call(
        kern,
        grid=(batch, DIFF_HEADS),
        in_specs=[
            vec(DIFF_QK_DIM), vec(DIFF_QK_DIM), vec(DIFF_QK_DIM), vec(DIFF_QK_DIM), vec(HEAD_DIM),
            pl.BlockSpec((None, 3, t, t), lambda b, h: (h, 0, 0, 0)),
            spec(DF_Q_BLK), spec(DF_K_BLK), spec(DF_V_BLK),
        ],
        out_specs=spec(0),
        out_shape=jax.ShapeDtypeStruct((batch, seq, DIFF_HEADS * HEAD_DIM), BF16),
        compiler_params=_params(2),
        name="diff_attention",
    )(_as_rows(lq1), _as_rows(lk1), _as_rows(lq2), _as_rows(lk2), _as_rows(subln_g), bias_tiles, proj, proj, proj)


def _cpow(n, ar, ai, dt):
    mag = jnp.exp(n * (ar * dt))
    ang = n * (ai * dt)
    return mag * jnp.cos(ang), mag * jnp.sin(ang)


def _repeat_rows(x, reps):
    n, w = x.shape
    return jnp.broadcast_to(x[:, None, :], (n, reps, w)).reshape(n * reps, w)


def _ssm_prep_kernel(ar_ref, ai_ref, ldt_ref, bt_re_ref, bt_im_ref, ct_re_ref, ct_im_ref,
                     m_intra_ref, m_in_ref, m_out_ref, p1_ref, p2_ref):
    t = SSM_CHUNK
    cpg = SSM_CH_PER_GROUP
    ar, ai = ar_ref[...], ai_ref[...]
    dt = jnp.exp(ldt_ref[...])
    lb_re, lb_im = _cpow(1.0, ar, ai, dt)
    den = ar * ar + ai * ai
    f_re = ((lb_re - 1.0) * ar + lb_im * ai) / den
    f_im = (lb_im * ar - (lb_re - 1.0) * ai) / den
    bt_re, bt_im = bt_re_ref[...], bt_im_ref[...]
    bb_re = f_re * bt_re - f_im * bt_im
    bb_im = f_re * bt_im + f_im * bt_re
    ct_re, ct_im = ct_re_ref[...], ct_im_ref[...]
    pos = lax.broadcasted_iota(jnp.int32, (t, 2 * SSM_STATE), 0).astype(F32)
    lane_is_re = lax.broadcasted_iota(jnp.int32, (t * cpg, 2 * SSM_STATE), 1) < SSM_STATE

    def times_power(n, x_re, x_im, im_sign):
        e_re, e_im = _cpow(n, ar, ai, dt)
        e_re, e_im = _repeat_rows(e_re, cpg), _repeat_rows(e_im, cpg)
        return jnp.where(lane_is_re, e_re * x_re - e_im * x_im, im_sign * (e_re * x_im + e_im * x_re))

    m_in_ref[...] = times_power((t - 1.0) - pos, bb_re, bb_im, 1.0).astype(BF16)
    m_out_ref[...] = times_power(pos + 1.0, ct_re, ct_im, -1.0).astype(BF16)
    p_packed = times_power(-pos, bb_re, bb_im, -1.0)
    q_packed = times_power(pos, ct_re, ct_im, 1.0)
    k_full = lax.dot_general(p_packed, q_packed, (((1,), (1,)), ((), ())), preferred_element_type=F32,
                             precision=lax.Precision.HIGHEST)
    shift = cpg.bit_length() - 1
    row_k = lax.broadcasted_iota(jnp.int32, k_full.shape, 0)
    lane_k = lax.broadcasted_iota(jnp.int32, k_full.shape, 1)
    causal = jnp.right_shift(lane_k, shift) >= jnp.right_shift(row_k, shift)
    m_intra_ref[...] = jnp.where(causal, k_full, 0.0).astype(BF16)

    row_j = lax.broadcasted_iota(jnp.int32, (SSM_SCAN_ROWS, 2 * SSM_STATE), 0)
    lane_j = lax.broadcasted_iota(jnp.int32, (SSM_SCAN_ROWS, 2 * SSM_STATE), 1)
    pr, pi = _cpow(jnp.left_shift(t, row_j).astype(F32), ar, ai, dt)
    p1_ref[...] = pr
    p2_ref[...] = jnp.where(lane_j < SSM_STATE, -pi, pi)


def _ssm_prep(a_re, a_im, log_dt, b_re, b_im, c_re, c_im):
    nl, g, p = a_re.shape
    t, cpg = SSM_CHUNK, SSM_CH_PER_GROUP
    tc = t * cpg
    twice = lambda x: jnp.concatenate([x, x], axis=-1)
    rows = [twice(x)[:, :, None, :] for x in (a_re, a_im, jnp.broadcast_to(log_dt[:, :, None], a_re.shape))]
    bt = [jnp.tile(twice(jnp.swapaxes(b, 2, 3)), (1, 1, t, 1)) for b in (b_re, b_im)]
    ct = [jnp.tile(twice(c), (1, 1, t, 1)) for c in (c_re, c_im)]
    blk = lambda shape: pl.BlockSpec((None, None) + shape, lambda l, gi: (l, gi, 0, 0))
    return pl.pallas_call(
        _ssm_prep_kernel,
        grid=(nl, g),
        in_specs=[blk((1, 2 * p))] * 3 + [blk((tc, 2 * p))] * 4,
        out_specs=[blk((tc, tc)), blk((tc, 2 * p)), blk((tc, 2 * p)),
                   blk((SSM_SCAN_ROWS, 2 * p)), blk((SSM_SCAN_ROWS, 2 * p))],
        out_shape=[jax.ShapeDtypeStruct((nl, g, tc, tc), BF16),
                   jax.ShapeDtypeStruct((nl, g, tc, 2 * p), BF16),
                   jax.ShapeDtypeStruct((nl, g, tc, 2 * p), BF16),
                   jax.ShapeDtypeStruct((nl, g, SSM_SCAN_ROWS, 2 * p), F32),
                   jax.ShapeDtypeStruct((nl, g, SSM_SCAN_ROWS, 2 * p), F32)],
        compiler_params=_params(2),
        name="ssm_prep",
    )(*rows, *bt, *ct)


def _gelu_tanh(x):
    return 0.5 * x * (1.0 + jnp.tanh(math.sqrt(2.0 / math.pi) * (x + 0.044715 * (x * x * x))))


def _ssm_kernel(u_ref, m_intra_ref, m_in_ref, m_out_ref, p1_ref, p2_ref, d_ref, o_ref, *, n_chunks):
    t, cpg = SSM_CHUNK, SSM_CH_PER_GROUP
    n_rows = u_ref.shape[0] // t
    groups = u_ref.shape[1] // cpg
    by_pos = [u_ref[pl.ds(tau, n_rows, stride=t), :] for tau in range(t)]
    row = lax.broadcasted_iota(jnp.int32, (n_rows, 2 * SSM_STATE), 0)
    chunk = jnp.bitwise_and(row, n_chunks - 1)

    def chunks_back(x, n):
        return jnp.where(chunk >= n, pltpu.roll(x, n, axis=0), 0.0)

    ys = []
    for g in range(groups):
        lanes = slice(g * cpg, (g + 1) * cpg)
        u = jnp.concatenate([x[:, lanes] for x in by_pos], axis=1)
        ub = u.astype(BF16)
        y = jnp.dot(ub, m_intra_ref[g], preferred_element_type=F32)
        s_in = jnp.dot(ub, m_in_ref[g], preferred_element_type=F32)
        x = chunks_back(s_in, 1)
        for j in range((n_chunks - 1).bit_length()):
            prev = chunks_back(x, 1 << j)
            x = x + p1_ref[g, j:j + 1, :] * prev + p2_ref[g, j:j + 1, :] * pltpu.roll(prev, SSM_STATE, axis=1)
        y_state = lax.dot_general(x.astype(BF16), m_out_ref[g], (((1,), (1,)), ((), ())),
                                  preferred_element_type=F32)
        ys.append(_gelu_tanh(y + y_state + d_ref[g] * u))
    for tau in range(t):
        lanes = slice(tau * cpg, (tau + 1) * cpg)
        o_ref[pl.ds(tau, n_rows, stride=t), :] = jnp.concatenate([y[:, lanes] for y in ys], axis=1)


def _ssm(u, layer, mats, d_skip, *, n_chunks):
    m, width = u.shape
    m_intra, m_in, m_out, p1, p2 = mats
    gps = 128 // SSM_CH_PER_GROUP
    lg = lambda x: pl.BlockSpec((None, gps) + x.shape[2:], lambda gi: (layer, gi, 0, 0))
    d_tiled = jnp.tile(d_skip, (1, 1, SSM_CHUNK))[:, :, None, :]
    kern = functools.partial(_ssm_kernel, n_chunks=n_chunks)
    return pl.pallas_call(
        kern,
        grid=(width // 128,),
        in_specs=[pl.BlockSpec((m, 128), lambda gi: (0, gi)),
                  lg(m_intra), lg(m_in), lg(m_out), lg(p1), lg(p2), lg(d_tiled)],
        out_specs=pl.BlockSpec((m, 128), lambda gi: (0, gi)),
        out_shape=jax.ShapeDtypeStruct(u.shape, F32),
        compiler_params=_params(1),
        name="ssm_chunks",
    )(u, m_intra, m_in, m_out, p1, p2, d_tiled)


def _cast_kernel(x_ref, o_ref):
    o_ref[...] = x_ref[...].astype(o_ref.dtype)


def _cast_bf16(w, *, rows):
    nl, r, c = w.shape
    return pl.pallas_call(
        _cast_kernel,
        grid=(nl, r // rows),
        in_specs=[pl.BlockSpec((None, rows, c), lambda l, i: (l, i, 0))],
        out_specs=pl.BlockSpec((None, rows, c), lambda l, i: (l, i, 0)),
        out_shape=jax.ShapeDtypeStruct(w.shape, BF16),
        compiler_params=_params(2),
        name="cast_bf16",
    )(w)


def _outproj_kernel(h_ref, sb_ref, ssm_ref, df_ref, wglu_ref, bglu_ref, w_ref, g_ref, o_ref, mixed_ref):
    w_sb, w_ssm = sb_ref.shape[1], ssm_ref.shape[1]
    y = ssm_ref[...]
    gate = jax.nn.sigmoid(jnp.dot(y.astype(BF16), wglu_ref[...].astype(BF16), preferred_element_type=F32)
                          + bglu_ref[...])
    mixed_ref[:, 0:w_sb] = sb_ref[...]
    mixed_ref[:, w_sb:w_sb + w_ssm] = (y * gate).astype(BF16)
    mixed_ref[:, w_sb + w_ssm:] = df_ref[...]
    out = jnp.dot(mixed_ref[...], w_ref[...], preferred_element_type=F32)
    o_ref[...] = h_ref[...] + _rms(out, g_ref[...])


def _outproj(h, o_sb, y_ssm, o_diff, layer, w_glu, b_glu, w_out_bf, g, *, tm):
    m, d = h.shape
    k = w_out_bf.shape[1]
    row_block = lambda x: pl.BlockSpec((tm, x.shape[1]), lambda i: (i, 0))
    return pl.pallas_call(
        _outproj_kernel,
        grid=(m // tm,),
        in_specs=[
            row_block(h), row_block(o_sb), row_block(y_ssm), row_block(o_diff),
            pl.BlockSpec((None,) + w_glu.shape[1:], lambda i: (layer, 0, 0)),
            _layer_vec_spec(layer, SSM_WIDTH),
            pl.BlockSpec((None, k, d), lambda i: (layer, 0, 0), pipeline_mode=pl.Buffered(1)),
            _layer_vec_spec(layer, d),
        ],
        out_specs=row_block(h),
        out_shape=jax.ShapeDtypeStruct((m, d), F32),
        scratch_shapes=[pltpu.VMEM((tm, k), BF16)],
        compiler_params=_params(1),
        name="outproj",
    )(h, o_sb, y_ssm, o_diff, w_glu, _as_rows(b_glu), w_out_bf, _as_rows(g))


def _trunk(x, ffn1_pre_g, ffn1_w_gate, ffn1_w_up, ffn1_w_down, ffn1_post_g, mix_pre_g, w_in, ssm_a_re, ssm_a_im,
           ssm_log_dt, ssm_b_re, ssm_b_im, ssm_c_re, ssm_c_im, ssm_d, ssm_w_glu, ssm_b_glu, diff_lq1, diff_lk1,
           diff_lq2, diff_lk2, diff_subln_g, rel_bias, w_out, mix_post_g, ffn2_pre_g, ffn2_w_gate, ffn2_w_up,
           ffn2_w_down, ffn2_post_g, *, tm, tf, tm_out):
    batch, seq, d = x.shape
    m = batch * seq
    depth = w_in.shape[0]
    h = x.reshape(m, d)
    bias_tiles = _t5_bias_tiles(rel_bias)
    ssm_mats = _ssm_prep(ssm_a_re, ssm_a_im, ssm_log_dt, ssm_b_re, ssm_b_im, ssm_c_re, ssm_c_im)
    w_out_bf = _cast_bf16(w_out, rows=512)
    for l in range(depth):
        h = _ffn(h, l, ffn1_pre_g, ffn1_w_gate, ffn1_w_up, ffn1_w_down, ffn1_post_g, tm=tm, tf=tf)

        proj, ssm_u = _inproj(h, l, mix_pre_g, w_in, tm=tm)
        proj = proj.reshape(batch, seq, PROJ_COLS)

        o_sb = _sb_attention(proj, batch=batch, seq=seq)

        y_ssm = _ssm(ssm_u, l, ssm_mats, ssm_d, n_chunks=seq // SSM_CHUNK)

        lambda_init = 0.8 - 0.6 * math.exp(-0.3 * l)
        o_diff = _diff_attention(proj, bias_tiles, l, diff_lq1, diff_lk1, diff_lq2, diff_lk2, diff_subln_g,
                                 batch=batch, seq=seq, lambda_init=lambda_init)

        h = _outproj(h, o_sb.reshape(m, -1), y_ssm, o_diff.reshape(m, -1), l, ssm_w_glu, ssm_b_glu, w_out_bf,
                     mix_post_g, tm=tm_out)

        h = _ffn(h, l, ffn2_pre_g, ffn2_w_gate, ffn2_w_up, ffn2_w_down, ffn2_post_g, tm=tm, tf=tf)
    return h.reshape(batch, seq, d)


def kernel(x, ffn1_pre_g, ffn1_w_gate, ffn1_w_up, ffn1_w_down, ffn1_post_g, mix_pre_g, w_in, ssm_a_re, ssm_a_im, ssm_log_dt, ssm_b_re, ssm_b_im, ssm_c_re, ssm_c_im, ssm_d, ssm_w_glu, ssm_b_glu, diff_lq1, diff_lk1, diff_lq2, diff_lk2, diff_subln_g, rel_bias, w_out, mix_post_g, ffn2_pre_g, ffn2_w_gate, ffn2_w_up, ffn2_w_down, ffn2_post_g):
    n_chunks = x.shape[1] // SSM_CHUNK
    assert x.shape[1] % ATT_TILE == 0 and n_chunks <= 1 << SSM_SCAN_ROWS and n_chunks & (n_chunks - 1) == 0
    return _trunk(x, ffn1_pre_g, ffn1_w_gate, ffn1_w_up, ffn1_w_down, ffn1_post_g, mix_pre_g, w_in, ssm_a_re,
                  ssm_a_im, ssm_log_dt, ssm_b_re, ssm_b_im, ssm_c_re, ssm_c_im, ssm_d, ssm_w_glu, ssm_b_glu,
                  diff_lq1, diff_lk1, diff_lq2, diff_lk2, diff_subln_g, rel_bias, w_out, mix_post_g, ffn2_pre_g,
                  ffn2_w_gate, ffn2_w_up, ffn2_w_down, ffn2_post_g, tm=1024, tf=512, tm_out=512)
```

```python
import functools
import math

import jax
import jax.numpy as jnp
from jax import lax
from jax.experimental import pallas as pl
from jax.experimental.pallas import tpu as pltpu

F32 = jnp.float32
BF16 = jnp.bfloat16

HEAD_DIM = 128
SB_HEADS = 6
DIFF_HEADS = 6
DIFF_QK_DIM = HEAD_DIM // 2
SSM_GROUPS = 32
SSM_CH_PER_GROUP = 16
SSM_STATE = 64
SSM_WIDTH = SSM_GROUPS * SSM_CH_PER_GROUP
NUM_BUCKETS = 32
MAX_DISTANCE = 128
NORM_EPS = 1e-6
FFN_RESIDUAL_WEIGHT = 0.5
LOG2E = 1.0 / math.log(2.0)

SB_Q_BLK, SB_K_BLK, SB_V_BLK = 0, 6, 12
DF_Q_BLK, DF_K_BLK, DF_V_BLK = 18, 24, 30
PROJ_COLS = 36 * HEAD_DIM
SSM_COL0 = 3 * SB_HEADS * HEAD_DIM
W_IN_SUB = 256
PROJ_STEP_COLS = 3 * W_IN_SUB
SB_Q_STEP, DF_Q_STEP = SB_Q_BLK // 6, DF_Q_BLK // 6
SB_LOGIT_SCALE2 = LOG2E / math.sqrt(HEAD_DIM)
DF_LOGIT_SCALE2 = LOG2E / math.sqrt(DIFF_QK_DIM)

V7X_VMEM_LIMIT = 60 * 1024 * 1024
FFN_SUB = 256
ROW_CHUNK = 128
ATT_TILE = 256
SB_NEAR_TILES = 2
SB_SATURATED = 151.0
SSM_CHUNK = 16
SSM_PREP_GROUPS = 8
SSM_SCAN_ROWS = 8
NEG_INF = float(jnp.finfo(jnp.float32).min)


def _params(n_grid, vmem=V7X_VMEM_LIMIT):
    return pltpu.CompilerParams(dimension_semantics=("arbitrary",) * n_grid, vmem_limit_bytes=vmem)


def _rms(x, g):
    ms = jnp.mean(x * x, axis=-1, keepdims=True)
    return x * lax.rsqrt(ms + NORM_EPS) * g


def _for_row_chunks(n_rows, chunk, body):
    def step(i, carry):
        body(pl.multiple_of(i * chunk, chunk))
        return carry
    lax.fori_loop(0, n_rows // chunk, step, 0)


def _layer_vec_spec(layer, n):
    return pl.BlockSpec((None, 1, n), lambda *_: (layer, 0, 0))


def _as_rows(p):
    return p.reshape(p.shape[0], 1, p.shape[1])


def _ffn_kernel(x_ref, pre_g_ref, wg_ref, wu_ref, wd_ref, post_g_ref, o_ref, xn_ref, a_ref, *, n_f, n_col_chunks):
    f = pl.program_id(1)
    tm, d = x_ref.shape

    @pl.when(f == 0)
    def _():
        def norm_rows(r0):
            rows = pl.ds(r0, ROW_CHUNK)
            xn_ref[rows, :] = _rms(x_ref[rows, :], pre_g_ref[...]).astype(BF16)
            o_ref[rows, :] = jnp.zeros((ROW_CHUNK, d), F32)
        _for_row_chunks(tm, ROW_CHUNK, norm_rows)

    xn = xn_ref[...]
    tf = wg_ref.shape[1]
    for hf in range(tf // FFN_SUB):
        sub = slice(hf * FFN_SUB, (hf + 1) * FFN_SUB)
        g = jnp.dot(xn, wg_ref[:, sub].astype(BF16), preferred_element_type=F32)
        u = jnp.dot(xn, wu_ref[:, sub].astype(BF16), preferred_element_type=F32)
        a_ref[:, sub] = (g * jax.nn.sigmoid(g) * u).astype(BF16)
    a = a_ref[...]
    cw = d // n_col_chunks
    for c in range(n_col_chunks):
        cols = slice(c * cw, (c + 1) * cw)
        o_ref[:, cols] += jnp.dot(a, wd_ref[:, cols].astype(BF16), preferred_element_type=F32)

    @pl.when(f == n_f - 1)
    def _():
        def finish_rows(r0):
            rows = pl.ds(r0, ROW_CHUNK)
            o_ref[rows, :] = x_ref[rows, :] + FFN_RESIDUAL_WEIGHT * _rms(o_ref[rows, :], post_g_ref[...])
        _for_row_chunks(tm, ROW_CHUNK, finish_rows)


def _ffn(h, layer, pre_g, w_gate, w_up, w_down, post_g, *, tm, tf):
    m, d = h.shape
    d_ff = w_gate.shape[2]
    n_f = d_ff // tf
    kern = functools.partial(_ffn_kernel, n_f=n_f, n_col_chunks=4)
    return pl.pallas_call(
        kern,
        grid=(m // tm, n_f),
        in_specs=[
            pl.BlockSpec((tm, d), lambda i, f: (i, 0)),
            _layer_vec_spec(layer, d),
            pl.BlockSpec((None, d, tf), lambda i, f: (layer, 0, f)),
            pl.BlockSpec((None, d, tf), lambda i, f: (layer, 0, f)),
            pl.BlockSpec((None, tf, d), lambda i, f: (layer, f, 0)),
            _layer_vec_spec(layer, d),
        ],
        out_specs=pl.BlockSpec((tm, d), lambda i, f: (i, 0), pipeline_mode=pl.Buffered(1)),
        out_shape=jax.ShapeDtypeStruct((m, d), F32),
        scratch_shapes=[pltpu.VMEM((tm, d), BF16), pltpu.VMEM((tm, tf), BF16)],
        compiler_params=_params(2),
        name="ffn",
    )(h, _as_rows(pre_g), w_gate, w_up, w_down, _as_rows(post_g))


def _inproj_kernel(x_ref, g_ref, w0_ref, w1_ref, w2_ref, proj_ref, u_ref, xn_ref, *, n_head_steps):
    j = pl.program_id(1)
    tm = x_ref.shape[0]
    sub = W_IN_SUB

    @pl.when(j == 0)
    def _():
        def norm_rows(r0):
            rows = pl.ds(r0, ROW_CHUNK)
            xn_ref[rows, :] = _rms(x_ref[rows, :], g_ref[...]).astype(BF16)
        _for_row_chunks(tm, ROW_CHUNK, norm_rows)

    @pl.when(j < n_head_steps)
    def _():
        xn = xn_ref[...]
        scale = jnp.where(j == SB_Q_STEP, SB_LOGIT_SCALE2, jnp.where(j == DF_Q_STEP, DF_LOGIT_SCALE2, 1.0))
        for c, w_ref in enumerate((w0_ref, w1_ref, w2_ref)):
            y = jnp.dot(xn, w_ref[...].astype(BF16), preferred_element_type=F32)
            proj_ref[:, c * sub:(c + 1) * sub] = (y * scale).astype(proj_ref.dtype)

    @pl.when(j == n_head_steps)
    def _():
        xn = xn_ref[...]
        for c, w_ref in enumerate((w0_ref, w1_ref)):
            u_ref[:, c * sub:(c + 1) * sub] = jnp.dot(xn, w_ref[...].astype(BF16), preferred_element_type=F32)


def _inproj(h, layer, g, w_in, *, tm):
    m, d = h.shape
    n_head_steps = PROJ_COLS // PROJ_STEP_COLS
    ssm_blk0 = SSM_COL0 // W_IN_SUB
    n_ssm_blks = SSM_WIDTH // W_IN_SUB
    last_blk = w_in.shape[2] // W_IN_SUB - 1

    def w_spec(c):
        def index(i, j):
            head_blk = 3 * j + c + jnp.where(3 * j + c >= ssm_blk0, n_ssm_blks, 0)
            blk = jnp.where(j < n_head_steps, head_blk, ssm_blk0 + c)
            return (layer, 0, jnp.minimum(blk, last_blk))
        return pl.BlockSpec((None, d, W_IN_SUB), index)

    kern = functools.partial(_inproj_kernel, n_head_steps=n_head_steps)
    return pl.pallas_call(
        kern,
        grid=(m // tm, n_head_steps + 1),
        in_specs=[
            pl.BlockSpec((tm, d), lambda i, j: (i, 0)),
            _layer_vec_spec(layer, d),
            w_spec(0), w_spec(1), w_spec(2),
        ],
        out_specs=[
            pl.BlockSpec((tm, PROJ_STEP_COLS), lambda i, j: (i, jnp.minimum(j, n_head_steps - 1))),
            pl.BlockSpec((tm, SSM_WIDTH), lambda i, j: (i, 0)),
        ],
        out_shape=[jax.ShapeDtypeStruct((m, PROJ_COLS), BF16), jax.ShapeDtypeStruct((m, SSM_WIDTH), F32)],
        scratch_shapes=[pltpu.VMEM((tm, d), BF16)],
        compiler_params=_params(2),
        name="inproj",
    )(h, _as_rows(g), w_in, w_in, w_in)


def _split2_bf16(x):
    hi = x.astype(BF16)
    lo = (x - hi.astype(F32)).astype(BF16)
    return hi, lo


def _mask_first_tile(x, keep):
    t = ATT_TILE
    head = jnp.where(keep, x[0:t], 0.0)
    return head if x.shape[0] == t else jnp.concatenate([head, x[t:]], axis=0)


def _sb_tile(q, k, v, later_rows, suffix_ones, first_tile_mask):
    z = lax.dot_general(q, k, (((1,), (1,)), ((), ())), preferred_element_type=F32)
    pos = jnp.maximum(z, 0.0)
    neg = jnp.minimum(z, 0.0)
    l2 = jnp.log2(1.0 + jnp.exp2(neg - pos))
    sp = pos + l2
    log_beta = neg - l2
    sp_in = sp if first_tile_mask is None else _mask_first_tile(sp, first_tile_mask)
    hi, lo = _split2_bf16(sp_in)
    suffix = jnp.dot(jnp.concatenate([hi, lo], axis=1), suffix_ones, preferred_element_type=F32)
    w = jnp.exp2(log_beta - suffix - later_rows)
    if first_tile_mask is not None:
        w = _mask_first_tile(w, first_tile_mask)
    pv = jnp.dot(w.astype(BF16), v, preferred_element_type=F32)
    return pv, jnp.sum(sp_in, axis=1, keepdims=True)


def _sb_kernel(q_ref, k_ref, v_ref, o_ref, acc_ref, later_ref):
    t = ATT_TILE
    seq = q_ref.shape[0]
    n_tiles = seq // t
    row = lax.broadcasted_iota(jnp.int32, (t, t), 0)
    col = lax.broadcasted_iota(jnp.int32, (t, t), 1)
    strictly_lower = row > col
    ones = jnp.where(strictly_lower, 1.0, 0.0).astype(BF16)
    suffix_ones = jnp.concatenate([ones, ones], axis=0)
    acc = [None] * n_tiles
    later = [None] * n_tiles

    for j in range(n_tiles - 1, -1, -1):
        r0 = j * t
        r1 = min(r0 + SB_NEAR_TILES * t, seq)
        later_rows = jnp.concatenate([jnp.zeros((t, 1), F32)] + later[j + 1:j + SB_NEAR_TILES], axis=0)
        pv, row_sum = _sb_tile(q_ref[r0:r1, :], k_ref[r0:r0 + t, :], v_ref[r0:r0 + t, :], later_rows, suffix_ones,
                               strictly_lower)
        for i in range(j, r1 // t):
            rows = slice((i - j) * t, (i - j + 1) * t)
            acc[i] = pv[rows] if i == j else acc[i] + pv[rows]
            later[i] = row_sum[rows] if i == j else later[i] + row_sum[rows]

    for i in range(n_tiles):
        acc_ref[i * t:(i + 1) * t, :] = acc[i]
        later_ref[i * t:(i + 1) * t, :] = later[i]

    def far_tiles_of(i, carry):
        rows = pl.ds(pl.multiple_of(i * t, t), t)
        n_far = i + 1 - SB_NEAR_TILES

        def far_tile(n, inner):
            k0 = pl.multiple_of((n_far - 1 - n) * t, t)
            pv, row_sum = _sb_tile(q_ref[rows, :], k_ref[pl.ds(k0, t), :], v_ref[pl.ds(k0, t), :], later_ref[rows, :],
                                   suffix_ones, None)
            acc_ref[rows, :] += pv
            later_ref[rows, :] += row_sum
            return inner

        lax.fori_loop(0, jnp.where(jnp.min(later_ref[rows, :]) < SB_SATURATED, n_far, 0), far_tile, 0)
        return carry

    lax.fori_loop(SB_NEAR_TILES, n_tiles, far_tiles_of, 0)
    o_ref[...] = acc_ref[...].astype(o_ref.dtype)


def _sb_attention(proj, *, batch, seq):
    spec = lambda blk: pl.BlockSpec((None, seq, HEAD_DIM), lambda b, h: (b, 0, blk + h))
    return pl.pallas_call(
        _sb_kernel,
        grid=(batch, SB_HEADS),
        in_specs=[spec(SB_Q_BLK), spec(SB_K_BLK), spec(SB_V_BLK)],
        out_specs=spec(0),
        out_shape=jax.ShapeDtypeStruct((batch, seq, SB_HEADS * HEAD_DIM), BF16),
        scratch_shapes=[pltpu.VMEM((seq, HEAD_DIM), F32), pltpu.VMEM((seq, 1), F32)],
        compiler_params=_params(2),
        name="sb_attention",
    )(proj, proj, proj)


def _t5_bias_kernel(rel_ref, o_ref):
    t = ATT_TILE
    h = pl.program_id(0)
    row = lax.broadcasted_iota(jnp.int32, (t, t), 0)
    col = lax.broadcasted_iota(jnp.int32, (t, t), 1)
    max_exact = NUM_BUCKETS // 2
    for d in range(3):
        n = jnp.maximum(d * t + row - col, 0)
        nf = jnp.maximum(n, 1).astype(F32)
        large = max_exact + (jnp.log(nf / max_exact) / math.log(MAX_DISTANCE / max_exact)
                             * (NUM_BUCKETS - max_exact)).astype(jnp.int32)
        large = jnp.minimum(large, NUM_BUCKETS - 1)
        bucket = jnp.where(n < max_exact, n, large)
        bias = jnp.zeros((t, t), F32)
        for b in range(NUM_BUCKETS):
            bias = jnp.where(bucket == b, rel_ref[b, h], bias)
        o_ref[d] = bias * LOG2E


def _t5_bias_tiles(rel_bias):
    t = ATT_TILE
    return pl.pallas_call(
        _t5_bias_kernel,
        grid=(DIFF_HEADS,),
        in_specs=[pl.BlockSpec(memory_space=pltpu.SMEM)],
        out_specs=pl.BlockSpec((None, 3, t, t), lambda h: (h, 0, 0, 0)),
        out_shape=jax.ShapeDtypeStruct((DIFF_HEADS, 3, t, t), F32),
        compiler_params=_params(1),
        name="t5_bias_tiles",
    )(rel_bias)


def _diff_kernel(lq1_ref, lk1_ref, lq2_ref, lk2_ref, subln_g_ref, bias_ref, q_ref, k_ref, v_ref, o_ref,
                 *, lambda_init):
    t = ATT_TILE
    seq = q_ref.shape[0]
    n_tiles = seq // t
    lane = lax.broadcasted_iota(jnp.int32, (t, HEAD_DIM), 1)
    row = lax.broadcasted_iota(jnp.int32, (t, t), 0)
    col = lax.broadcasted_iota(jnp.int32, (t, t), 1)
    causal = col <= row
    lam = (jnp.exp(jnp.sum(lq1_ref[...] * lk1_ref[...], axis=1, keepdims=True))
           - jnp.exp(jnp.sum(lq2_ref[...] * lk2_ref[...], axis=1, keepdims=True)) + lambda_init)

    for i in range(n_tiles):
        n_keys = (i + 1) * t
        q = q_ref[i * t:(i + 1) * t, :]
        zero = jnp.zeros_like(q)
        q_maps = (jnp.where(lane < DIFF_QK_DIM, q, zero), jnp.where(lane >= DIFF_QK_DIM, q, zero))
        k = k_ref[0:n_keys, :]
        exps, sums = [], []
        for qm in q_maps:
            raw = lax.dot_general(qm, k, (((1,), (1,)), ((), ())), preferred_element_type=F32)
            parts = []
            for j in range(i + 1):
                s_j = raw[:, j * t:(j + 1) * t] + bias_ref[min(i - j, 2)]
                parts.append(jnp.where(causal, s_j, NEG_INF) if j == i else s_j)
            s = parts[0] if i == 0 else jnp.concatenate(parts, axis=1)
            e = jnp.exp2(s - jnp.max(s, axis=1, keepdims=True))
            exps.append(e)
            sums.append(jnp.sum(e, axis=1, keepdims=True))
        p = exps[0] - exps[1] * (lam * sums[0] / sums[1])
        o = jnp.dot(p.astype(BF16), v_ref[0:n_keys, :], preferred_element_type=F32) * (1.0 / sums[0])
        o_ref[i * t:(i + 1) * t, :] = (_rms(o, subln_g_ref[...]) * (1.0 - lambda_init)).astype(o_ref.dtype)


def _diff_attention(proj, bias_tiles, layer, lq1, lk1, lq2, lk2, subln_g, *, batch, seq, lambda_init):
    t = ATT_TILE
    vec = lambda n: _layer_vec_spec(layer, n)
    spec = lambda blk: pl.BlockSpec((None, seq, HEAD_DIM), lambda b, h: (b, 0, blk + h))
    kern = functools.partial(_diff_kernel, lambda_init=lambda_init)
    return pl.pallas_call(
        kern,
        grid=(batch, DIFF_HEADS),
        in_specs=[
            vec(DIFF_QK_DIM), vec(DIFF_QK_DIM), vec(DIFF_QK_DIM), vec(DIFF_QK_DIM), vec(HEAD_DIM),
            pl.BlockSpec((None, 3, t, t), lambda b, h: (h, 0, 0, 0)),
            spec(DF_Q_BLK), spec(DF_K_BLK), spec(DF_V_BLK),
        ],
        out_specs=spec(0),
        out_shape=jax.ShapeDtypeStruct((batch, seq, DIFF_HEADS * HEAD_DIM), BF16),
        compiler_params=_params(2),
        name="diff_attention",
    )(_as_rows(lq1), _as_rows(lk1), _as_rows(lq2), _as_rows(lk2), _as_rows(subln_g), bias_tiles, proj, proj, proj)


def _cpow(n, ar, ai, dt):
    mag = jnp.exp(n * (ar * dt))
    ang = n * (ai * dt)
    return mag * jnp.cos(ang), mag * jnp.sin(ang)


def _repeat_rows(x, reps):
    n, w = x.shape
    return jnp.broadcast_to(x[:, None, :], (n, reps, w)).reshape(n * reps, w)


def _ssm_prep_group(ar_ref, ai_ref, ldt_ref, bt_re_ref, bt_im_ref, ct_re_ref, ct_im_ref,
                    m_intra_ref, m_in_ref, m_out_ref, p1_ref, p2_ref):
    t = SSM_CHUNK
    cpg = SSM_CH_PER_GROUP
    ar, ai = ar_ref[...], ai_ref[...]
    dt = jnp.exp(ldt_ref[...])
    lb_re, lb_im = _cpow(1.0, ar, ai, dt)
    den = ar * ar + ai * ai
    f_re = ((lb_re - 1.0) * ar + lb_im * ai) / den
    f_im = (lb_im * ar - (lb_re - 1.0) * ai) / den
    bt_re, bt_im = bt_re_ref[...], bt_im_ref[...]
    bb_re = f_re * bt_re - f_im * bt_im
    bb_im = f_re * bt_im + f_im * bt_re
    ct_re, ct_im = ct_re_ref[...], ct_im_ref[...]
    pos = lax.broadcasted_iota(jnp.int32, (t, 2 * SSM_STATE), 0).astype(F32)
    lane_is_re = lax.broadcasted_iota(jnp.int32, (t * cpg, 2 * SSM_STATE), 1) < SSM_STATE

    def times_power(n, x_re, x_im, im_sign):
        e_re, e_im = _cpow(n, ar, ai, dt)
        e_re, e_im = _repeat_rows(e_re, cpg), _repeat_rows(e_im, cpg)
        return jnp.where(lane_is_re, e_re * x_re - e_im * x_im, im_sign * (e_re * x_im + e_im * x_re))

    m_in_ref[...] = times_power((t - 1.0) - pos, bb_re, bb_im, 1.0).astype(BF16)
    m_out_ref[...] = times_power(pos + 1.0, ct_re, ct_im, -1.0).astype(BF16)
    p_packed = times_power(-pos, bb_re, bb_im, -1.0)
    q_packed = times_power(pos, ct_re, ct_im, 1.0)
    k_full = lax.dot_general(p_packed, q_packed, (((1,), (1,)), ((), ())), preferred_element_type=F32,
                             precision=lax.Precision.HIGHEST)
    shift = cpg.bit_length() - 1
    row_k = lax.broadcasted_iota(jnp.int32, k_full.shape, 0)
    lane_k = lax.broadcasted_iota(jnp.int32, k_full.shape, 1)
    causal = jnp.right_shift(lane_k, shift) >= jnp.right_shift(row_k, shift)
    m_intra_ref[...] = jnp.where(causal, k_full, 0.0).astype(BF16)

    row_j = lax.broadcasted_iota(jnp.int32, (SSM_SCAN_ROWS, 2 * SSM_STATE), 0)
    lane_j = lax.broadcasted_iota(jnp.int32, (SSM_SCAN_ROWS, 2 * SSM_STATE), 1)
    pr, pi = _cpow(jnp.left_shift(t, row_j).astype(F32), ar, ai, dt)
    p1_ref[...] = pr
    p2_ref[...] = jnp.where(lane_j < SSM_STATE, -pi, pi)


def _ssm_prep_kernel(*refs):
    for g in range(refs[0].shape[0]):
        _ssm_prep_group(*[r.at[g] for r in refs])


def _ssm_prep(a_re, a_im, log_dt, b_re, b_im, c_re, c_im):
    nl, g, p = a_re.shape
    t, cpg = SSM_CHUNK, SSM_CH_PER_GROUP
    tc = t * cpg
    twice = lambda x: jnp.concatenate([x, x], axis=-1)
    rows = [twice(x)[:, :, None, :] for x in (a_re, a_im, jnp.broadcast_to(log_dt[:, :, None], a_re.shape))]
    bt = [jnp.tile(twice(jnp.swapaxes(b, 2, 3)), (1, 1, t, 1)) for b in (b_re, b_im)]
    ct = [jnp.tile(twice(c), (1, 1, t, 1)) for c in (c_re, c_im)]
    gps = SSM_PREP_GROUPS
    blk = lambda shape: pl.BlockSpec((None, gps) + shape, lambda l, gi: (l, gi, 0, 0))
    return pl.pallas_call(
        _ssm_prep_kernel,
        grid=(nl, g // gps),
        in_specs=[blk((1, 2 * p))] * 3 + [blk((tc, 2 * p))] * 4,
        out_specs=[blk((tc, tc)), blk((tc, 2 * p)), blk((tc, 2 * p)),
                   blk((SSM_SCAN_ROWS, 2 * p)), blk((SSM_SCAN_ROWS, 2 * p))],
        out_shape=[jax.ShapeDtypeStruct((nl, g, tc, tc), BF16),
                   jax.ShapeDtypeStruct((nl, g, tc, 2 * p), BF16),
                   jax.ShapeDtypeStruct((nl, g, tc, 2 * p), BF16),
                   jax.ShapeDtypeStruct((nl, g, SSM_SCAN_ROWS, 2 * p), F32),
                   jax.ShapeDtypeStruct((nl, g, SSM_SCAN_ROWS, 2 * p), F32)],
        compiler_params=_params(2),
        name="ssm_prep",
    )(*rows, *bt, *ct)


def _gelu_tanh(x):
    return 0.5 * x * (1.0 + jnp.tanh(math.sqrt(2.0 / math.pi) * (x + 0.044715 * (x * x * x))))


def _ssm_kernel(u_ref, m_intra_ref, m_in_ref, m_out_ref, p1_ref, p2_ref, d_ref, o_ref, *, n_chunks):
    t, cpg = SSM_CHUNK, SSM_CH_PER_GROUP
    n_rows = u_ref.shape[0] // t
    groups = u_ref.shape[1] // cpg
    by_pos = [u_ref[pl.ds(tau, n_rows, stride=t), :] for tau in range(t)]
    row = lax.broadcasted_iota(jnp.int32, (n_rows, 2 * SSM_STATE), 0)
    chunk = jnp.bitwise_and(row, n_chunks - 1)

    def chunks_back(x, n):
        return jnp.where(chunk >= n, pltpu.roll(x, n, axis=0), 0.0)

    ys = []
    for g in range(groups):
        lanes = slice(g * cpg, (g + 1) * cpg)
        u = jnp.concatenate([x[:, lanes] for x in by_pos], axis=1)
        ub = u.astype(BF16)
        y = jnp.dot(ub, m_intra_ref[g], preferred_element_type=F32)
        s_in = jnp.dot(ub, m_in_ref[g], preferred_element_type=F32)
        x = chunks_back(s_in, 1)
        for j in range((n_chunks - 1).bit_length()):
            prev = chunks_back(x, 1 << j)
            x = x + p1_ref[g, j:j + 1, :] * prev + p2_ref[g, j:j + 1, :] * pltpu.roll(prev, SSM_STATE, axis=1)
        y_state = lax.dot_general(x.astype(BF16), m_out_ref[g], (((1,), (1,)), ((), ())),
                                  preferred_element_type=F32)
        ys.append(_gelu_tanh(y + y_state + d_ref[g] * u))
    for tau in range(t):
        lanes = slice(tau * cpg, (tau + 1) * cpg)
        o_ref[pl.ds(tau, n_rows, stride=t), :] = jnp.concatenate([y[:, lanes] for y in ys], axis=1)


def _ssm(u, layer, mats, d_skip, *, n_chunks):
    m, width = u.shape
    m_intra, m_in, m_out, p1, p2 = mats
    gps = 128 // SSM_CH_PER_GROUP
    lg = lambda x: pl.BlockSpec((None, gps) + x.shape[2:], lambda gi: (layer, gi, 0, 0))
    d_tiled = jnp.tile(d_skip, (1, 1, SSM_CHUNK))[:, :, None, :]
    kern = functools.partial(_ssm_kernel, n_chunks=n_chunks)
    return pl.pallas_call(
        kern,
        grid=(width // 128,),
        in_specs=[pl.BlockSpec((m, 128), lambda gi: (0, gi)),
                  lg(m_intra), lg(m_in), lg(m_out), lg(p1), lg(p2), lg(d_tiled)],
        out_specs=pl.BlockSpec((m, 128), lambda gi: (0, gi)),
        out_shape=jax.ShapeDtypeStruct(u.shape, F32),
        compiler_params=_params(1),
        name="ssm_chunks",
    )(u, m_intra, m_in, m_out, p1, p2, d_tiled)


def _cast_kernel(x_ref, o_ref):
    o_ref[...] = x_ref[...].astype(o_ref.dtype)


def _cast_bf16(w, *, rows):
    nl, r, c = w.shape
    return pl.pallas_call(
        _cast_kernel,
        grid=(nl, r // rows),
        in_specs=[pl.BlockSpec((None, rows, c), lambda l, i: (l, i, 0))],
        out_specs=pl.BlockSpec((None, rows, c), lambda l, i: (l, i, 0)),
        out_shape=jax.ShapeDtypeStruct(w.shape, BF16),
        compiler_params=_params(2),
        name="cast_bf16",
    )(w)


def _outproj_kernel(h_ref, sb_ref, ssm_ref, df_ref, wglu_ref, bglu_ref, w_ref, g_ref, o_ref, mixed_ref):
    w_sb, w_ssm = sb_ref.shape[1], ssm_ref.shape[1]
    y = ssm_ref[...]
    gate = jax.nn.sigmoid(jnp.dot(y.astype(BF16), wglu_ref[...].astype(BF16), preferred_element_type=F32)
                          + bglu_ref[...])
    mixed_ref[:, 0:w_sb] = sb_ref[...]
    mixed_ref[:, w_sb:w_sb + w_ssm] = (y * gate).astype(BF16)
    mixed_ref[:, w_sb + w_ssm:] = df_ref[...]
    out = jnp.dot(mixed_ref[...], w_ref[...], preferred_element_type=F32)
    o_ref[...] = h_ref[...] + _rms(out, g_ref[...])


def _outproj(h, o_sb, y_ssm, o_diff, layer, w_glu, b_glu, w_out_bf, g, *, tm):
    m, d = h.shape
    k = w_out_bf.shape[1]
    row_block = lambda x: pl.BlockSpec((tm, x.shape[1]), lambda i: (i, 0))
    return pl.pallas_call(
        _outproj_kernel,
        grid=(m // tm,),
        in_specs=[
            row_block(h), row_block(o_sb), row_block(y_ssm), row_block(o_diff),
            pl.BlockSpec((None,) + w_glu.shape[1:], lambda i: (layer, 0, 0)),
            _layer_vec_spec(layer, SSM_WIDTH),
            pl.BlockSpec((None, k, d), lambda i: (layer, 0, 0), pipeline_mode=pl.Buffered(1)),
            _layer_vec_spec(layer, d),
        ],
        out_specs=row_block(h),
        out_shape=jax.ShapeDtypeStruct((m, d), F32),
        scratch_shapes=[pltpu.VMEM((tm, k), BF16)],
        compiler_params=_params(1),
        name="outproj",
    )(h, o_sb, y_ssm, o_diff, w_glu, _as_rows(b_glu), w_out_bf, _as_rows(g))


def _trunk(x, ffn1_pre_g, ffn1_w_gate, ffn1_w_up, ffn1_w_down, ffn1_post_g, mix_pre_g, w_in, ssm_a_re, ssm_a_im,
           ssm_log_dt, ssm_b_re, ssm_b_im, ssm_c_re, ssm_c_im, ssm_d, ssm_w_glu, ssm_b_glu, diff_lq1, diff_lk1,
           diff_lq2, diff_lk2, diff_subln_g, rel_bias, w_out, mix_post_g, ffn2_pre_g, ffn2_w_gate, ffn2_w_up,
           ffn2_w_down, ffn2_post_g, *, tm, tf, tm_out):
    batch, seq, d = x.shape
    m = batch * seq
    depth = w_in.shape[0]
    h = x.reshape(m, d)
    bias_tiles = _t5_bias_tiles(rel_bias)
    ssm_mats = _ssm_prep(ssm_a_re, ssm_a_im, ssm_log_dt, ssm_b_re, ssm_b_im, ssm_c_re, ssm_c_im)
    w_out_bf = _cast_bf16(w_out, rows=512)
    for l in range(depth):
        h = _ffn(h, l, ffn1_pre_g, ffn1_w_gate, ffn1_w_up, ffn1_w_down, ffn1_post_g, tm=tm, tf=tf)

        proj, ssm_u = _inproj(h, l, mix_pre_g, w_in, tm=tm)
        proj = proj.reshape(batch, seq, PROJ_COLS)

        o_sb = _sb_attention(proj, batch=batch, seq=seq)

        y_ssm = _ssm(ssm_u, l, ssm_mats, ssm_d, n_chunks=seq // SSM_CHUNK)

        lambda_init = 0.8 - 0.6 * math.exp(-0.3 * l)
        o_diff = _diff_attention(proj, bias_tiles, l, diff_lq1, diff_lk1, diff_lq2, diff_lk2, diff_subln_g,
                                 batch=batch, seq=seq, lambda_init=lambda_init)

        h = _outproj(h, o_sb.reshape(m, -1), y_ssm, o_diff.reshape(m, -1), l, ssm_w_glu, ssm_b_glu, w_out_bf,
                     mix_post_g, tm=tm_out)

        h = _ffn(h, l, ffn2_pre_g, ffn2_w_gate, ffn2_w_up, ffn2_w_down, ffn2_post_g, tm=tm, tf=tf)
    return h.reshape(batch, seq, d)


def kernel(x, ffn1_pre_g, ffn1_w_gate, ffn1_w_up, ffn1_w_down, ffn1_post_g, mix_pre_g, w_in, ssm_a_re, ssm_a_im, ssm_log_dt, ssm_b_re, ssm_b_im, ssm_c_re, ssm_c_im, ssm_d, ssm_w_glu, ssm_b_glu, diff_lq1, diff_lk1, diff_lq2, diff_lk2, diff_subln_g, rel_bias, w_out, mix_post_g, ffn2_pre_g, ffn2_w_gate, ffn2_w_up, ffn2_w_down, ffn2_post_g):
    n_chunks = x.shape[1] // SSM_CHUNK
    assert x.shape[1] % ATT_TILE == 0 and n_chunks <= 1 << SSM_SCAN_ROWS and n_chunks & (n_chunks - 1) == 0
    return _trunk(x, ffn1_pre_g, ffn1_w_gate, ffn1_w_up, ffn1_w_down, ffn1_post_g, mix_pre_g, w_in, ssm_a_re,
                  ssm_a_im, ssm_log_dt, ssm_b_re, ssm_b_im, ssm_c_re, ssm_c_im, ssm_d, ssm_w_glu, ssm_b_glu,
                  diff_lq1, diff_lk1, diff_lq2, diff_lk2, diff_subln_g, rel_bias, w_out, mix_post_g, ffn2_pre_g,
                  ffn2_w_gate, ffn2_w_up, ffn2_w_down, ffn2_post_g, tm=1024, tf=512, tm_out=512)
```

```python
import functools
import math

import jax
import jax.numpy as jnp
from jax import lax
from jax.experimental import pallas as pl
from jax.experimental.pallas import tpu as pltpu

F32 = jnp.float32
BF16 = jnp.bfloat16

HEAD_DIM = 128
SB_HEADS = 6
DIFF_HEADS = 6
DIFF_QK_DIM = HEAD_DIM // 2
SSM_GROUPS = 32
SSM_CH_PER_GROUP = 16
SSM_STATE = 64
SSM_WIDTH = SSM_GROUPS * SSM_CH_PER_GROUP
NUM_BUCKETS = 32
MAX_DISTANCE = 128
NORM_EPS = 1e-6
FFN_RESIDUAL_WEIGHT = 0.5
LOG2E = 1.0 / math.log(2.0)

SB_Q_BLK, SB_K_BLK, SB_V_BLK = 0, 6, 12
DF_Q_BLK, DF_K_BLK, DF_V_BLK = 18, 24, 30
PROJ_COLS = 36 * HEAD_DIM
SSM_COL0 = 3 * SB_HEADS * HEAD_DIM
W_IN_SUB = 256
PROJ_STEP_COLS = 3 * W_IN_SUB
SB_Q_STEP, DF_Q_STEP = SB_Q_BLK // 6, DF_Q_BLK // 6
SB_LOGIT_SCALE2 = LOG2E / math.sqrt(HEAD_DIM)
DF_LOGIT_SCALE2 = LOG2E / math.sqrt(DIFF_QK_DIM)

V7X_VMEM_LIMIT = 60 * 1024 * 1024
FFN_SUB = 256
ROW_CHUNK = 128
ATT_TILE = 256
SB_NEAR_TILES = 2
SB_SATURATED = 151.0
SSM_CHUNK = 16
SSM_PREP_GROUPS = 8
SSM_SCAN_ROWS = 8
NEG_INF = float(jnp.finfo(jnp.float32).min)


def _params(n_grid, vmem=V7X_VMEM_LIMIT):
    return pltpu.CompilerParams(dimension_semantics=("arbitrary",) * n_grid, vmem_limit_bytes=vmem)


def _rms(x, g):
    ms = jnp.mean(x * x, axis=-1, keepdims=True)
    return x * lax.rsqrt(ms + NORM_EPS) * g


def _for_row_chunks(n_rows, chunk, body):
    def step(i, carry):
        body(pl.multiple_of(i * chunk, chunk))
        return carry
    lax.fori_loop(0, n_rows // chunk, step, 0)


def _layer_vec_spec(layer, n):
    return pl.BlockSpec((None, 1, n), lambda *_: (layer, 0, 0))


def _as_rows(p):
    return p.reshape(p.shape[0], 1, p.shape[1])


def _ffn_kernel(x_ref, pre_g_ref, wg_ref, wu_ref, wd_ref, post_g_ref, o_ref, xn_ref, a_ref, *, n_f, n_col_chunks):
    f = pl.program_id(1)
    tm, d = x_ref.shape

    @pl.when(f == 0)
    def _():
        def norm_rows(r0):
            rows = pl.ds(r0, ROW_CHUNK)
            xn_ref[rows, :] = _rms(x_ref[rows, :], pre_g_ref[...]).astype(BF16)
            o_ref[rows, :] = jnp.zeros((ROW_CHUNK, d), F32)
        _for_row_chunks(tm, ROW_CHUNK, norm_rows)

    xn = xn_ref[...]
    tf = wg_ref.shape[1]
    for hf in range(tf // FFN_SUB):
        sub = slice(hf * FFN_SUB, (hf + 1) * FFN_SUB)
        g = jnp.dot(xn, wg_ref[:, sub].astype(BF16), preferred_element_type=F32)
        u = jnp.dot(xn, wu_ref[:, sub].astype(BF16), preferred_element_type=F32)
        a_ref[:, sub] = (g * jax.nn.sigmoid(g) * u).astype(BF16)
    a = a_ref[...]
    cw = d // n_col_chunks
    for c in range(n_col_chunks):
        cols = slice(c * cw, (c + 1) * cw)
        o_ref[:, cols] += jnp.dot(a, wd_ref[:, cols].astype(BF16), preferred_element_type=F32)

    @pl.when(f == n_f - 1)
    def _():
        def finish_rows(r0):
            rows = pl.ds(r0, ROW_CHUNK)
            o_ref[rows, :] = x_ref[rows, :] + FFN_RESIDUAL_WEIGHT * _rms(o_ref[rows, :], post_g_ref[...])
        _for_row_chunks(tm, ROW_CHUNK, finish_rows)


def _ffn(h, layer, pre_g, w_gate, w_up, w_down, post_g, *, tm, tf):
    m, d = h.shape
    d_ff = w_gate.shape[2]
    n_f = d_ff // tf
    kern = functools.partial(_ffn_kernel, n_f=n_f, n_col_chunks=4)
    return pl.pallas_call(
        kern,
        grid=(m // tm, n_f),
        in_specs=[
            pl.BlockSpec((tm, d), lambda i, f: (i, 0)),
            _layer_vec_spec(layer, d),
            pl.BlockSpec((None, d, tf), lambda i, f: (layer, 0, f)),
            pl.BlockSpec((None, d, tf), lambda i, f: (layer, 0, f)),
            pl.BlockSpec((None, tf, d), lambda i, f: (layer, f, 0)),
            _layer_vec_spec(layer, d),
        ],
        out_specs=pl.BlockSpec((tm, d), lambda i, f: (i, 0), pipeline_mode=pl.Buffered(1)),
        out_shape=jax.ShapeDtypeStruct((m, d), F32),
        scratch_shapes=[pltpu.VMEM((tm, d), BF16), pltpu.VMEM((tm, tf), BF16)],
        compiler_params=_params(2),
        name="ffn",
    )(h, _as_rows(pre_g), w_gate, w_up, w_down, _as_rows(post_g))


def _inproj_kernel(x_ref, g_ref, w0_ref, w1_ref, w2_ref, proj_ref, u_ref, xn_ref, *, n_head_steps):
    j = pl.program_id(1)
    tm = x_ref.shape[0]
    sub = W_IN_SUB

    @pl.when(j == 0)
    def _():
        def norm_rows(r0):
            rows = pl.ds(r0, ROW_CHUNK)
            xn_ref[rows, :] = _rms(x_ref[rows, :], g_ref[...]).astype(BF16)
        _for_row_chunks(tm, ROW_CHUNK, norm_rows)

    @pl.when(j < n_head_steps)
    def _():
        xn = xn_ref[...]
        scale = jnp.where(j == SB_Q_STEP, SB_LOGIT_SCALE2, jnp.where(j == DF_Q_STEP, DF_LOGIT_SCALE2, 1.0))
        for c, w_ref in enumerate((w0_ref, w1_ref, w2_ref)):
            y = jnp.dot(xn, w_ref[...].astype(BF16), preferred_element_type=F32)
            proj_ref[:, c * sub:(c + 1) * sub] = (y * scale).astype(proj_ref.dtype)

    @pl.when(j == n_head_steps)
    def _():
        xn = xn_ref[...]
        for c, w_ref in enumerate((w0_ref, w1_ref)):
            u_ref[:, c * sub:(c + 1) * sub] = jnp.dot(xn, w_ref[...].astype(BF16), preferred_element_type=F32)


def _inproj(h, layer, g, w_in, *, tm):
    m, d = h.shape
    n_head_steps = PROJ_COLS // PROJ_STEP_COLS
    ssm_blk0 = SSM_COL0 // W_IN_SUB
    n_ssm_blks = SSM_WIDTH // W_IN_SUB
    last_blk = w_in.shape[2] // W_IN_SUB - 1

    def w_spec(c):
        def index(i, j):
            head_blk = 3 * j + c + jnp.where(3 * j + c >= ssm_blk0, n_ssm_blks, 0)
            blk = jnp.where(j < n_head_steps, head_blk, ssm_blk0 + c)
            return (layer, 0, jnp.minimum(blk, last_blk))
        return pl.BlockSpec((None, d, W_IN_SUB), index)

    kern = functools.partial(_inproj_kernel, n_head_steps=n_head_steps)
    return pl.pallas_call(
        kern,
        grid=(m // tm, n_head_steps + 1),
        in_specs=[
            pl.BlockSpec((tm, d), lambda i, j: (i, 0)),
            _layer_vec_spec(layer, d),
            w_spec(0), w_spec(1), w_spec(2),
        ],
        out_specs=[
            pl.BlockSpec((tm, PROJ_STEP_COLS), lambda i, j: (i, jnp.minimum(j, n_head_steps - 1))),
            pl.BlockSpec((tm, SSM_WIDTH), lambda i, j: (i, 0)),
        ],
        out_shape=[jax.ShapeDtypeStruct((m, PROJ_COLS), BF16), jax.ShapeDtypeStruct((m, SSM_WIDTH), F32)],
        scratch_shapes=[pltpu.VMEM((tm, d), BF16)],
        compiler_params=_params(2),
        name="inproj",
    )(h, _as_rows(g), w_in, w_in, w_in)


def _split2_bf16(x):
    hi = x.astype(BF16)
    lo = (x - hi.astype(F32)).astype(BF16)
    return hi, lo


def _mask_first_tile(x, keep):
    t = ATT_TILE
    head = jnp.where(keep, x[0:t], 0.0)
    return head if x.shape[0] == t else jnp.concatenate([head, x[t:]], axis=0)


def _sb_tile_front(q, k, suffix_ones, first_tile_mask):
    z = lax.dot_general(q, k, (((1,), (1,)), ((), ())), preferred_element_type=F32)
    pos = jnp.maximum(z, 0.0)
    neg = jnp.minimum(z, 0.0)
    l2 = jnp.log2(1.0 + jnp.exp2(neg - pos))
    sp = pos + l2
    sp_in = sp if first_tile_mask is None else _mask_first_tile(sp, first_tile_mask)
    hi, lo = _split2_bf16(sp_in)
    suffix = jnp.dot(jnp.concatenate([hi, lo], axis=1), suffix_ones, preferred_element_type=F32)
    return (neg - l2) - suffix, jnp.sum(sp_in, axis=1, keepdims=True)


def _sb_tile_back(log_w, later_rows, v, first_tile_mask):
    w = jnp.exp2(log_w - later_rows)
    if first_tile_mask is not None:
        w = _mask_first_tile(w, first_tile_mask)
    return jnp.dot(w.astype(BF16), v, preferred_element_type=F32)


def _sb_tile(q, k, v, later_rows, suffix_ones, first_tile_mask):
    log_w, row_sum = _sb_tile_front(q, k, suffix_ones, first_tile_mask)
    return _sb_tile_back(log_w, later_rows, v, first_tile_mask), row_sum


def _sb_kernel(q_ref, k_ref, v_ref, o_ref, acc_ref, later_ref):
    t = ATT_TILE
    seq = q_ref.shape[0]
    n_tiles = seq // t
    row = lax.broadcasted_iota(jnp.int32, (t, t), 0)
    col = lax.broadcasted_iota(jnp.int32, (t, t), 1)
    strictly_lower = row > col
    ones = jnp.where(strictly_lower, 1.0, 0.0).astype(BF16)
    suffix_ones = jnp.concatenate([ones, ones], axis=0)
    acc = [None] * n_tiles
    later = [None] * n_tiles

    def front(j):
        r0 = j * t
        r1 = min(r0 + SB_NEAR_TILES * t, seq)
        return _sb_tile_front(q_ref[r0:r1, :], k_ref[r0:r0 + t, :], suffix_ones, strictly_lower)

    fronts = {n_tiles - 1: front(n_tiles - 1)}
    for j in range(n_tiles - 1, -1, -1):
        r0 = j * t
        r1 = min(r0 + SB_NEAR_TILES * t, seq)
        if j > 0:
            fronts[j - 1] = front(j - 1)
        log_w, row_sum = fronts.pop(j)
        later_rows = jnp.concatenate([jnp.zeros((t, 1), F32)] + later[j + 1:j + SB_NEAR_TILES], axis=0)
        pv = _sb_tile_back(log_w, later_rows, v_ref[r0:r0 + t, :], strictly_lower)
        for i in range(j, r1 // t):
            rows = slice((i - j) * t, (i - j + 1) * t)
            acc[i] = pv[rows] if i == j else acc[i] + pv[rows]
            later[i] = row_sum[rows] if i == j else later[i] + row_sum[rows]

    for i in range(n_tiles):
        acc_ref[i * t:(i + 1) * t, :] = acc[i]
        later_ref[i * t:(i + 1) * t, :] = later[i]

    def far_tiles_of(i, carry):
        rows = pl.ds(pl.multiple_of(i * t, t), t)
        n_far = i + 1 - SB_NEAR_TILES

        def far_tile(n, inner):
            k0 = pl.multiple_of((n_far - 1 - n) * t, t)
            pv, row_sum = _sb_tile(q_ref[rows, :], k_ref[pl.ds(k0, t), :], v_ref[pl.ds(k0, t), :], later_ref[rows, :],
                                   suffix_ones, None)
            acc_ref[rows, :] += pv
            later_ref[rows, :] += row_sum
            return inner

        lax.fori_loop(0, jnp.where(jnp.min(later_ref[rows, :]) < SB_SATURATED, n_far, 0), far_tile, 0)
        return carry

    if n_tiles > SB_NEAR_TILES:
        any_unsaturated = jnp.min(functools.reduce(jnp.minimum, later[SB_NEAR_TILES:])) < SB_SATURATED
        lax.fori_loop(SB_NEAR_TILES, jnp.where(any_unsaturated, n_tiles, SB_NEAR_TILES), far_tiles_of, 0)
    o_ref[...] = acc_ref[...].astype(o_ref.dtype)


def _sb_attention(proj, *, batch, seq):
    spec = lambda blk: pl.BlockSpec((None, seq, HEAD_DIM), lambda b, h: (b, 0, blk + h))
    return pl.pallas_call(
        _sb_kernel,
        grid=(batch, SB_HEADS),
        in_specs=[spec(SB_Q_BLK), spec(SB_K_BLK), spec(SB_V_BLK)],
        out_specs=spec(0),
        out_shape=jax.ShapeDtypeStruct((batch, seq, SB_HEADS * HEAD_DIM), BF16),
        scratch_shapes=[pltpu.VMEM((seq, HEAD_DIM), F32), pltpu.VMEM((seq, 1), F32)],
        compiler_params=_params(2),
        name="sb_attention",
    )(proj, proj, proj)


def _t5_bias_kernel(rel_ref, o_ref):
    t = ATT_TILE
    h = pl.program_id(0)
    row = lax.broadcasted_iota(jnp.int32, (t, t), 0)
    col = lax.broadcasted_iota(jnp.int32, (t, t), 1)
    max_exact = NUM_BUCKETS // 2
    for d in range(3):
        n = jnp.maximum(d * t + row - col, 0)
        nf = jnp.maximum(n, 1).astype(F32)
        large = max_exact + (jnp.log(nf / max_exact) / math.log(MAX_DISTANCE / max_exact)
                             * (NUM_BUCKETS - max_exact)).astype(jnp.int32)
        large = jnp.minimum(large, NUM_BUCKETS - 1)
        bucket = jnp.where(n < max_exact, n, large)
        bias = jnp.zeros((t, t), F32)
        for b in range(NUM_BUCKETS):
            bias = jnp.where(bucket == b, rel_ref[b, h], bias)
        o_ref[d] = bias * LOG2E


def _t5_bias_tiles(rel_bias):
    t = ATT_TILE
    return pl.pallas_call(
        _t5_bias_kernel,
        grid=(DIFF_HEADS,),
        in_specs=[pl.BlockSpec(memory_space=pltpu.SMEM)],
        out_specs=pl.BlockSpec((None, 3, t, t), lambda h: (h, 0, 0, 0)),
        out_shape=jax.ShapeDtypeStruct((DIFF_HEADS, 3, t, t), F32),
        compiler_params=_params(1),
        name="t5_bias_tiles",
    )(rel_bias)


def _diff_kernel(lq1_ref, lk1_ref, lq2_ref, lk2_ref, subln_g_ref, bias_ref, q_ref, k_ref, v_ref, o_ref,
                 *, lambda_init):
    t = ATT_TILE
    seq = q_ref.shape[0]
    n_tiles = seq // t
    lane = lax.broadcasted_iota(jnp.int32, (t, HEAD_DIM), 1)
    row = lax.broadcasted_iota(jnp.int32, (t, t), 0)
    col = lax.broadcasted_iota(jnp.int32, (t, t), 1)
    causal = col <= row
    lam = (jnp.exp(jnp.sum(lq1_ref[...] * lk1_ref[...], axis=1, keepdims=True))
           - jnp.exp(jnp.sum(lq2_ref[...] * lk2_ref[...], axis=1, keepdims=True)) + lambda_init)

    for i in range(n_tiles):
        n_keys = (i + 1) * t
        q = q_ref[i * t:(i + 1) * t, :]
        zero = jnp.zeros_like(q)
        q_maps = (jnp.where(lane < DIFF_QK_DIM, q, zero), jnp.where(lane >= DIFF_QK_DIM, q, zero))
        k = k_ref[0:n_keys, :]
        exps, sums = [], []
        for qm in q_maps:
            raw = lax.dot_general(qm, k, (((1,), (1,)), ((), ())), preferred_element_type=F32)
            parts = []
            for j in range(i + 1):
                s_j = raw[:, j * t:(j + 1) * t] + bias_ref[min(i - j, 2)]
                parts.append(jnp.where(causal, s_j, NEG_INF) if j == i else s_j)
            s = parts[0] if i == 0 else jnp.concatenate(parts, axis=1)
            e = jnp.exp2(s - jnp.max(s, axis=1, keepdims=True))
            exps.append(e)
            sums.append(jnp.sum(e, axis=1, keepdims=True))
        p = exps[0] - exps[1] * (lam * sums[0] / sums[1])
        o = jnp.dot(p.astype(BF16), v_ref[0:n_keys, :], preferred_element_type=F32) * (1.0 / sums[0])
        o_ref[i * t:(i + 1) * t, :] = (_rms(o, subln_g_ref[...]) * (1.0 - lambda_init)).astype(o_ref.dtype)


def _diff_attention(proj, bias_tiles, layer, lq1, lk1, lq2, lk2, subln_g, *, batch, seq, lambda_init):
    t = ATT_TILE
    vec = lambda n: _layer_vec_spec(layer, n)
    spec = lambda blk: pl.BlockSpec((None, seq, HEAD_DIM), lambda b, h: (b, 0, blk + h))
    kern = functools.partial(_diff_kernel, lambda_init=lambda_init)
    return pl.pallas_call(
        kern,
        grid=(batch, DIFF_HEADS),
        in_specs=[
            vec(DIFF_QK_DIM), vec(DIFF_QK_DIM), vec(DIFF_QK_DIM), vec(DIFF_QK_DIM), vec(HEAD_DIM),
            pl.BlockSpec((None, 3, t, t), lambda b, h: (h, 0, 0, 0)),
            spec(DF_Q_BLK), spec(DF_K_BLK), spec(DF_V_BLK),
        ],
        out_specs=spec(0),
        out_shape=jax.ShapeDtypeStruct((batch, seq, DIFF_HEADS * HEAD_DIM), BF16),
        compiler_params=_params(2),
        name="diff_attention",
    )(_as_rows(lq1), _as_rows(lk1), _as_rows(lq2), _as_rows(lk2), _as_rows(subln_g), bias_tiles, proj, proj, proj)


def _cpow(n, ar, ai, dt):
    mag = jnp.exp(n * (ar * dt))
    ang = n * (ai * dt)
    return mag * jnp.cos(ang), mag * jnp.sin(ang)


def _repeat_rows(x, reps):
    n, w = x.shape
    return jnp.broadcast_to(x[:, None, :], (n, reps, w)).reshape(n * reps, w)


def _ssm_prep_group(ar_ref, ai_ref, ldt_ref, bt_re_ref, bt_im_ref, ct_re_ref, ct_im_ref,
                    m_intra_ref, m_in_ref, m_out_ref, p1_ref, p2_ref):
    t = SSM_CHUNK
    cpg = SSM_CH_PER_GROUP
    ar, ai = ar_ref[...], ai_ref[...]
    dt = jnp.exp(ldt_ref[...])
    lb_re, lb_im = _cpow(1.0, ar, ai, dt)
    den = ar * ar + ai * ai
    f_re = ((lb_re - 1.0) * ar + lb_im * ai) / den
    f_im = (lb_im * ar - (lb_re - 1.0) * ai) / den
    bt_re, bt_im = bt_re_ref[...], bt_im_ref[...]
    bb_re = f_re * bt_re - f_im * bt_im
    bb_im = f_re * bt_im + f_im * bt_re
    ct_re, ct_im = ct_re_ref[...], ct_im_ref[...]
    pos = lax.broadcasted_iota(jnp.int32, (t, 2 * SSM_STATE), 0).astype(F32)
    lane_is_re = lax.broadcasted_iota(jnp.int32, (t * cpg, 2 * SSM_STATE), 1) < SSM_STATE

    def times_power(n, x_re, x_im, im_sign):
        e_re, e_im = _cpow(n, ar, ai, dt)
        e_re, e_im = _repeat_rows(e_re, cpg), _repeat_rows(e_im, cpg)
        return jnp.where(lane_is_re, e_re * x_re - e_im * x_im, im_sign * (e_re * x_im + e_im * x_re))

    m_in_ref[...] = times_power((t - 1.0) - pos, bb_re, bb_im, 1.0).astype(BF16)
    m_out_ref[...] = times_power(pos + 1.0, ct_re, ct_im, -1.0).astype(BF16)
    p_packed = times_power(-pos, bb_re, bb_im, -1.0)
    q_packed = times_power(pos, ct_re, ct_im, 1.0)
    k_full = lax.dot_general(p_packed, q_packed, (((1,), (1,)), ((), ())), preferred_element_type=F32,
                             precision=lax.Precision.HIGHEST)
    shift = cpg.bit_length() - 1
    row_k = lax.broadcasted_iota(jnp.int32, k_full.shape, 0)
    lane_k = lax.broadcasted_iota(jnp.int32, k_full.shape, 1)
    causal = jnp.right_shift(lane_k, shift) >= jnp.right_shift(row_k, shift)
    m_intra_ref[...] = jnp.where(causal, k_full, 0.0).astype(BF16)

    row_j = lax.broadcasted_iota(jnp.int32, (SSM_SCAN_ROWS, 2 * SSM_STATE), 0)
    lane_j = lax.broadcasted_iota(jnp.int32, (SSM_SCAN_ROWS, 2 * SSM_STATE), 1)
    pr, pi = _cpow(jnp.left_shift(t, row_j).astype(F32), ar, ai, dt)
    p1_ref[...] = pr
    p2_ref[...] = jnp.where(lane_j < SSM_STATE, -pi, pi)


def _ssm_prep_kernel(*refs):
    for g in range(refs[0].shape[0]):
        _ssm_prep_group(*[r.at[g] for r in refs])


def _ssm_prep(a_re, a_im, log_dt, b_re, b_im, c_re, c_im):
    nl, g, p = a_re.shape
    t, cpg = SSM_CHUNK, SSM_CH_PER_GROUP
    tc = t * cpg
    twice = lambda x: jnp.concatenate([x, x], axis=-1)
    rows = [twice(x)[:, :, None, :] for x in (a_re, a_im, jnp.broadcast_to(log_dt[:, :, None], a_re.shape))]
    bt = [jnp.tile(twice(jnp.swapaxes(b, 2, 3)), (1, 1, t, 1)) for b in (b_re, b_im)]
    ct = [jnp.tile(twice(c), (1, 1, t, 1)) for c in (c_re, c_im)]
    gps = SSM_PREP_GROUPS
    blk = lambda shape: pl.BlockSpec((None, gps) + shape, lambda l, gi: (l, gi, 0, 0))
    return pl.pallas_call(
        _ssm_prep_kernel,
        grid=(nl, g // gps),
        in_specs=[blk((1, 2 * p))] * 3 + [blk((tc, 2 * p))] * 4,
        out_specs=[blk((tc, tc)), blk((tc, 2 * p)), blk((tc, 2 * p)),
                   blk((SSM_SCAN_ROWS, 2 * p)), blk((SSM_SCAN_ROWS, 2 * p))],
        out_shape=[jax.ShapeDtypeStruct((nl, g, tc, tc), BF16),
                   jax.ShapeDtypeStruct((nl, g, tc, 2 * p), BF16),
                   jax.ShapeDtypeStruct((nl, g, tc, 2 * p), BF16),
                   jax.ShapeDtypeStruct((nl, g, SSM_SCAN_ROWS, 2 * p), F32),
                   jax.ShapeDtypeStruct((nl, g, SSM_SCAN_ROWS, 2 * p), F32)],
        compiler_params=_params(2),
        name="ssm_prep",
    )(*rows, *bt, *ct)


def _gelu_tanh(x):
    return 0.5 * x * (1.0 + jnp.tanh(math.sqrt(2.0 / math.pi) * (x + 0.044715 * (x * x * x))))


def _ssm_kernel(u_ref, m_intra_ref, m_in_ref, m_out_ref, p1_ref, p2_ref, d_ref, o_ref, *, n_chunks):
    t, cpg = SSM_CHUNK, SSM_CH_PER_GROUP
    n_rows = u_ref.shape[0] // t
    groups = u_ref.shape[1] // cpg
    by_pos = [u_ref[pl.ds(tau, n_rows, stride=t), :] for tau in range(t)]
    row = lax.broadcasted_iota(jnp.int32, (n_rows, 2 * SSM_STATE), 0)
    chunk = jnp.bitwise_and(row, n_chunks - 1)

    def chunks_back(x, n):
        return jnp.where(chunk >= n, pltpu.roll(x, n, axis=0), 0.0)

    ys = []
    for g in range(groups):
        lanes = slice(g * cpg, (g + 1) * cpg)
        u = jnp.concatenate([x[:, lanes] for x in by_pos], axis=1)
        ub = u.astype(BF16)
        y = jnp.dot(ub, m_intra_ref[g], preferred_element_type=F32)
        s_in = jnp.dot(ub, m_in_ref[g], preferred_element_type=F32)
        x = chunks_back(s_in, 1)
        for j in range((n_chunks - 1).bit_length()):
            prev = chunks_back(x, 1 << j)
            x = x + p1_ref[g, j:j + 1, :] * prev + p2_ref[g, j:j + 1, :] * pltpu.roll(prev, SSM_STATE, axis=1)
        y_state = lax.dot_general(x.astype(BF16), m_out_ref[g], (((1,), (1,)), ((), ())),
                                  preferred_element_type=F32)
        ys.append(_gelu_tanh(y + y_state + d_ref[g] * u))
    for tau in range(t):
        lanes = slice(tau * cpg, (tau + 1) * cpg)
        o_ref[pl.ds(tau, n_rows, stride=t), :] = jnp.concatenate([y[:, lanes] for y in ys], axis=1)


def _ssm(u, layer, mats, d_skip, *, n_chunks):
    m, width = u.shape
    m_intra, m_in, m_out, p1, p2 = mats
    gps = 128 // SSM_CH_PER_GROUP
    lg = lambda x: pl.BlockSpec((None, gps) + x.shape[2:], lambda gi: (layer, gi, 0, 0))
    d_tiled = jnp.tile(d_skip, (1, 1, SSM_CHUNK))[:, :, None, :]
    kern = functools.partial(_ssm_kernel, n_chunks=n_chunks)
    return pl.pallas_call(
        kern,
        grid=(width // 128,),
        in_specs=[pl.BlockSpec((m, 128), lambda gi: (0, gi)),
                  lg(m_intra), lg(m_in), lg(m_out), lg(p1), lg(p2), lg(d_tiled)],
        out_specs=pl.BlockSpec((m, 128), lambda gi: (0, gi)),
        out_shape=jax.ShapeDtypeStruct(u.shape, F32),
        compiler_params=_params(1),
        name="ssm_chunks",
    )(u, m_intra, m_in, m_out, p1, p2, d_tiled)


def _cast_kernel(x_ref, o_ref):
    o_ref[...] = x_ref[...].astype(o_ref.dtype)


def _cast_bf16(w, *, rows):
    nl, r, c = w.shape
    return pl.pallas_call(
        _cast_kernel,
        grid=(nl, r // rows),
        in_specs=[pl.BlockSpec((None, rows, c), lambda l, i: (l, i, 0))],
        out_specs=pl.BlockSpec((None, rows, c), lambda l, i: (l, i, 0)),
        out_shape=jax.ShapeDtypeStruct(w.shape, BF16),
        compiler_params=_params(2),
        name="cast_bf16",
    )(w)


def _outproj_kernel(h_ref, sb_ref, ssm_ref, df_ref, wglu_ref, bglu_ref, w_ref, g_ref, o_ref, mixed_ref):
    w_sb, w_ssm = sb_ref.shape[1], ssm_ref.shape[1]
    y = ssm_ref[...]
    gate = jax.nn.sigmoid(jnp.dot(y.astype(BF16), wglu_ref[...].astype(BF16), preferred_element_type=F32)
                          + bglu_ref[...])
    mixed_ref[:, 0:w_sb] = sb_ref[...]
    mixed_ref[:, w_sb:w_sb + w_ssm] = (y * gate).astype(BF16)
    mixed_ref[:, w_sb + w_ssm:] = df_ref[...]
    out = jnp.dot(mixed_ref[...], w_ref[...], preferred_element_type=F32)
    o_ref[...] = h_ref[...] + _rms(out, g_ref[...])


def _outproj(h, o_sb, y_ssm, o_diff, layer, w_glu, b_glu, w_out_bf, g, *, tm):
    m, d = h.shape
    k = w_out_bf.shape[1]
    row_block = lambda x: pl.BlockSpec((tm, x.shape[1]), lambda i: (i, 0))
    return pl.pallas_call(
        _outproj_kernel,
        grid=(m // tm,),
        in_specs=[
            row_block(h), row_block(o_sb), row_block(y_ssm), row_block(o_diff),
            pl.BlockSpec((None,) + w_glu.shape[1:], lambda i: (layer, 0, 0)),
            _layer_vec_spec(layer, SSM_WIDTH),
            pl.BlockSpec((None, k, d), lambda i: (layer, 0, 0), pipeline_mode=pl.Buffered(1)),
            _layer_vec_spec(layer, d),
        ],
        out_specs=row_block(h),
        out_shape=jax.ShapeDtypeStruct((m, d), F32),
        scratch_shapes=[pltpu.VMEM((tm, k), BF16)],
        compiler_params=_params(1),
        name="outproj",
    )(h, o_sb, y_ssm, o_diff, w_glu, _as_rows(b_glu), w_out_bf, _as_rows(g))


def _trunk(x, ffn1_pre_g, ffn1_w_gate, ffn1_w_up, ffn1_w_down, ffn1_post_g, mix_pre_g, w_in, ssm_a_re, ssm_a_im,
           ssm_log_dt, ssm_b_re, ssm_b_im, ssm_c_re, ssm_c_im, ssm_d, ssm_w_glu, ssm_b_glu, diff_lq1, diff_lk1,
           diff_lq2, diff_lk2, diff_subln_g, rel_bias, w_out, mix_post_g, ffn2_pre_g, ffn2_w_gate, ffn2_w_up,
           ffn2_w_down, ffn2_post_g, *, tm, tf, tm_out):
    batch, seq, d = x.shape
    m = batch * seq
    depth = w_in.shape[0]
    h = x.reshape(m, d)
    bias_tiles = _t5_bias_tiles(rel_bias)
    ssm_mats = _ssm_prep(ssm_a_re, ssm_a_im, ssm_log_dt, ssm_b_re, ssm_b_im, ssm_c_re, ssm_c_im)
    w_out_bf = _cast_bf16(w_out, rows=512)
    for l in range(depth):
        h = _ffn(h, l, ffn1_pre_g, ffn1_w_gate, ffn1_w_up, ffn1_w_down, ffn1_post_g, tm=tm, tf=tf)

        proj, ssm_u = _inproj(h, l, mix_pre_g, w_in, tm=tm)
        proj = proj.reshape(batch, seq, PROJ_COLS)

        o_sb = _sb_attention(proj, batch=batch, seq=seq)

        y_ssm = _ssm(ssm_u, l, ssm_mats, ssm_d, n_chunks=seq // SSM_CHUNK)

        lambda_init = 0.8 - 0.6 * math.exp(-0.3 * l)
        o_diff = _diff_attention(proj, bias_tiles, l, diff_lq1, diff_lk1, diff_lq2, diff_lk2, diff_subln_g,
                                 batch=batch, seq=seq, lambda_init=lambda_init)

        h = _outproj(h, o_sb.reshape(m, -1), y_ssm, o_diff.reshape(m, -1), l, ssm_w_glu, ssm_b_glu, w_out_bf,
                     mix_post_g, tm=tm_out)

        h = _ffn(h, l, ffn2_pre_g, ffn2_w_gate, ffn2_w_up, ffn2_w_down, ffn2_post_g, tm=tm, tf=tf)
    return h.reshape(batch, seq, d)


def kernel(x, ffn1_pre_g, ffn1_w_gate, ffn1_w_up, ffn1_w_down, ffn1_post_g, mix_pre_g, w_in, ssm_a_re, ssm_a_im, ssm_log_dt, ssm_b_re, ssm_b_im, ssm_c_re, ssm_c_im, ssm_d, ssm_w_glu, ssm_b_glu, diff_lq1, diff_lk1, diff_lq2, diff_lk2, diff_subln_g, rel_bias, w_out, mix_post_g, ffn2_pre_g, ffn2_w_gate, ffn2_w_up, ffn2_w_down, ffn2_post_g):
    n_chunks = x.shape[1] // SSM_CHUNK
    assert x.shape[1] % ATT_TILE == 0 and n_chunks <= 1 << SSM_SCAN_ROWS and n_chunks & (n_chunks - 1) == 0
    return _trunk(x, ffn1_pre_g, ffn1_w_gate, ffn1_w_up, ffn1_w_down, ffn1_post_g, mix_pre_g, w_in, ssm_a_re,
                  ssm_a_im, ssm_log_dt, ssm_b_re, ssm_b_im, ssm_c_re, ssm_c_im, ssm_d, ssm_w_glu, ssm_b_glu,
                  diff_lq1, diff_lk1, diff_lq2, diff_lk2, diff_subln_g, rel_bias, w_out, mix_post_g, ffn2_pre_g,
                  ffn2_w_gate, ffn2_w_up, ffn2_w_down, ffn2_post_g, tm=1024, tf=512, tm_out=512)
```

```python
import functools
import math

import jax
import jax.numpy as jnp
from jax import lax
from jax.experimental import pallas as pl
from jax.experimental.pallas import tpu as pltpu

F32 = jnp.float32
BF16 = jnp.bfloat16

HEAD_DIM = 128
SB_HEADS = 6
DIFF_HEADS = 6
DIFF_QK_DIM = HEAD_DIM // 2
SSM_GROUPS = 32
SSM_CH_PER_GROUP = 16
SSM_STATE = 64
SSM_WIDTH = SSM_GROUPS * SSM_CH_PER_GROUP
NUM_BUCKETS = 32
MAX_DISTANCE = 128
NORM_EPS = 1e-6
FFN_RESIDUAL_WEIGHT = 0.5
LOG2E = 1.0 / math.log(2.0)

SB_Q_BLK, SB_K_BLK, SB_V_BLK = 0, 6, 12
DF_Q_BLK, DF_K_BLK, DF_V_BLK = 18, 24, 30
PROJ_COLS = 36 * HEAD_DIM
SSM_COL0 = 3 * SB_HEADS * HEAD_DIM
W_IN_SUB = 256
PROJ_STEP_COLS = 3 * W_IN_SUB
SB_Q_STEP, DF_Q_STEP = SB_Q_BLK // 6, DF_Q_BLK // 6
SB_LOGIT_SCALE2 = LOG2E / math.sqrt(HEAD_DIM)
DF_LOGIT_SCALE2 = LOG2E / math.sqrt(DIFF_QK_DIM)

V7X_VMEM_LIMIT = 60 * 1024 * 1024
FFN_SUB = 256
ROW_CHUNK = 128
ATT_TILE = 256
SB_NEAR_TILES = 2
SB_SATURATED = 151.0
SSM_CHUNK = 16
SSM_PREP_GROUPS = 8
SSM_SCAN_ROWS = 8
NEG_INF = float(jnp.finfo(jnp.float32).min)


def _params(n_grid, vmem=V7X_VMEM_LIMIT):
    return pltpu.CompilerParams(dimension_semantics=("arbitrary",) * n_grid, vmem_limit_bytes=vmem)


def _rms(x, g):
    ms = jnp.mean(x * x, axis=-1, keepdims=True)
    return x * lax.rsqrt(ms + NORM_EPS) * g


def _for_row_chunks(n_rows, chunk, body):
    def step(i, carry):
        body(pl.multiple_of(i * chunk, chunk))
        return carry
    lax.fori_loop(0, n_rows // chunk, step, 0)


def _layer_vec_spec(layer, n):
    return pl.BlockSpec((None, 1, n), lambda *_: (layer, 0, 0))


def _as_rows(p):
    return p.reshape(p.shape[0], 1, p.shape[1])


def _ffn_kernel(x_ref, pre_g_ref, wg_ref, wu_ref, wd_ref, post_g_ref, o_ref, xn_ref, a_ref, *, n_f, n_col_chunks):
    f = pl.program_id(1)
    tm, d = x_ref.shape

    @pl.when(f == 0)
    def _():
        def norm_rows(r0):
            rows = pl.ds(r0, ROW_CHUNK)
            xn_ref[rows, :] = _rms(x_ref[rows, :], pre_g_ref[...]).astype(BF16)
            o_ref[rows, :] = jnp.zeros((ROW_CHUNK, d), F32)
        _for_row_chunks(tm, ROW_CHUNK, norm_rows)

    xn = xn_ref[...]
    tf = wg_ref.shape[1]
    for hf in range(tf // FFN_SUB):
        sub = slice(hf * FFN_SUB, (hf + 1) * FFN_SUB)
        g = jnp.dot(xn, wg_ref[:, sub].astype(BF16), preferred_element_type=F32)
        u = jnp.dot(xn, wu_ref[:, sub].astype(BF16), preferred_element_type=F32)
        a_ref[:, sub] = (g * jax.nn.sigmoid(g) * u).astype(BF16)
    a = a_ref[...]
    cw = d // n_col_chunks
    for c in range(n_col_chunks):
        cols = slice(c * cw, (c + 1) * cw)
        o_ref[:, cols] += jnp.dot(a, wd_ref[:, cols].astype(BF16), preferred_element_type=F32)

    @pl.when(f == n_f - 1)
    def _():
        def finish_rows(r0):
            rows = pl.ds(r0, ROW_CHUNK)
            o_ref[rows, :] = x_ref[rows, :] + FFN_RESIDUAL_WEIGHT * _rms(o_ref[rows, :], post_g_ref[...])
        _for_row_chunks(tm, ROW_CHUNK, finish_rows)


def _ffn(h, layer, pre_g, w_gate, w_up, w_down, post_g, *, tm, tf):
    m, d = h.shape
    d_ff = w_gate.shape[2]
    n_f = d_ff // tf
    kern = functools.partial(_ffn_kernel, n_f=n_f, n_col_chunks=4)
    return pl.pallas_call(
        kern,
        grid=(m // tm, n_f),
        in_specs=[
            pl.BlockSpec((tm, d), lambda i, f: (i, 0)),
            _layer_vec_spec(layer, d),
            pl.BlockSpec((None, d, tf), lambda i, f: (layer, 0, f)),
            pl.BlockSpec((None, d, tf), lambda i, f: (layer, 0, f)),
            pl.BlockSpec((None, tf, d), lambda i, f: (layer, f, 0)),
            _layer_vec_spec(layer, d),
        ],
        out_specs=pl.BlockSpec((tm, d), lambda i, f: (i, 0), pipeline_mode=pl.Buffered(1)),
        out_shape=jax.ShapeDtypeStruct((m, d), F32),
        scratch_shapes=[pltpu.VMEM((tm, d), BF16), pltpu.VMEM((tm, tf), BF16)],
        compiler_params=_params(2),
        name="ffn",
    )(h, _as_rows(pre_g), w_gate, w_up, w_down, _as_rows(post_g))


def _inproj_kernel(x_ref, g_ref, w0_ref, w1_ref, w2_ref, proj_ref, u_ref, xn_ref, *, n_head_steps):
    j = pl.program_id(1)
    tm = x_ref.shape[0]
    sub = W_IN_SUB

    @pl.when(j == 0)
    def _():
        def norm_rows(r0):
            rows = pl.ds(r0, ROW_CHUNK)
            xn_ref[rows, :] = _rms(x_ref[rows, :], g_ref[...]).astype(BF16)
        _for_row_chunks(tm, ROW_CHUNK, norm_rows)

    @pl.when(j < n_head_steps)
    def _():
        xn = xn_ref[...]
        scale = jnp.where(j == SB_Q_STEP, SB_LOGIT_SCALE2, jnp.where(j == DF_Q_STEP, DF_LOGIT_SCALE2, 1.0))
        for c, w_ref in enumerate((w0_ref, w1_ref, w2_ref)):
            y = jnp.dot(xn, w_ref[...].astype(BF16), preferred_element_type=F32)
            proj_ref[:, c * sub:(c + 1) * sub] = (y * scale).astype(proj_ref.dtype)

    @pl.when(j == n_head_steps)
    def _():
        xn = xn_ref[...]
        for c, w_ref in enumerate((w0_ref, w1_ref)):
            u_ref[:, c * sub:(c + 1) * sub] = jnp.dot(xn, w_ref[...].astype(BF16), preferred_element_type=F32)


def _inproj(h, layer, g, w_in, *, tm):
    m, d = h.shape
    n_head_steps = PROJ_COLS // PROJ_STEP_COLS
    ssm_blk0 = SSM_COL0 // W_IN_SUB
    n_ssm_blks = SSM_WIDTH // W_IN_SUB
    last_blk = w_in.shape[2] // W_IN_SUB - 1

    def w_spec(c):
        def index(i, j):
            head_blk = 3 * j + c + jnp.where(3 * j + c >= ssm_blk0, n_ssm_blks, 0)
            blk = jnp.where(j < n_head_steps, head_blk, ssm_blk0 + c)
            return (layer, 0, jnp.minimum(blk, last_blk))
        return pl.BlockSpec((None, d, W_IN_SUB), index)

    kern = functools.partial(_inproj_kernel, n_head_steps=n_head_steps)
    return pl.pallas_call(
        kern,
        grid=(m // tm, n_head_steps + 1),
        in_specs=[
            pl.BlockSpec((tm, d), lambda i, j: (i, 0)),
            _layer_vec_spec(layer, d),
            w_spec(0), w_spec(1), w_spec(2),
        ],
        out_specs=[
            pl.BlockSpec((tm, PROJ_STEP_COLS), lambda i, j: (i, jnp.minimum(j, n_head_steps - 1))),
            pl.BlockSpec((tm, SSM_WIDTH), lambda i, j: (i, 0)),
        ],
        out_shape=[jax.ShapeDtypeStruct((m, PROJ_COLS), BF16), jax.ShapeDtypeStruct((m, SSM_WIDTH), F32)],
        scratch_shapes=[pltpu.VMEM((tm, d), BF16)],
        compiler_params=_params(2),
        name="inproj",
    )(h, _as_rows(g), w_in, w_in, w_in)


def _split2_bf16(x):
    hi = x.astype(BF16)
    lo = (x - hi.astype(F32)).astype(BF16)
    return hi, lo


def _mask_first_tile(x, keep):
    t = ATT_TILE
    head = jnp.where(keep, x[0:t], 0.0)
    return head if x.shape[0] == t else jnp.concatenate([head, x[t:]], axis=0)


def _sb_tile_front(q, k, suffix_ones, first_tile_mask):
    z = lax.dot_general(q, k, (((1,), (1,)), ((), ())), preferred_element_type=F32)
    pos = jnp.maximum(z, 0.0)
    neg = jnp.minimum(z, 0.0)
    l2 = jnp.log2(1.0 + jnp.exp2(neg - pos))
    sp = pos + l2
    sp_in = sp if first_tile_mask is None else _mask_first_tile(sp, first_tile_mask)
    hi, lo = _split2_bf16(sp_in)
    suffix = jnp.dot(jnp.concatenate([hi, lo], axis=1), suffix_ones, preferred_element_type=F32)
    return (neg - l2) - suffix, jnp.sum(sp_in, axis=1, keepdims=True)


def _sb_tile_back(log_w, later_rows, v, first_tile_mask):
    w = jnp.exp2(log_w - later_rows)
    if first_tile_mask is not None:
        w = _mask_first_tile(w, first_tile_mask)
    return jnp.dot(w.astype(BF16), v, preferred_element_type=F32)


def _sb_tile(q, k, v, later_rows, suffix_ones, first_tile_mask):
    log_w, row_sum = _sb_tile_front(q, k, suffix_ones, first_tile_mask)
    return _sb_tile_back(log_w, later_rows, v, first_tile_mask), row_sum


def _sb_kernel(q_ref, k_ref, v_ref, o_ref, acc_ref, later_ref):
    t = ATT_TILE
    seq = q_ref.shape[0]
    n_tiles = seq // t
    row = lax.broadcasted_iota(jnp.int32, (t, t), 0)
    col = lax.broadcasted_iota(jnp.int32, (t, t), 1)
    strictly_lower = row > col
    ones = jnp.where(strictly_lower, 1.0, 0.0).astype(BF16)
    suffix_ones = jnp.concatenate([ones, ones], axis=0)
    acc = [None] * n_tiles
    later = [None] * n_tiles

    def front(j):
        r0 = j * t
        r1 = min(r0 + SB_NEAR_TILES * t, seq)
        return _sb_tile_front(q_ref[r0:r1, :], k_ref[r0:r0 + t, :], suffix_ones, strictly_lower)

    fronts = {n_tiles - 1: front(n_tiles - 1)}
    for j in range(n_tiles - 1, -1, -1):
        r0 = j * t
        r1 = min(r0 + SB_NEAR_TILES * t, seq)
        if j > 0:
            fronts[j - 1] = front(j - 1)
        log_w, row_sum = fronts.pop(j)
        later_rows = jnp.concatenate([jnp.zeros((t, 1), F32)] + later[j + 1:j + SB_NEAR_TILES], axis=0)
        pv = _sb_tile_back(log_w, later_rows, v_ref[r0:r0 + t, :], strictly_lower)
        for i in range(j, r1 // t):
            rows = slice((i - j) * t, (i - j + 1) * t)
            acc[i] = pv[rows] if i == j else acc[i] + pv[rows]
            later[i] = row_sum[rows] if i == j else later[i] + row_sum[rows]

    for i in range(n_tiles):
        acc_ref[i * t:(i + 1) * t, :] = acc[i]
        later_ref[i * t:(i + 1) * t, :] = later[i]

    def far_tiles_of(i, carry):
        rows = pl.ds(pl.multiple_of(i * t, t), t)
        n_far = i + 1 - SB_NEAR_TILES

        def far_tile(n, inner):
            k0 = pl.multiple_of((n_far - 1 - n) * t, t)
            pv, row_sum = _sb_tile(q_ref[rows, :], k_ref[pl.ds(k0, t), :], v_ref[pl.ds(k0, t), :], later_ref[rows, :],
                                   suffix_ones, None)
            acc_ref[rows, :] += pv
            later_ref[rows, :] += row_sum
            return inner

        lax.fori_loop(0, jnp.where(jnp.min(later_ref[rows, :]) < SB_SATURATED, n_far, 0), far_tile, 0)
        return carry

    if n_tiles > SB_NEAR_TILES:
        any_unsaturated = jnp.min(functools.reduce(jnp.minimum, later[SB_NEAR_TILES:])) < SB_SATURATED
        lax.fori_loop(SB_NEAR_TILES, jnp.where(any_unsaturated, n_tiles, SB_NEAR_TILES), far_tiles_of, 0)
    o_ref[...] = acc_ref[...].astype(o_ref.dtype)


def _sb_attention(proj, *, batch, seq):
    spec = lambda blk: pl.BlockSpec((None, seq, HEAD_DIM), lambda b, h: (b, 0, blk + h))
    return pl.pallas_call(
        _sb_kernel,
        grid=(batch, SB_HEADS),
        in_specs=[spec(SB_Q_BLK), spec(SB_K_BLK), spec(SB_V_BLK)],
        out_specs=spec(0),
        out_shape=jax.ShapeDtypeStruct((batch, seq, SB_HEADS * HEAD_DIM), BF16),
        scratch_shapes=[pltpu.VMEM((seq, HEAD_DIM), F32), pltpu.VMEM((seq, 1), F32)],
        compiler_params=_params(2),
        name="sb_attention",
    )(proj, proj, proj)


def _t5_bias_kernel(rel_ref, o_ref):
    t = ATT_TILE
    h = pl.program_id(0)
    row = lax.broadcasted_iota(jnp.int32, (t, t), 0)
    col = lax.broadcasted_iota(jnp.int32, (t, t), 1)
    max_exact = NUM_BUCKETS // 2
    for d in range(3):
        n = jnp.maximum(d * t + row - col, 0)
        nf = jnp.maximum(n, 1).astype(F32)
        large = max_exact + (jnp.log(nf / max_exact) / math.log(MAX_DISTANCE / max_exact)
                             * (NUM_BUCKETS - max_exact)).astype(jnp.int32)
        large = jnp.minimum(large, NUM_BUCKETS - 1)
        bucket = jnp.where(n < max_exact, n, large)
        bias = jnp.zeros((t, t), F32)
        for b in range(NUM_BUCKETS):
            bias = jnp.where(bucket == b, rel_ref[b, h], bias)
        o_ref[d] = bias * LOG2E


def _t5_bias_tiles(rel_bias):
    t = ATT_TILE
    return pl.pallas_call(
        _t5_bias_kernel,
        grid=(DIFF_HEADS,),
        in_specs=[pl.BlockSpec(memory_space=pltpu.SMEM)],
        out_specs=pl.BlockSpec((None, 3, t, t), lambda h: (h, 0, 0, 0)),
        out_shape=jax.ShapeDtypeStruct((DIFF_HEADS, 3, t, t), F32),
        compiler_params=_params(1),
        name="t5_bias_tiles",
    )(rel_bias)


def _diff_kernel(lq1_ref, lk1_ref, lq2_ref, lk2_ref, subln_g_ref, bias_ref, q_ref, k_ref, v_ref, o_ref,
                 *, lambda_init):
    t = ATT_TILE
    seq = q_ref.shape[0]
    n_tiles = seq // t
    lane = lax.broadcasted_iota(jnp.int32, (t, HEAD_DIM), 1)
    row = lax.broadcasted_iota(jnp.int32, (t, t), 0)
    col = lax.broadcasted_iota(jnp.int32, (t, t), 1)
    causal = col <= row
    lam = (jnp.exp(jnp.sum(lq1_ref[...] * lk1_ref[...], axis=1, keepdims=True))
           - jnp.exp(jnp.sum(lq2_ref[...] * lk2_ref[...], axis=1, keepdims=True)) + lambda_init)

    def logits_of(i):
        q = q_ref[i * t:(i + 1) * t, :]
        zero = jnp.zeros_like(q)
        q_maps = (jnp.where(lane < DIFF_QK_DIM, q, zero), jnp.where(lane >= DIFF_QK_DIM, q, zero))
        k = k_ref[0:(i + 1) * t, :]
        return [lax.dot_general(qm, k, (((1,), (1,)), ((), ())), preferred_element_type=F32) for qm in q_maps]

    raws = {0: logits_of(0)}
    for i in range(n_tiles):
        n_keys = (i + 1) * t
        if i + 1 < n_tiles:
            raws[i + 1] = logits_of(i + 1)
        exps, sums = [], []
        for raw in raws.pop(i):
            parts = []
            for j in range(i + 1):
                s_j = raw[:, j * t:(j + 1) * t] + bias_ref[min(i - j, 2)]
                parts.append(jnp.where(causal, s_j, NEG_INF) if j == i else s_j)
            s = parts[0] if i == 0 else jnp.concatenate(parts, axis=1)
            e = jnp.exp2(s - jnp.max(s, axis=1, keepdims=True))
            exps.append(e)
            sums.append(jnp.sum(e, axis=1, keepdims=True))
        p = exps[0] - exps[1] * (lam * sums[0] / sums[1])
        o = jnp.dot(p.astype(BF16), v_ref[0:n_keys, :], preferred_element_type=F32) * (1.0 / sums[0])
        o_ref[i * t:(i + 1) * t, :] = (_rms(o, subln_g_ref[...]) * (1.0 - lambda_init)).astype(o_ref.dtype)


def _diff_attention(proj, bias_tiles, layer, lq1, lk1, lq2, lk2, subln_g, *, batch, seq, lambda_init):
    t = ATT_TILE
    vec = lambda n: _layer_vec_spec(layer, n)
    spec = lambda blk: pl.BlockSpec((None, seq, HEAD_DIM), lambda b, h: (b, 0, blk + h))
    kern = functools.partial(_diff_kernel, lambda_init=lambda_init)
    return pl.pallas_call(
        kern,
        grid=(batch, DIFF_HEADS),
        in_specs=[
            vec(DIFF_QK_DIM), vec(DIFF_QK_DIM), vec(DIFF_QK_DIM), vec(DIFF_QK_DIM), vec(HEAD_DIM),
            pl.BlockSpec((None, 3, t, t), lambda b, h: (h, 0, 0, 0)),
            spec(DF_Q_BLK), spec(DF_K_BLK), spec(DF_V_BLK),
        ],
        out_specs=spec(0),
        out_shape=jax.ShapeDtypeStruct((batch, seq, DIFF_HEADS * HEAD_DIM), BF16),
        compiler_params=_params(2),
        name="diff_attention",
    )(_as_rows(lq1), _as_rows(lk1), _as_rows(lq2), _as_rows(lk2), _as_rows(subln_g), bias_tiles, proj, proj, proj)


def _cpow(n, ar, ai, dt):
    mag = jnp.exp(n * (ar * dt))
    ang = n * (ai * dt)
    return mag * jnp.cos(ang), mag * jnp.sin(ang)


def _repeat_rows(x, reps):
    n, w = x.shape
    return jnp.broadcast_to(x[:, None, :], (n, reps, w)).reshape(n * reps, w)


def _ssm_prep_group(ar_ref, ai_ref, ldt_ref, bt_re_ref, bt_im_ref, ct_re_ref, ct_im_ref,
                    m_intra_ref, m_in_ref, m_out_ref, p1_ref, p2_ref):
    t = SSM_CHUNK
    cpg = SSM_CH_PER_GROUP
    ar, ai = ar_ref[...], ai_ref[...]
    dt = jnp.exp(ldt_ref[...])
    lb_re, lb_im = _cpow(1.0, ar, ai, dt)
    den = ar * ar + ai * ai
    f_re = ((lb_re - 1.0) * ar + lb_im * ai) / den
    f_im = (lb_im * ar - (lb_re - 1.0) * ai) / den
    bt_re, bt_im = bt_re_ref[...], bt_im_ref[...]
    bb_re = f_re * bt_re - f_im * bt_im
    bb_im = f_re * bt_im + f_im * bt_re
    ct_re, ct_im = ct_re_ref[...], ct_im_ref[...]
    pos = lax.broadcasted_iota(jnp.int32, (t, 2 * SSM_STATE), 0).astype(F32)
    lane_is_re = lax.broadcasted_iota(jnp.int32, (t * cpg, 2 * SSM_STATE), 1) < SSM_STATE

    def times_power(n, x_re, x_im, im_sign):
        e_re, e_im = _cpow(n, ar, ai, dt)
        e_re, e_im = _repeat_rows(e_re, cpg), _repeat_rows(e_im, cpg)
        return jnp.where(lane_is_re, e_re * x_re - e_im * x_im, im_sign * (e_re * x_im + e_im * x_re))

    m_in_ref[...] = times_power((t - 1.0) - pos, bb_re, bb_im, 1.0).astype(BF16)
    m_out_ref[...] = times_power(pos + 1.0, ct_re, ct_im, -1.0).astype(BF16)
    p_packed = times_power(-pos, bb_re, bb_im, -1.0)
    q_packed = times_power(pos, ct_re, ct_im, 1.0)
    k_full = lax.dot_general(p_packed, q_packed, (((1,), (1,)), ((), ())), preferred_element_type=F32,
                             precision=lax.Precision.HIGHEST)
    shift = cpg.bit_length() - 1
    row_k = lax.broadcasted_iota(jnp.int32, k_full.shape, 0)
    lane_k = lax.broadcasted_iota(jnp.int32, k_full.shape, 1)
    causal = jnp.right_shift(lane_k, shift) >= jnp.right_shift(row_k, shift)
    m_intra_ref[...] = jnp.where(causal, k_full, 0.0).astype(BF16)

    row_j = lax.broadcasted_iota(jnp.int32, (SSM_SCAN_ROWS, 2 * SSM_STATE), 0)
    lane_j = lax.broadcasted_iota(jnp.int32, (SSM_SCAN_ROWS, 2 * SSM_STATE), 1)
    pr, pi = _cpow(jnp.left_shift(t, row_j).astype(F32), ar, ai, dt)
    p1_ref[...] = pr
    p2_ref[...] = jnp.where(lane_j < SSM_STATE, -pi, pi)


def _ssm_prep_kernel(*refs):
    for g in range(refs[0].shape[0]):
        _ssm_prep_group(*[r.at[g] for r in refs])


def _ssm_prep(a_re, a_im, log_dt, b_re, b_im, c_re, c_im):
    nl, g, p = a_re.shape
    t, cpg = SSM_CHUNK, SSM_CH_PER_GROUP
    tc = t * cpg
    twice = lambda x: jnp.concatenate([x, x], axis=-1)
    rows = [twice(x)[:, :, None, :] for x in (a_re, a_im, jnp.broadcast_to(log_dt[:, :, None], a_re.shape))]
    bt = [jnp.tile(twice(jnp.swapaxes(b, 2, 3)), (1, 1, t, 1)) for b in (b_re, b_im)]
    ct = [jnp.tile(twice(c), (1, 1, t, 1)) for c in (c_re, c_im)]
    gps = SSM_PREP_GROUPS
    blk = lambda shape: pl.BlockSpec((None, gps) + shape, lambda l, gi: (l, gi, 0, 0))
    return pl.pallas_call(
        _ssm_prep_kernel,
        grid=(nl, g // gps),
        in_specs=[blk((1, 2 * p))] * 3 + [blk((tc, 2 * p))] * 4,
        out_specs=[blk((tc, tc)), blk((tc, 2 * p)), blk((tc, 2 * p)),
                   blk((SSM_SCAN_ROWS, 2 * p)), blk((SSM_SCAN_ROWS, 2 * p))],
        out_shape=[jax.ShapeDtypeStruct((nl, g, tc, tc), BF16),
                   jax.ShapeDtypeStruct((nl, g, tc, 2 * p), BF16),
                   jax.ShapeDtypeStruct((nl, g, tc, 2 * p), BF16),
                   jax.ShapeDtypeStruct((nl, g, SSM_SCAN_ROWS, 2 * p), F32),
                   jax.ShapeDtypeStruct((nl, g, SSM_SCAN_ROWS, 2 * p), F32)],
        compiler_params=_params(2),
        name="ssm_prep",
    )(*rows, *bt, *ct)


def _gelu_tanh(x):
    return 0.5 * x * (1.0 + jnp.tanh(math.sqrt(2.0 / math.pi) * (x + 0.044715 * (x * x * x))))


def _ssm_kernel(u_ref, m_intra_ref, m_in_ref, m_out_ref, p1_ref, p2_ref, d_ref, o_ref, *, n_chunks):
    t, cpg = SSM_CHUNK, SSM_CH_PER_GROUP
    n_rows = u_ref.shape[0] // t
    groups = u_ref.shape[1] // cpg
    by_pos = [u_ref[pl.ds(tau, n_rows, stride=t), :] for tau in range(t)]
    row = lax.broadcasted_iota(jnp.int32, (n_rows, 2 * SSM_STATE), 0)
    chunk = jnp.bitwise_and(row, n_chunks - 1)

    def chunks_back(x, n):
        return jnp.where(chunk >= n, pltpu.roll(x, n, axis=0), 0.0)

    def regroup(g):
        return jnp.concatenate([x[:, g * cpg:(g + 1) * cpg] for x in by_pos], axis=1)

    ys = []
    us = {0: regroup(0)}
    for g in range(groups):
        if g + 1 < groups:
            us[g + 1] = regroup(g + 1)
        u = us.pop(g)
        ub = u.astype(BF16)
        y = jnp.dot(ub, m_intra_ref[g], preferred_element_type=F32)
        s_in = jnp.dot(ub, m_in_ref[g], preferred_element_type=F32)
        x = chunks_back(s_in, 1)
        for j in range((n_chunks - 1).bit_length()):
            prev = chunks_back(x, 1 << j)
            x = x + p1_ref[g, j:j + 1, :] * prev + p2_ref[g, j:j + 1, :] * pltpu.roll(prev, SSM_STATE, axis=1)
        y_state = lax.dot_general(x.astype(BF16), m_out_ref[g], (((1,), (1,)), ((), ())),
                                  preferred_element_type=F32)
        ys.append(_gelu_tanh(y + y_state + d_ref[g] * u))
    for tau in range(t):
        lanes = slice(tau * cpg, (tau + 1) * cpg)
        o_ref[pl.ds(tau, n_rows, stride=t), :] = jnp.concatenate([y[:, lanes] for y in ys], axis=1)


def _ssm(u, layer, mats, d_skip, *, n_chunks):
    m, width = u.shape
    m_intra, m_in, m_out, p1, p2 = mats
    gps = 128 // SSM_CH_PER_GROUP
    lg = lambda x: pl.BlockSpec((None, gps) + x.shape[2:], lambda gi: (layer, gi, 0, 0))
    d_tiled = jnp.tile(d_skip, (1, 1, SSM_CHUNK))[:, :, None, :]
    kern = functools.partial(_ssm_kernel, n_chunks=n_chunks)
    return pl.pallas_call(
        kern,
        grid=(width // 128,),
        in_specs=[pl.BlockSpec((m, 128), lambda gi: (0, gi)),
                  lg(m_intra), lg(m_in), lg(m_out), lg(p1), lg(p2), lg(d_tiled)],
        out_specs=pl.BlockSpec((m, 128), lambda gi: (0, gi)),
        out_shape=jax.ShapeDtypeStruct(u.shape, F32),
        compiler_params=_params(1),
        name="ssm_chunks",
    )(u, m_intra, m_in, m_out, p1, p2, d_tiled)


def _cast_kernel(x_ref, o_ref):
    o_ref[...] = x_ref[...].astype(o_ref.dtype)


def _cast_bf16(w, *, rows):
    nl, r, c = w.shape
    return pl.pallas_call(
        _cast_kernel,
        grid=(nl, r // rows),
        in_specs=[pl.BlockSpec((None, rows, c), lambda l, i: (l, i, 0))],
        out_specs=pl.BlockSpec((None, rows, c), lambda l, i: (l, i, 0)),
        out_shape=jax.ShapeDtypeStruct(w.shape, BF16),
        compiler_params=_params(2),
        name="cast_bf16",
    )(w)


def _outproj_kernel(h_ref, sb_ref, ssm_ref, df_ref, wglu_ref, bglu_ref, w_ref, g_ref, o_ref, mixed_ref):
    w_sb, w_ssm = sb_ref.shape[1], ssm_ref.shape[1]
    y = ssm_ref[...]
    gate = jax.nn.sigmoid(jnp.dot(y.astype(BF16), wglu_ref[...].astype(BF16), preferred_element_type=F32)
                          + bglu_ref[...])
    mixed_ref[:, 0:w_sb] = sb_ref[...]
    mixed_ref[:, w_sb:w_sb + w_ssm] = (y * gate).astype(BF16)
    mixed_ref[:, w_sb + w_ssm:] = df_ref[...]
    out = jnp.dot(mixed_ref[...], w_ref[...], preferred_element_type=F32)
    o_ref[...] = h_ref[...] + _rms(out, g_ref[...])


def _outproj(h, o_sb, y_ssm, o_diff, layer, w_glu, b_glu, w_out_bf, g, *, tm):
    m, d = h.shape
    k = w_out_bf.shape[1]
    row_block = lambda x: pl.BlockSpec((tm, x.shape[1]), lambda i: (i, 0))
    return pl.pallas_call(
        _outproj_kernel,
        grid=(m // tm,),
        in_specs=[
            row_block(h), row_block(o_sb), row_block(y_ssm), row_block(o_diff),
            pl.BlockSpec((None,) + w_glu.shape[1:], lambda i: (layer, 0, 0)),
            _layer_vec_spec(layer, SSM_WIDTH),
            pl.BlockSpec((None, k, d), lambda i: (layer, 0, 0), pipeline_mode=pl.Buffered(1)),
            _layer_vec_spec(layer, d),
        ],
        out_specs=row_block(h),
        out_shape=jax.ShapeDtypeStruct((m, d), F32),
        scratch_shapes=[pltpu.VMEM((tm, k), BF16)],
        compiler_params=_params(1),
        name="outproj",
    )(h, o_sb, y_ssm, o_diff, w_glu, _as_rows(b_glu), w_out_bf, _as_rows(g))


def _trunk(x, ffn1_pre_g, ffn1_w_gate, ffn1_w_up, ffn1_w_down, ffn1_post_g, mix_pre_g, w_in, ssm_a_re, ssm_a_im,
           ssm_log_dt, ssm_b_re, ssm_b_im, ssm_c_re, ssm_c_im, ssm_d, ssm_w_glu, ssm_b_glu, diff_lq1, diff_lk1,
           diff_lq2, diff_lk2, diff_subln_g, rel_bias, w_out, mix_post_g, ffn2_pre_g, ffn2_w_gate, ffn2_w_up,
           ffn2_w_down, ffn2_post_g, *, tm, tf, tm_out):
    batch, seq, d = x.shape
    m = batch * seq
    depth = w_in.shape[0]
    h = x.reshape(m, d)
    bias_tiles = _t5_bias_tiles(rel_bias)
    ssm_mats = _ssm_prep(ssm_a_re, ssm_a_im, ssm_log_dt, ssm_b_re, ssm_b_im, ssm_c_re, ssm_c_im)
    w_out_bf = _cast_bf16(w_out, rows=512)
    for l in range(depth):
        h = _ffn(h, l, ffn1_pre_g, ffn1_w_gate, ffn1_w_up, ffn1_w_down, ffn1_post_g, tm=tm, tf=tf)

        proj, ssm_u = _inproj(h, l, mix_pre_g, w_in, tm=tm)
        proj = proj.reshape(batch, seq, PROJ_COLS)

        o_sb = _sb_attention(proj, batch=batch, seq=seq)

        y_ssm = _ssm(ssm_u, l, ssm_mats, ssm_d, n_chunks=seq // SSM_CHUNK)

        lambda_init = 0.8 - 0.6 * math.exp(-0.3 * l)
        o_diff = _diff_attention(proj, bias_tiles, l, diff_lq1, diff_lk1, diff_lq2, diff_lk2, diff_subln_g,
                                 batch=batch, seq=seq, lambda_init=lambda_init)

        h = _outproj(h, o_sb.reshape(m, -1), y_ssm, o_diff.reshape(m, -1), l, ssm_w_glu, ssm_b_glu, w_out_bf,
                     mix_post_g, tm=tm_out)

        h = _ffn(h, l, ffn2_pre_g, ffn2_w_gate, ffn2_w_up, ffn2_w_down, ffn2_post_g, tm=tm, tf=tf)
    return h.reshape(batch, seq, d)


def kernel(x, ffn1_pre_g, ffn1_w_gate, ffn1_w_up, ffn1_w_down, ffn1_post_g, mix_pre_g, w_in, ssm_a_re, ssm_a_im, ssm_log_dt, ssm_b_re, ssm_b_im, ssm_c_re, ssm_c_im, ssm_d, ssm_w_glu, ssm_b_glu, diff_lq1, diff_lk1, diff_lq2, diff_lk2, diff_subln_g, rel_bias, w_out, mix_post_g, ffn2_pre_g, ffn2_w_gate, ffn2_w_up, ffn2_w_down, ffn2_post_g):
    n_chunks = x.shape[1] // SSM_CHUNK
    assert x.shape[1] % ATT_TILE == 0 and n_chunks <= 1 << SSM_SCAN_ROWS and n_chunks & (n_chunks - 1) == 0
    return _trunk(x, ffn1_pre_g, ffn1_w_gate, ffn1_w_up, ffn1_w_down, ffn1_post_g, mix_pre_g, w_in, ssm_a_re,
                  ssm_a_im, ssm_log_dt, ssm_b_re, ssm_b_im, ssm_c_re, ssm_c_im, ssm_d, ssm_w_glu, ssm_b_glu,
                  diff_lq1, diff_lk1, diff_lq2, diff_lk2, diff_subln_g, rel_bias, w_out, mix_post_g, ffn2_pre_g,
                  ffn2_w_gate, ffn2_w_up, ffn2_w_down, ffn2_post_g, tm=1024, tf=512, tm_out=512)
```

```python
import functools
import math

import jax
import jax.numpy as jnp
from jax import lax
from jax.experimental import pallas as pl
from jax.experimental.pallas import tpu as pltpu

F32 = jnp.float32
BF16 = jnp.bfloat16

HEAD_DIM = 128
SB_HEADS = 6
DIFF_HEADS = 6
DIFF_QK_DIM = HEAD_DIM // 2
SSM_GROUPS = 32
SSM_CH_PER_GROUP = 16
SSM_STATE = 64
SSM_WIDTH = SSM_GROUPS * SSM_CH_PER_GROUP
NUM_BUCKETS = 32
MAX_DISTANCE = 128
NORM_EPS = 1e-6
FFN_RESIDUAL_WEIGHT = 0.5
LOG2E = 1.0 / math.log(2.0)

SB_Q_BLK, SB_K_BLK, SB_V_BLK = 0, 6, 12
DF_Q_BLK, DF_K_BLK, DF_V_BLK = 18, 24, 30
PROJ_COLS = 36 * HEAD_DIM
SSM_COL0 = 3 * SB_HEADS * HEAD_DIM
PROJ_STEP_COLS = SB_HEADS * HEAD_DIM
SB_Q_STEP, DF_Q_STEP = SB_Q_BLK // SB_HEADS, DF_Q_BLK // DIFF_HEADS
SB_LOGIT_SCALE2 = LOG2E / math.sqrt(HEAD_DIM)
DF_LOGIT_SCALE2 = LOG2E / math.sqrt(DIFF_QK_DIM)

V7X_VMEM_LIMIT = 60 * 1024 * 1024
FFN_SUB = 256
ROW_CHUNK = 128
ATT_TILE = 256
SB_NEAR_TILES = 2
SB_SATURATED = 151.0
SSM_CHUNK = 16
SSM_PREP_GROUPS = 8
SSM_SCAN_ROWS = 8
NEG_INF = float(jnp.finfo(jnp.float32).min)


def _params(n_grid, vmem=V7X_VMEM_LIMIT):
    return pltpu.CompilerParams(dimension_semantics=("arbitrary",) * n_grid, vmem_limit_bytes=vmem)


def _rms(x, g):
    ms = jnp.mean(x * x, axis=-1, keepdims=True)
    return x * lax.rsqrt(ms + NORM_EPS) * g


def _for_row_chunks(n_rows, chunk, body):
    def step(i, carry):
        body(pl.multiple_of(i * chunk, chunk))
        return carry
    lax.fori_loop(0, n_rows // chunk, step, 0)


def _layer_vec_spec(layer, n):
    return pl.BlockSpec((None, 1, n), lambda *_: (layer, 0, 0))


def _as_rows(p):
    return p.reshape(p.shape[0], 1, p.shape[1])


def _ffn_kernel(x_ref, pre_g_ref, wg_ref, wu_ref, wd_ref, post_g_ref, o_ref, xn_ref, a_ref, *, n_f, n_col_chunks):
    f = pl.program_id(1)
    tm, d = x_ref.shape

    @pl.when(f == 0)
    def _():
        def norm_rows(r0):
            rows = pl.ds(r0, ROW_CHUNK)
            xn_ref[rows, :] = _rms(x_ref[rows, :], pre_g_ref[...]).astype(BF16)
            o_ref[rows, :] = jnp.zeros((ROW_CHUNK, d), F32)
        _for_row_chunks(tm, ROW_CHUNK, norm_rows)

    xn = xn_ref[...]
    tf = wg_ref.shape[1]
    for hf in range(tf // FFN_SUB):
        sub = slice(hf * FFN_SUB, (hf + 1) * FFN_SUB)
        g = jnp.dot(xn, wg_ref[:, sub].astype(BF16), preferred_element_type=F32)
        u = jnp.dot(xn, wu_ref[:, sub].astype(BF16), preferred_element_type=F32)
        a_ref[:, sub] = (g * jax.nn.sigmoid(g) * u).astype(BF16)
    a = a_ref[...]
    cw = d // n_col_chunks
    for c in range(n_col_chunks):
        cols = slice(c * cw, (c + 1) * cw)
        o_ref[:, cols] += jnp.dot(a, wd_ref[:, cols].astype(BF16), preferred_element_type=F32)

    @pl.when(f == n_f - 1)
    def _():
        def finish_rows(r0):
            rows = pl.ds(r0, ROW_CHUNK)
            o_ref[rows, :] = x_ref[rows, :] + FFN_RESIDUAL_WEIGHT * _rms(o_ref[rows, :], post_g_ref[...])
        _for_row_chunks(tm, ROW_CHUNK, finish_rows)


def _ffn(h, layer, pre_g, w_gate, w_up, w_down, post_g, *, tm, tf):
    m, d = h.shape
    d_ff = w_gate.shape[2]
    n_f = d_ff // tf
    kern = functools.partial(_ffn_kernel, n_f=n_f, n_col_chunks=4)
    return pl.pallas_call(
        kern,
        grid=(m // tm, n_f),
        in_specs=[
            pl.BlockSpec((tm, d), lambda i, f: (i, 0)),
            _layer_vec_spec(layer, d),
            pl.BlockSpec((None, d, tf), lambda i, f: (layer, 0, f)),
            pl.BlockSpec((None, d, tf), lambda i, f: (layer, 0, f)),
            pl.BlockSpec((None, tf, d), lambda i, f: (layer, f, 0)),
            _layer_vec_spec(layer, d),
        ],
        out_specs=pl.BlockSpec((tm, d), lambda i, f: (i, 0), pipeline_mode=pl.Buffered(1)),
        out_shape=jax.ShapeDtypeStruct((m, d), F32),
        scratch_shapes=[pltpu.VMEM((tm, d), BF16), pltpu.VMEM((tm, tf), BF16)],
        compiler_params=_params(2),
        name="ffn",
    )(h, _as_rows(pre_g), w_gate, w_up, w_down, _as_rows(post_g))


def _inproj_kernel(x_ref, g_ref, w_ref, proj_ref, u_ref, xn_ref, *, n_head_steps):
    j = pl.program_id(1)
    tm = x_ref.shape[0]

    @pl.when(j == 0)
    def _():
        def norm_rows(r0):
            rows = pl.ds(r0, ROW_CHUNK)
            xn_ref[rows, :] = _rms(x_ref[rows, :], g_ref[...]).astype(BF16)
        _for_row_chunks(tm, ROW_CHUNK, norm_rows)

    @pl.when(j < n_head_steps)
    def _():
        scale = jnp.where(j == SB_Q_STEP, SB_LOGIT_SCALE2, jnp.where(j == DF_Q_STEP, DF_LOGIT_SCALE2, 1.0))
        y = jnp.dot(xn_ref[...], w_ref[0].astype(BF16), preferred_element_type=F32)
        proj_ref[...] = (y * scale).astype(proj_ref.dtype)

    @pl.when(j == n_head_steps)
    def _():
        u_ref[...] = jnp.dot(xn_ref[...], w_ref[0, :, 0:SSM_WIDTH].astype(BF16), preferred_element_type=F32)


def _inproj(h, layer, g, w_in, *, tm):
    m, d = h.shape
    n_head_steps = PROJ_COLS // PROJ_STEP_COLS
    ssm_step0 = SSM_COL0 // PROJ_STEP_COLS

    def w_col(i, j):
        head_blk = j * (PROJ_STEP_COLS // HEAD_DIM) + jnp.where(j >= ssm_step0, SSM_WIDTH // HEAD_DIM, 0)
        return (layer, 0, jnp.where(j < n_head_steps, head_blk, SSM_COL0 // HEAD_DIM) * HEAD_DIM)

    kern = functools.partial(_inproj_kernel, n_head_steps=n_head_steps)
    return pl.pallas_call(
        kern,
        grid=(m // tm, n_head_steps + 1),
        in_specs=[
            pl.BlockSpec((tm, d), lambda i, j: (i, 0)),
            _layer_vec_spec(layer, d),
            pl.BlockSpec((pl.Element(1), pl.Element(d), pl.Element(PROJ_STEP_COLS)), w_col),
        ],
        out_specs=[
            pl.BlockSpec((tm, PROJ_STEP_COLS), lambda i, j: (i, jnp.minimum(j, n_head_steps - 1))),
            pl.BlockSpec((tm, SSM_WIDTH), lambda i, j: (i, 0)),
        ],
        out_shape=[jax.ShapeDtypeStruct((m, PROJ_COLS), BF16), jax.ShapeDtypeStruct((m, SSM_WIDTH), F32)],
        scratch_shapes=[pltpu.VMEM((tm, d), BF16)],
        compiler_params=_params(2),
        name="inproj",
    )(h, _as_rows(g), w_in)


def _split2_bf16(x):
    hi = x.astype(BF16)
    lo = (x - hi.astype(F32)).astype(BF16)
    return hi, lo


def _mask_first_tile(x, keep):
    t = ATT_TILE
    head = jnp.where(keep, x[0:t], 0.0)
    return head if x.shape[0] == t else jnp.concatenate([head, x[t:]], axis=0)


def _sb_tile_front(q, k, suffix_ones, first_tile_mask):
    z = lax.dot_general(q, k, (((1,), (1,)), ((), ())), preferred_element_type=F32)
    pos = jnp.maximum(z, 0.0)
    neg = jnp.minimum(z, 0.0)
    l2 = jnp.log2(1.0 + jnp.exp2(neg - pos))
    sp = pos + l2
    sp_in = sp if first_tile_mask is None else _mask_first_tile(sp, first_tile_mask)
    hi, lo = _split2_bf16(sp_in)
    suffix = jnp.dot(jnp.concatenate([hi, lo], axis=1), suffix_ones, preferred_element_type=F32)
    return (neg - l2) - suffix, jnp.sum(sp_in, axis=1, keepdims=True)


def _sb_tile_back(log_w, later_rows, v, first_tile_mask):
    w = jnp.exp2(log_w - later_rows)
    if first_tile_mask is not None:
        w = _mask_first_tile(w, first_tile_mask)
    return jnp.dot(w.astype(BF16), v, preferred_element_type=F32)


def _sb_tile(q, k, v, later_rows, suffix_ones, first_tile_mask):
    log_w, row_sum = _sb_tile_front(q, k, suffix_ones, first_tile_mask)
    return _sb_tile_back(log_w, later_rows, v, first_tile_mask), row_sum


def _sb_kernel(q_ref, k_ref, v_ref, o_ref, acc_ref, later_ref):
    t = ATT_TILE
    seq = q_ref.shape[0]
    n_tiles = seq // t
    row = lax.broadcasted_iota(jnp.int32, (t, t), 0)
    col = lax.broadcasted_iota(jnp.int32, (t, t), 1)
    strictly_lower = row > col
    ones = jnp.where(strictly_lower, 1.0, 0.0).astype(BF16)
    suffix_ones = jnp.concatenate([ones, ones], axis=0)
    acc = [None] * n_tiles
    later = [None] * n_tiles

    def front(j):
        r0 = j * t
        r1 = min(r0 + SB_NEAR_TILES * t, seq)
        return _sb_tile_front(q_ref[r0:r1, :], k_ref[r0:r0 + t, :], suffix_ones, strictly_lower)

    fronts = {n_tiles - 1: front(n_tiles - 1)}
    for j in range(n_tiles - 1, -1, -1):
        r0 = j * t
        r1 = min(r0 + SB_NEAR_TILES * t, seq)
        if j > 0:
            fronts[j - 1] = front(j - 1)
        log_w, row_sum = fronts.pop(j)
        later_rows = jnp.concatenate([jnp.zeros((t, 1), F32)] + later[j + 1:j + SB_NEAR_TILES], axis=0)
        pv = _sb_tile_back(log_w, later_rows, v_ref[r0:r0 + t, :], strictly_lower)
        for i in range(j, r1 // t):
            rows = slice((i - j) * t, (i - j + 1) * t)
            acc[i] = pv[rows] if i == j else acc[i] + pv[rows]
            later[i] = row_sum[rows] if i == j else later[i] + row_sum[rows]

    for i in range(n_tiles):
        acc_ref[i * t:(i + 1) * t, :] = acc[i]
        later_ref[i * t:(i + 1) * t, :] = later[i]

    def far_tiles_of(i, carry):
        rows = pl.ds(pl.multiple_of(i * t, t), t)
        n_far = i + 1 - SB_NEAR_TILES

        def far_tile(n, inner):
            k0 = pl.multiple_of((n_far - 1 - n) * t, t)
            pv, row_sum = _sb_tile(q_ref[rows, :], k_ref[pl.ds(k0, t), :], v_ref[pl.ds(k0, t), :], later_ref[rows, :],
                                   suffix_ones, None)
            acc_ref[rows, :] += pv
            later_ref[rows, :] += row_sum
            return inner

        lax.fori_loop(0, jnp.where(jnp.min(later_ref[rows, :]) < SB_SATURATED, n_far, 0), far_tile, 0)
        return carry

    if n_tiles > SB_NEAR_TILES:
        any_unsaturated = jnp.min(functools.reduce(jnp.minimum, later[SB_NEAR_TILES:])) < SB_SATURATED
        lax.fori_loop(SB_NEAR_TILES, jnp.where(any_unsaturated, n_tiles, SB_NEAR_TILES), far_tiles_of, 0)
    o_ref[...] = acc_ref[...].astype(o_ref.dtype)


def _sb_attention(proj, *, batch, seq):
    spec = lambda blk: pl.BlockSpec((None, seq, HEAD_DIM), lambda b, h: (b, 0, blk + h))
    return pl.pallas_call(
        _sb_kernel,
        grid=(batch, SB_HEADS),
        in_specs=[spec(SB_Q_BLK), spec(SB_K_BLK), spec(SB_V_BLK)],
        out_specs=spec(0),
        out_shape=jax.ShapeDtypeStruct((batch, seq, SB_HEADS * HEAD_DIM), BF16),
        scratch_shapes=[pltpu.VMEM((seq, HEAD_DIM), F32), pltpu.VMEM((seq, 1), F32)],
        compiler_params=_params(2),
        name="sb_attention",
    )(proj, proj, proj)


def _t5_bias_kernel(rel_ref, o_ref):
    t = ATT_TILE
    h = pl.program_id(0)
    row = lax.broadcasted_iota(jnp.int32, (t, t), 0)
    col = lax.broadcasted_iota(jnp.int32, (t, t), 1)
    max_exact = NUM_BUCKETS // 2
    for d in range(3):
        n = jnp.maximum(d * t + row - col, 0)
        nf = jnp.maximum(n, 1).astype(F32)
        large = max_exact + (jnp.log(nf / max_exact) / math.log(MAX_DISTANCE / max_exact)
                             * (NUM_BUCKETS - max_exact)).astype(jnp.int32)
        large = jnp.minimum(large, NUM_BUCKETS - 1)
        bucket = jnp.where(n < max_exact, n, large)
        bias = jnp.zeros((t, t), F32)
        for b in range(NUM_BUCKETS):
            bias = jnp.where(bucket == b, rel_ref[b, h], bias)
        o_ref[d] = bias * LOG2E


def _t5_bias_tiles(rel_bias):
    t = ATT_TILE
    return pl.pallas_call(
        _t5_bias_kernel,
        grid=(DIFF_HEADS,),
        in_specs=[pl.BlockSpec(memory_space=pltpu.SMEM)],
        out_specs=pl.BlockSpec((None, 3, t, t), lambda h: (h, 0, 0, 0)),
        out_shape=jax.ShapeDtypeStruct((DIFF_HEADS, 3, t, t), F32),
        compiler_params=_params(1),
        name="t5_bias_tiles",
    )(rel_bias)


def _diff_kernel(lq1_ref, lk1_ref, lq2_ref, lk2_ref, subln_g_ref, bias_ref, q_ref, k_ref, v_ref, o_ref,
                 *, lambda_init):
    t = ATT_TILE
    seq = q_ref.shape[0]
    n_tiles = seq // t
    lane = lax.broadcasted_iota(jnp.int32, (t, HEAD_DIM), 1)
    row = lax.broadcasted_iota(jnp.int32, (t, t), 0)
    col = lax.broadcasted_iota(jnp.int32, (t, t), 1)
    causal = col <= row
    lam = (jnp.exp(jnp.sum(lq1_ref[...] * lk1_ref[...], axis=1, keepdims=True))
           - jnp.exp(jnp.sum(lq2_ref[...] * lk2_ref[...], axis=1, keepdims=True)) + lambda_init)

    def logits_of(i):
        q = q_ref[i * t:(i + 1) * t, :]
        zero = jnp.zeros_like(q)
        q_maps = (jnp.where(lane < DIFF_QK_DIM, q, zero), jnp.where(lane >= DIFF_QK_DIM, q, zero))
        k = k_ref[0:(i + 1) * t, :]
        return [lax.dot_general(qm, k, (((1,), (1,)), ((), ())), preferred_element_type=F32) for qm in q_maps]

    raws = {0: logits_of(0)}
    for i in range(n_tiles):
        n_keys = (i + 1) * t
        if i + 1 < n_tiles:
            raws[i + 1] = logits_of(i + 1)
        exps, sums = [], []
        for raw in raws.pop(i):
            parts = []
            for j in range(i + 1):
                s_j = raw[:, j * t:(j + 1) * t] + bias_ref[min(i - j, 2)]
                parts.append(jnp.where(causal, s_j, NEG_INF) if j == i else s_j)
            s = parts[0] if i == 0 else jnp.concatenate(parts, axis=1)
            e = jnp.exp2(s - jnp.max(s, axis=1, keepdims=True))
            exps.append(e)
            sums.append(jnp.sum(e, axis=1, keepdims=True))
        p = exps[0] - exps[1] * (lam * sums[0] / sums[1])
        o = jnp.dot(p.astype(BF16), v_ref[0:n_keys, :], preferred_element_type=F32) * (1.0 / sums[0])
        o_ref[i * t:(i + 1) * t, :] = (_rms(o, subln_g_ref[...]) * (1.0 - lambda_init)).astype(o_ref.dtype)


def _diff_attention(proj, bias_tiles, layer, lq1, lk1, lq2, lk2, subln_g, *, batch, seq, lambda_init):
    t = ATT_TILE
    vec = lambda n: _layer_vec_spec(layer, n)
    spec = lambda blk: pl.BlockSpec((None, seq, HEAD_DIM), lambda b, h: (b, 0, blk + h))
    kern = functools.partial(_diff_kernel, lambda_init=lambda_init)
    return pl.pallas_call(
        kern,
        grid=(batch, DIFF_HEADS),
        in_specs=[
            vec(DIFF_QK_DIM), vec(DIFF_QK_DIM), vec(DIFF_QK_DIM), vec(DIFF_QK_DIM), vec(HEAD_DIM),
            pl.BlockSpec((None, 3, t, t), lambda b, h: (h, 0, 0, 0)),
            spec(DF_Q_BLK), spec(DF_K_BLK), spec(DF_V_BLK),
        ],
        out_specs=spec(0),
        out_shape=jax.ShapeDtypeStruct((batch, seq, DIFF_HEADS * HEAD_DIM), BF16),
        compiler_params=_params(2),
        name="diff_attention",
    )(_as_rows(lq1), _as_rows(lk1), _as_rows(lq2), _as_rows(lk2), _as_rows(subln_g), bias_tiles, proj, proj, proj)


def _cpow(n, ar, ai, dt):
    mag = jnp.exp(n * (ar * dt))
    ang = n * (ai * dt)
    return mag * jnp.cos(ang), mag * jnp.sin(ang)


def _repeat_rows(x, reps):
    n, w = x.shape
    return jnp.broadcast_to(x[:, None, :], (n, reps, w)).reshape(n * reps, w)


def _ssm_prep_group(ar_ref, ai_ref, ldt_ref, bt_re_ref, bt_im_ref, ct_re_ref, ct_im_ref,
                    m_intra_ref, m_in_ref, m_out_ref, p1_ref, p2_ref):
    t = SSM_CHUNK
    cpg = SSM_CH_PER_GROUP
    ar, ai = ar_ref[...], ai_ref[...]
    dt = jnp.exp(ldt_ref[...])
    lb_re, lb_im = _cpow(1.0, ar, ai, dt)
    den = ar * ar + ai * ai
    f_re = ((lb_re - 1.0) * ar + lb_im * ai) / den
    f_im = (lb_im * ar - (lb_re - 1.0) * ai) / den
    bt_re, bt_im = bt_re_ref[...], bt_im_ref[...]
    bb_re = f_re * bt_re - f_im * bt_im
    bb_im = f_re * bt_im + f_im * bt_re
    ct_re, ct_im = ct_re_ref[...], ct_im_ref[...]
    pos = lax.broadcasted_iota(jnp.int32, (t, 2 * SSM_STATE), 0).astype(F32)
    lane_is_re = lax.broadcasted_iota(jnp.int32, (t * cpg, 2 * SSM_STATE), 1) < SSM_STATE

    def times_power(n, x_re, x_im, im_sign):
        e_re, e_im = _cpow(n, ar, ai, dt)
        e_re, e_im = _repeat_rows(e_re, cpg), _repeat_rows(e_im, cpg)
        return jnp.where(lane_is_re, e_re * x_re - e_im * x_im, im_sign * (e_re * x_im + e_im * x_re))

    m_in_ref[...] = times_power((t - 1.0) - pos, bb_re, bb_im, 1.0).astype(BF16)
    m_out_ref[...] = times_power(pos + 1.0, ct_re, ct_im, -1.0).astype(BF16)
    p_packed = times_power(-pos, bb_re, bb_im, -1.0)
    q_packed = times_power(pos, ct_re, ct_im, 1.0)
    k_full = lax.dot_general(p_packed, q_packed, (((1,), (1,)), ((), ())), preferred_element_type=F32,
                             precision=lax.Precision.HIGHEST)
    shift = cpg.bit_length() - 1
    row_k = lax.broadcasted_iota(jnp.int32, k_full.shape, 0)
    lane_k = lax.broadcasted_iota(jnp.int32, k_full.shape, 1)
    causal = jnp.right_shift(lane_k, shift) >= jnp.right_shift(row_k, shift)
    m_intra_ref[...] = jnp.where(causal, k_full, 0.0).astype(BF16)

    row_j = lax.broadcasted_iota(jnp.int32, (SSM_SCAN_ROWS, 2 * SSM_STATE), 0)
    lane_j = lax.broadcasted_iota(jnp.int32, (SSM_SCAN_ROWS, 2 * SSM_STATE), 1)
    pr, pi = _cpow(jnp.left_shift(t, row_j).astype(F32), ar, ai, dt)
    p1_ref[...] = pr
    p2_ref[...] = jnp.where(lane_j < SSM_STATE, -pi, pi)


def _ssm_prep_kernel(*refs):
    for g in range(refs[0].shape[0]):
        _ssm_prep_group(*[r.at[g] for r in refs])


def _ssm_prep(a_re, a_im, log_dt, b_re, b_im, c_re, c_im):
    nl, g, p = a_re.shape
    t, cpg = SSM_CHUNK, SSM_CH_PER_GROUP
    tc = t * cpg
    twice = lambda x: jnp.concatenate([x, x], axis=-1)
    rows = [twice(x)[:, :, None, :] for x in (a_re, a_im, jnp.broadcast_to(log_dt[:, :, None], a_re.shape))]
    bt = [jnp.tile(twice(jnp.swapaxes(b, 2, 3)), (1, 1, t, 1)) for b in (b_re, b_im)]
    ct = [jnp.tile(twice(c), (1, 1, t, 1)) for c in (c_re, c_im)]
    gps = SSM_PREP_GROUPS
    blk = lambda shape: pl.BlockSpec((None, gps) + shape, lambda l, gi: (l, gi, 0, 0))
    return pl.pallas_call(
        _ssm_prep_kernel,
        grid=(nl, g // gps),
        in_specs=[blk((1, 2 * p))] * 3 + [blk((tc, 2 * p))] * 4,
        out_specs=[blk((tc, tc)), blk((tc, 2 * p)), blk((tc, 2 * p)),
                   blk((SSM_SCAN_ROWS, 2 * p)), blk((SSM_SCAN_ROWS, 2 * p))],
        out_shape=[jax.ShapeDtypeStruct((nl, g, tc, tc), BF16),
                   jax.ShapeDtypeStruct((nl, g, tc, 2 * p), BF16),
                   jax.ShapeDtypeStruct((nl, g, tc, 2 * p), BF16),
                   jax.ShapeDtypeStruct((nl, g, SSM_SCAN_ROWS, 2 * p), F32),
                   jax.ShapeDtypeStruct((nl, g, SSM_SCAN_ROWS, 2 * p), F32)],
        compiler_params=_params(2),
        name="ssm_prep",
    )(*rows, *bt, *ct)


def _gelu_tanh(x):
    return 0.5 * x * (1.0 + jnp.tanh(math.sqrt(2.0 / math.pi) * (x + 0.044715 * (x * x * x))))


def _ssm_kernel(u_ref, m_intra_ref, m_in_ref, m_out_ref, p1_ref, p2_ref, d_ref, o_ref, *, n_chunks):
    t, cpg = SSM_CHUNK, SSM_CH_PER_GROUP
    n_rows = u_ref.shape[0] // t
    groups = u_ref.shape[1] // cpg
    by_pos = [u_ref[pl.ds(tau, n_rows, stride=t), :] for tau in range(t)]
    row = lax.broadcasted_iota(jnp.int32, (n_rows, 2 * SSM_STATE), 0)
    chunk = jnp.bitwise_and(row, n_chunks - 1)

    def chunks_back(x, n):
        return jnp.where(chunk >= n, pltpu.roll(x, n, axis=0), 0.0)

    def regroup(g):
        return jnp.concatenate([x[:, g * cpg:(g + 1) * cpg] for x in by_pos], axis=1)

    ys = []
    us = {0: regroup(0)}
    for g in range(groups):
        if g + 1 < groups:
            us[g + 1] = regroup(g + 1)
        u = us.pop(g)
        ub = u.astype(BF16)
        y = jnp.dot(ub, m_intra_ref[g], preferred_element_type=F32)
        s_in = jnp.dot(ub, m_in_ref[g], preferred_element_type=F32)
        x = chunks_back(s_in, 1)
        for j in range((n_chunks - 1).bit_length()):
            prev = chunks_back(x, 1 << j)
            x = x + p1_ref[g, j:j + 1, :] * prev + p2_ref[g, j:j + 1, :] * pltpu.roll(prev, SSM_STATE, axis=1)
        y_state = lax.dot_general(x.astype(BF16), m_out_ref[g], (((1,), (1,)), ((), ())),
                                  preferred_element_type=F32)
        ys.append(_gelu_tanh(y + y_state + d_ref[g] * u))
    for tau in range(t):
        lanes = slice(tau * cpg, (tau + 1) * cpg)
        o_ref[pl.ds(tau, n_rows, stride=t), :] = jnp.concatenate([y[:, lanes] for y in ys], axis=1)


def _ssm(u, layer, mats, d_skip, *, n_chunks):
    m, width = u.shape
    m_intra, m_in, m_out, p1, p2 = mats
    gps = 128 // SSM_CH_PER_GROUP
    lg = lambda x: pl.BlockSpec((None, gps) + x.shape[2:], lambda gi: (layer, gi, 0, 0))
    d_tiled = jnp.tile(d_skip, (1, 1, SSM_CHUNK))[:, :, None, :]
    kern = functools.partial(_ssm_kernel, n_chunks=n_chunks)
    return pl.pallas_call(
        kern,
        grid=(width // 128,),
        in_specs=[pl.BlockSpec((m, 128), lambda gi: (0, gi)),
                  lg(m_intra), lg(m_in), lg(m_out), lg(p1), lg(p2), lg(d_tiled)],
        out_specs=pl.BlockSpec((m, 128), lambda gi: (0, gi)),
        out_shape=jax.ShapeDtypeStruct(u.shape, F32),
        compiler_params=_params(1),
        name="ssm_chunks",
    )(u, m_intra, m_in, m_out, p1, p2, d_tiled)


def _cast_kernel(x_ref, o_ref):
    o_ref[...] = x_ref[...].astype(o_ref.dtype)


def _cast_bf16(w, *, rows):
    nl, r, c = w.shape
    return pl.pallas_call(
        _cast_kernel,
        grid=(nl, r // rows),
        in_specs=[pl.BlockSpec((None, rows, c), lambda l, i: (l, i, 0))],
        out_specs=pl.BlockSpec((None, rows, c), lambda l, i: (l, i, 0)),
        out_shape=jax.ShapeDtypeStruct(w.shape, BF16),
        compiler_params=_params(2),
        name="cast_bf16",
    )(w)


def _outproj_kernel(h_ref, sb_ref, ssm_ref, df_ref, wglu_ref, bglu_ref, w_ref, g_ref, o_ref, mixed_ref):
    w_sb, w_ssm = sb_ref.shape[1], ssm_ref.shape[1]
    y = ssm_ref[...]
    gate = jax.nn.sigmoid(jnp.dot(y.astype(BF16), wglu_ref[...].astype(BF16), preferred_element_type=F32)
                          + bglu_ref[...])
    mixed_ref[:, 0:w_sb] = sb_ref[...]
    mixed_ref[:, w_sb:w_sb + w_ssm] = (y * gate).astype(BF16)
    mixed_ref[:, w_sb + w_ssm:] = df_ref[...]
    out = jnp.dot(mixed_ref[...], w_ref[...], preferred_element_type=F32)
    o_ref[...] = h_ref[...] + _rms(out, g_ref[...])


def _outproj(h, o_sb, y_ssm, o_diff, layer, w_glu, b_glu, w_out_bf, g, *, tm):
    m, d = h.shape
    k = w_out_bf.shape[1]
    row_block = lambda x: pl.BlockSpec((tm, x.shape[1]), lambda i: (i, 0))
    return pl.pallas_call(
        _outproj_kernel,
        grid=(m // tm,),
        in_specs=[
            row_block(h), row_block(o_sb), row_block(y_ssm), row_block(o_diff),
            pl.BlockSpec((None,) + w_glu.shape[1:], lambda i: (layer, 0, 0)),
            _layer_vec_spec(layer, SSM_WIDTH),
            pl.BlockSpec((None, k, d), lambda i: (layer, 0, 0), pipeline_mode=pl.Buffered(1)),
            _layer_vec_spec(layer, d),
        ],
        out_specs=row_block(h),
        out_shape=jax.ShapeDtypeStruct((m, d), F32),
        scratch_shapes=[pltpu.VMEM((tm, k), BF16)],
        compiler_params=_params(1),
        name="outproj",
    )(h, o_sb, y_ssm, o_diff, w_glu, _as_rows(b_glu), w_out_bf, _as_rows(g))


def _trunk(x, ffn1_pre_g, ffn1_w_gate, ffn1_w_up, ffn1_w_down, ffn1_post_g, mix_pre_g, w_in, ssm_a_re, ssm_a_im,
           ssm_log_dt, ssm_b_re, ssm_b_im, ssm_c_re, ssm_c_im, ssm_d, ssm_w_glu, ssm_b_glu, diff_lq1, diff_lk1,
           diff_lq2, diff_lk2, diff_subln_g, rel_bias, w_out, mix_post_g, ffn2_pre_g, ffn2_w_gate, ffn2_w_up,
           ffn2_w_down, ffn2_post_g, *, tm, tf, tm_out):
    batch, seq, d = x.shape
    m = batch * seq
    depth = w_in.shape[0]
    h = x.reshape(m, d)
    bias_tiles = _t5_bias_tiles(rel_bias)
    ssm_mats = _ssm_prep(ssm_a_re, ssm_a_im, ssm_log_dt, ssm_b_re, ssm_b_im, ssm_c_re, ssm_c_im)
    w_out_bf = _cast_bf16(w_out, rows=512)
    for l in range(depth):
        h = _ffn(h, l, ffn1_pre_g, ffn1_w_gate, ffn1_w_up, ffn1_w_down, ffn1_post_g, tm=tm, tf=tf)

        proj, ssm_u = _inproj(h, l, mix_pre_g, w_in, tm=tm)
        proj = proj.reshape(batch, seq, PROJ_COLS)

        o_sb = _sb_attention(proj, batch=batch, seq=seq)

        y_ssm = _ssm(ssm_u, l, ssm_mats, ssm_d, n_chunks=seq // SSM_CHUNK)

        lambda_init = 0.8 - 0.6 * math.exp(-0.3 * l)
        o_diff = _diff_attention(proj, bias_tiles, l, diff_lq1, diff_lk1, diff_lq2, diff_lk2, diff_subln_g,
                                 batch=batch, seq=seq, lambda_init=lambda_init)

        h = _outproj(h, o_sb.reshape(m, -1), y_ssm, o_diff.reshape(m, -1), l, ssm_w_glu, ssm_b_glu, w_out_bf,
                     mix_post_g, tm=tm_out)

        h = _ffn(h, l, ffn2_pre_g, ffn2_w_gate, ffn2_w_up, ffn2_w_down, ffn2_post_g, tm=tm, tf=tf)
    return h.reshape(batch, seq, d)


def kernel(x, ffn1_pre_g, ffn1_w_gate, ffn1_w_up, ffn1_w_down, ffn1_post_g, mix_pre_g, w_in, ssm_a_re, ssm_a_im, ssm_log_dt, ssm_b_re, ssm_b_im, ssm_c_re, ssm_c_im, ssm_d, ssm_w_glu, ssm_b_glu, diff_lq1, diff_lk1, diff_lq2, diff_lk2, diff_subln_g, rel_bias, w_out, mix_post_g, ffn2_pre_g, ffn2_w_gate, ffn2_w_up, ffn2_w_down, ffn2_post_g):
    n_chunks = x.shape[1] // SSM_CHUNK
    assert x.shape[1] % ATT_TILE == 0 and n_chunks <= 1 << SSM_SCAN_ROWS and n_chunks & (n_chunks - 1) == 0
    return _trunk(x, ffn1_pre_g, ffn1_w_gate, ffn1_w_up, ffn1_w_down, ffn1_post_g, mix_pre_g, w_in, ssm_a_re,
                  ssm_a_im, ssm_log_dt, ssm_b_re, ssm_b_im, ssm_c_re, ssm_c_im, ssm_d, ssm_w_glu, ssm_b_glu,
                  diff_lq1, diff_lk1, diff_lq2, diff_lk2, diff_subln_g, rel_bias, w_out, mix_post_g, ffn2_pre_g,
                  ffn2_w_gate, ffn2_w_up, ffn2_w_down, ffn2_post_g, tm=1024, tf=512, tm_out=512)
```

```python
import functools
import math

import jax
import jax.numpy as jnp
from jax import lax
from jax.experimental import pallas as pl
from jax.experimental.pallas import tpu as pltpu

F32 = jnp.float32
BF16 = jnp.bfloat16

HEAD_DIM = 128
SB_HEADS = 6
DIFF_HEADS = 6
DIFF_QK_DIM = HEAD_DIM // 2
SSM_GROUPS = 32
SSM_CH_PER_GROUP = 16
SSM_STATE = 64
SSM_WIDTH = SSM_GROUPS * SSM_CH_PER_GROUP
NUM_BUCKETS = 32
MAX_DISTANCE = 128
NORM_EPS = 1e-6
FFN_RESIDUAL_WEIGHT = 0.5
LOG2E = 1.0 / math.log(2.0)

SB_Q_BLK, SB_K_BLK, SB_V_BLK = 0, 6, 12
DF_Q_BLK, DF_K_BLK, DF_V_BLK = 18, 24, 30
PROJ_COLS = 36 * HEAD_DIM
SSM_COL0 = 3 * SB_HEADS * HEAD_DIM
PROJ_STEP_COLS = SB_HEADS * HEAD_DIM
SB_Q_STEP, DF_Q_STEP = SB_Q_BLK // SB_HEADS, DF_Q_BLK // DIFF_HEADS
SB_LOGIT_SCALE2 = LOG2E / math.sqrt(HEAD_DIM)
DF_LOGIT_SCALE2 = LOG2E / math.sqrt(DIFF_QK_DIM)

V7X_VMEM_LIMIT = 60 * 1024 * 1024
FFN_SUB = 256
ROW_CHUNK = 128
ATT_TILE = 256
SB_NEAR_TILES = 2
SB_SATURATED = 151.0
SSM_CHUNK = 16
SSM_PREP_GROUPS = 8
SSM_SCAN_ROWS = 8
NEG_INF = float(jnp.finfo(jnp.float32).min)


def _params(n_grid, vmem=V7X_VMEM_LIMIT):
    return pltpu.CompilerParams(dimension_semantics=("arbitrary",) * n_grid, vmem_limit_bytes=vmem)


def _rms(x, g):
    ms = jnp.mean(x * x, axis=-1, keepdims=True)
    return x * lax.rsqrt(ms + NORM_EPS) * g


def _for_row_chunks(n_rows, chunk, body):
    def step(i, carry):
        body(pl.multiple_of(i * chunk, chunk))
        return carry
    lax.fori_loop(0, n_rows // chunk, step, 0)


def _layer_vec_spec(layer, n):
    return pl.BlockSpec((None, 1, n), lambda *_: (layer, 0, 0))


def _as_rows(p):
    return p.reshape(p.shape[0], 1, p.shape[1])


def _ffn_kernel(x_ref, pre_g_ref, wg_ref, wu_ref, wd_ref, post_g_ref, o_ref, xn_ref, a_ref, *, n_f, n_col_chunks):
    f = pl.program_id(1)
    tm, d = x_ref.shape

    @pl.when(f == 0)
    def _():
        def norm_rows(r0):
            rows = pl.ds(r0, ROW_CHUNK)
            xn_ref[rows, :] = _rms(x_ref[rows, :], pre_g_ref[...]).astype(BF16)
            o_ref[rows, :] = jnp.zeros((ROW_CHUNK, d), F32)
        _for_row_chunks(tm, ROW_CHUNK, norm_rows)

    xn = xn_ref[...]
    tf = wg_ref.shape[1]
    for hf in range(tf // FFN_SUB):
        sub = slice(hf * FFN_SUB, (hf + 1) * FFN_SUB)
        g = jnp.dot(xn, wg_ref[:, sub].astype(BF16), preferred_element_type=F32)
        u = jnp.dot(xn, wu_ref[:, sub].astype(BF16), preferred_element_type=F32)
        a_ref[:, sub] = (g * jax.nn.sigmoid(g) * u).astype(BF16)
    a = a_ref[...]
    cw = d // n_col_chunks
    for c in range(n_col_chunks):
        cols = slice(c * cw, (c + 1) * cw)
        o_ref[:, cols] += jnp.dot(a, wd_ref[:, cols].astype(BF16), preferred_element_type=F32)

    @pl.when(f == n_f - 1)
    def _():
        def finish_rows(r0):
            rows = pl.ds(r0, ROW_CHUNK)
            o_ref[rows, :] = x_ref[rows, :] + FFN_RESIDUAL_WEIGHT * _rms(o_ref[rows, :], post_g_ref[...])
        _for_row_chunks(tm, ROW_CHUNK, finish_rows)


def _ffn(h, layer, pre_g, w_gate, w_up, w_down, post_g, *, tm, tf):
    m, d = h.shape
    d_ff = w_gate.shape[2]
    n_f = d_ff // tf
    kern = functools.partial(_ffn_kernel, n_f=n_f, n_col_chunks=4)
    return pl.pallas_call(
        kern,
        grid=(m // tm, n_f),
        in_specs=[
            pl.BlockSpec((tm, d), lambda i, f: (i, 0)),
            _layer_vec_spec(layer, d),
            pl.BlockSpec((None, d, tf), lambda i, f: (layer, 0, f)),
            pl.BlockSpec((None, d, tf), lambda i, f: (layer, 0, f)),
            pl.BlockSpec((None, tf, d), lambda i, f: (layer, f, 0)),
            _layer_vec_spec(layer, d),
        ],
        out_specs=pl.BlockSpec((tm, d), lambda i, f: (i, 0), pipeline_mode=pl.Buffered(1)),
        out_shape=jax.ShapeDtypeStruct((m, d), F32),
        scratch_shapes=[pltpu.VMEM((tm, d), BF16), pltpu.VMEM((tm, tf), BF16)],
        compiler_params=_params(2),
        name="ffn",
    )(h, _as_rows(pre_g), w_gate, w_up, w_down, _as_rows(post_g))


def _inproj_kernel(x_ref, g_ref, w_ref, proj_ref, u_ref, xn_ref, *, n_head_steps):
    j = pl.program_id(1)
    tm = x_ref.shape[0]

    @pl.when(j == 0)
    def _():
        def norm_rows(r0):
            rows = pl.ds(r0, ROW_CHUNK)
            xn_ref[rows, :] = _rms(x_ref[rows, :], g_ref[...]).astype(BF16)
        _for_row_chunks(tm, ROW_CHUNK, norm_rows)

    @pl.when(j < n_head_steps)
    def _():
        scale = jnp.where(j == SB_Q_STEP, SB_LOGIT_SCALE2, jnp.where(j == DF_Q_STEP, DF_LOGIT_SCALE2, 1.0))
        y = jnp.dot(xn_ref[...], w_ref[0].astype(BF16), preferred_element_type=F32)
        proj_ref[...] = (y * scale).astype(proj_ref.dtype)

    @pl.when(j == n_head_steps)
    def _():
        u_ref[...] = jnp.dot(xn_ref[...], w_ref[0, :, 0:SSM_WIDTH].astype(BF16), preferred_element_type=F32)


def _inproj(h, layer, g, w_in, *, tm):
    m, d = h.shape
    n_head_steps = PROJ_COLS // PROJ_STEP_COLS
    ssm_step0 = SSM_COL0 // PROJ_STEP_COLS

    def w_col(i, j):
        head_blk = j * (PROJ_STEP_COLS // HEAD_DIM) + jnp.where(j >= ssm_step0, SSM_WIDTH // HEAD_DIM, 0)
        return (layer, 0, jnp.where(j < n_head_steps, head_blk, SSM_COL0 // HEAD_DIM) * HEAD_DIM)

    kern = functools.partial(_inproj_kernel, n_head_steps=n_head_steps)
    return pl.pallas_call(
        kern,
        grid=(m // tm, n_head_steps + 1),
        in_specs=[
            pl.BlockSpec((tm, d), lambda i, j: (i, 0)),
            _layer_vec_spec(layer, d),
            pl.BlockSpec((pl.Element(1), pl.Element(d), pl.Element(PROJ_STEP_COLS)), w_col),
        ],
        out_specs=[
            pl.BlockSpec((tm, PROJ_STEP_COLS), lambda i, j: (i, jnp.minimum(j, n_head_steps - 1))),
            pl.BlockSpec((tm, SSM_WIDTH), lambda i, j: (i, 0)),
        ],
        out_shape=[jax.ShapeDtypeStruct((m, PROJ_COLS), BF16), jax.ShapeDtypeStruct((m, SSM_WIDTH), F32)],
        scratch_shapes=[pltpu.VMEM((tm, d), BF16)],
        compiler_params=_params(2),
        name="inproj",
    )(h, _as_rows(g), w_in)


def _split2_bf16(x):
    hi = x.astype(BF16)
    lo = (x - hi.astype(F32)).astype(BF16)
    return hi, lo


def _mask_first_tile(x, keep):
    t = ATT_TILE
    head = jnp.where(keep, x[0:t], 0.0)
    return head if x.shape[0] == t else jnp.concatenate([head, x[t:]], axis=0)


def _sb_tile_front(q, k, suffix_ones, first_tile_mask):
    z = lax.dot_general(q, k, (((1,), (1,)), ((), ())), preferred_element_type=F32)
    pos = jnp.maximum(z, 0.0)
    neg = jnp.minimum(z, 0.0)
    l2 = jnp.log2(1.0 + jnp.exp2(neg - pos))
    sp = pos + l2
    sp_in = sp if first_tile_mask is None else _mask_first_tile(sp, first_tile_mask)
    hi, lo = _split2_bf16(sp_in)
    suffix = jnp.dot(jnp.concatenate([hi, lo], axis=1), suffix_ones, preferred_element_type=F32)
    return (neg - l2) - suffix, jnp.sum(sp_in, axis=1, keepdims=True)


def _sb_tile_back(log_w, later_rows, v, first_tile_mask):
    w = jnp.exp2(log_w - later_rows)
    if first_tile_mask is not None:
        w = _mask_first_tile(w, first_tile_mask)
    return jnp.dot(w.astype(BF16), v, preferred_element_type=F32)


def _sb_tile(q, k, v, later_rows, suffix_ones, first_tile_mask):
    log_w, row_sum = _sb_tile_front(q, k, suffix_ones, first_tile_mask)
    return _sb_tile_back(log_w, later_rows, v, first_tile_mask), row_sum


def _sb_kernel(q_ref, k_ref, v_ref, o_ref, acc_ref, later_ref):
    t = ATT_TILE
    seq = q_ref.shape[0]
    n_tiles = seq // t
    row = lax.broadcasted_iota(jnp.int32, (t, t), 0)
    col = lax.broadcasted_iota(jnp.int32, (t, t), 1)
    strictly_lower = row > col
    ones = jnp.where(strictly_lower, 1.0, 0.0).astype(BF16)
    suffix_ones = jnp.concatenate([ones, ones], axis=0)
    acc = [None] * n_tiles
    later = [None] * n_tiles

    def front(j):
        r0 = j * t
        r1 = min(r0 + SB_NEAR_TILES * t, seq)
        return _sb_tile_front(q_ref[r0:r1, :], k_ref[r0:r0 + t, :], suffix_ones, strictly_lower)

    fronts = {n_tiles - 1: front(n_tiles - 1)}
    for j in range(n_tiles - 1, -1, -1):
        r0 = j * t
        r1 = min(r0 + SB_NEAR_TILES * t, seq)
        if j > 0:
            fronts[j - 1] = front(j - 1)
        log_w, row_sum = fronts.pop(j)
        later_rows = jnp.concatenate([jnp.zeros((t, 1), F32)] + later[j + 1:j + SB_NEAR_TILES], axis=0)
        pv = _sb_tile_back(log_w, later_rows, v_ref[r0:r0 + t, :], strictly_lower)
        for i in range(j, r1 // t):
            rows = slice((i - j) * t, (i - j + 1) * t)
            acc[i] = pv[rows] if i == j else acc[i] + pv[rows]
            later[i] = row_sum[rows] if i == j else later[i] + row_sum[rows]

    for i in range(n_tiles):
        acc_ref[i * t:(i + 1) * t, :] = acc[i]
        later_ref[i * t:(i + 1) * t, :] = later[i]

    def far_tiles_of(i, carry):
        rows = pl.ds(pl.multiple_of(i * t, t), t)
        n_far = i + 1 - SB_NEAR_TILES

        def far_tile(n, inner):
            k0 = pl.multiple_of((n_far - 1 - n) * t, t)
            pv, row_sum = _sb_tile(q_ref[rows, :], k_ref[pl.ds(k0, t), :], v_ref[pl.ds(k0, t), :], later_ref[rows, :],
                                   suffix_ones, None)
            acc_ref[rows, :] += pv
            later_ref[rows, :] += row_sum
            return inner

        lax.fori_loop(0, jnp.where(jnp.min(later_ref[rows, :]) < SB_SATURATED, n_far, 0), far_tile, 0)
        return carry

    if n_tiles > SB_NEAR_TILES:
        any_unsaturated = jnp.min(functools.reduce(jnp.minimum, later[SB_NEAR_TILES:])) < SB_SATURATED
        lax.fori_loop(SB_NEAR_TILES, jnp.where(any_unsaturated, n_tiles, SB_NEAR_TILES), far_tiles_of, 0)
    o_ref[...] = acc_ref[...].astype(o_ref.dtype)


def _sb_attention(proj, *, batch, seq):
    spec = lambda blk: pl.BlockSpec((None, seq, HEAD_DIM), lambda b, h: (b, 0, blk + h))
    return pl.pallas_call(
        _sb_kernel,
        grid=(batch, SB_HEADS),
        in_specs=[spec(SB_Q_BLK), spec(SB_K_BLK), spec(SB_V_BLK)],
        out_specs=spec(0),
        out_shape=jax.ShapeDtypeStruct((batch, seq, SB_HEADS * HEAD_DIM), BF16),
        scratch_shapes=[pltpu.VMEM((seq, HEAD_DIM), F32), pltpu.VMEM((seq, 1), F32)],
        compiler_params=_params(2),
        name="sb_attention",
    )(proj, proj, proj)


def _t5_bias_kernel(rel_ref, o_ref):
    t = ATT_TILE
    h = pl.program_id(0)
    row = lax.broadcasted_iota(jnp.int32, (t, t), 0)
    col = lax.broadcasted_iota(jnp.int32, (t, t), 1)
    max_exact = NUM_BUCKETS // 2
    for d in range(3):
        n = jnp.maximum(d * t + row - col, 0)
        nf = jnp.maximum(n, 1).astype(F32)
        large = max_exact + (jnp.log(nf / max_exact) / math.log(MAX_DISTANCE / max_exact)
                             * (NUM_BUCKETS - max_exact)).astype(jnp.int32)
        large = jnp.minimum(large, NUM_BUCKETS - 1)
        bucket = jnp.where(n < max_exact, n, large)
        bias = jnp.zeros((t, t), F32)
        for b in range(NUM_BUCKETS):
            bias = jnp.where(bucket == b, rel_ref[b, h], bias)
        o_ref[d] = bias * LOG2E


def _t5_bias_tiles(rel_bias):
    t = ATT_TILE
    return pl.pallas_call(
        _t5_bias_kernel,
        grid=(DIFF_HEADS,),
        in_specs=[pl.BlockSpec(memory_space=pltpu.SMEM)],
        out_specs=pl.BlockSpec((None, 3, t, t), lambda h: (h, 0, 0, 0)),
        out_shape=jax.ShapeDtypeStruct((DIFF_HEADS, 3, t, t), F32),
        compiler_params=_params(1),
        name="t5_bias_tiles",
    )(rel_bias)


def _diff_kernel(lq1_ref, lk1_ref, lq2_ref, lk2_ref, subln_g_ref, bias_ref, q_ref, k_ref, v_ref, o_ref,
                 *, lambda_init):
    t = ATT_TILE
    seq = q_ref.shape[0]
    n_tiles = seq // t
    lane = lax.broadcasted_iota(jnp.int32, (t, HEAD_DIM), 1)
    row = lax.broadcasted_iota(jnp.int32, (t, t), 0)
    col = lax.broadcasted_iota(jnp.int32, (t, t), 1)
    causal = col <= row
    lam = (jnp.exp(jnp.sum(lq1_ref[...] * lk1_ref[...], axis=1, keepdims=True))
           - jnp.exp(jnp.sum(lq2_ref[...] * lk2_ref[...], axis=1, keepdims=True)) + lambda_init)

    def logits_of(i):
        q = q_ref[i * t:(i + 1) * t, :]
        zero = jnp.zeros_like(q)
        q_maps = (jnp.where(lane < DIFF_QK_DIM, q, zero), jnp.where(lane >= DIFF_QK_DIM, q, zero))
        k = k_ref[0:(i + 1) * t, :]
        return [lax.dot_general(qm, k, (((1,), (1,)), ((), ())), preferred_element_type=F32) for qm in q_maps]

    raws = {0: logits_of(0)}
    for i in range(n_tiles):
        n_keys = (i + 1) * t
        if i + 1 < n_tiles:
            raws[i + 1] = logits_of(i + 1)
        exps, sums = [], []
        for raw in raws.pop(i):
            parts = []
            for j in range(i + 1):
                s_j = raw[:, j * t:(j + 1) * t] + bias_ref[min(i - j, 2)]
                parts.append(jnp.where(causal, s_j, NEG_INF) if j == i else s_j)
            s = parts[0] if i == 0 else jnp.concatenate(parts, axis=1)
            e = jnp.exp2(s - jnp.max(s, axis=1, keepdims=True))
            exps.append(e)
            sums.append(jnp.sum(e, axis=1, keepdims=True))
        p = exps[0] - exps[1] * (lam * sums[0] / sums[1])
        o = jnp.dot(p.astype(BF16), v_ref[0:n_keys, :], preferred_element_type=F32) * (1.0 / sums[0])
        o_ref[i * t:(i + 1) * t, :] = (_rms(o, subln_g_ref[...]) * (1.0 - lambda_init)).astype(o_ref.dtype)


def _diff_attention(proj, bias_tiles, layer, lq1, lk1, lq2, lk2, subln_g, *, batch, seq, lambda_init):
    t = ATT_TILE
    vec = lambda n: _layer_vec_spec(layer, n)
    spec = lambda blk: pl.BlockSpec((None, seq, HEAD_DIM), lambda b, h: (b, 0, blk + h))
    kern = functools.partial(_diff_kernel, lambda_init=lambda_init)
    return pl.pallas_call(
        kern,
        grid=(batch, DIFF_HEADS),
        in_specs=[
            vec(DIFF_QK_DIM), vec(DIFF_QK_DIM), vec(DIFF_QK_DIM), vec(DIFF_QK_DIM), vec(HEAD_DIM),
            pl.BlockSpec((None, 3, t, t), lambda b, h: (h, 0, 0, 0)),
            spec(DF_Q_BLK), spec(DF_K_BLK), spec(DF_V_BLK),
        ],
        out_specs=spec(0),
        out_shape=jax.ShapeDtypeStruct((batch, seq, DIFF_HEADS * HEAD_DIM), BF16),
        compiler_params=_params(2),
        name="diff_attention",
    )(_as_rows(lq1), _as_rows(lk1), _as_rows(lq2), _as_rows(lk2), _as_rows(subln_g), bias_tiles, proj, proj, proj)


def _cpow(n, ar, ai, dt):
    mag = jnp.exp(n * (ar * dt))
    ang = n * (ai * dt)
    return mag * jnp.cos(ang), mag * jnp.sin(ang)


def _repeat_rows(x, reps):
    n, w = x.shape
    return jnp.broadcast_to(x[:, None, :], (n, reps, w)).reshape(n * reps, w)


def _ssm_prep_group(ar_ref, ai_ref, ldt_ref, bt_re_ref, bt_im_ref, ct_re_ref, ct_im_ref,
                    m_intra_ref, m_in_ref, m_out_ref, p1_ref, p2_ref):
    t = SSM_CHUNK
    cpg = SSM_CH_PER_GROUP
    ar, ai = ar_ref[...], ai_ref[...]
    dt = jnp.exp(ldt_ref[...])
    lb_re, lb_im = _cpow(1.0, ar, ai, dt)
    den = ar * ar + ai * ai
    f_re = ((lb_re - 1.0) * ar + lb_im * ai) / den
    f_im = (lb_im * ar - (lb_re - 1.0) * ai) / den
    bt_re, bt_im = bt_re_ref[...], bt_im_ref[...]
    bb_re = f_re * bt_re - f_im * bt_im
    bb_im = f_re * bt_im + f_im * bt_re
    ct_re, ct_im = ct_re_ref[...], ct_im_ref[...]
    pos = lax.broadcasted_iota(jnp.int32, (t, 2 * SSM_STATE), 0).astype(F32)
    lane_is_re = lax.broadcasted_iota(jnp.int32, (t * cpg, 2 * SSM_STATE), 1) < SSM_STATE

    def times_power(n, x_re, x_im, im_sign):
        e_re, e_im = _cpow(n, ar, ai, dt)
        e_re, e_im = _repeat_rows(e_re, cpg), _repeat_rows(e_im, cpg)
        return jnp.where(lane_is_re, e_re * x_re - e_im * x_im, im_sign * (e_re * x_im + e_im * x_re))

    m_in_ref[...] = times_power((t - 1.0) - pos, bb_re, bb_im, 1.0).astype(BF16)
    m_out_ref[...] = times_power(pos + 1.0, ct_re, ct_im, -1.0).astype(BF16)
    p_packed = times_power(-pos, bb_re, bb_im, -1.0)
    q_packed = times_power(pos, ct_re, ct_im, 1.0)
    k_full = lax.dot_general(p_packed, q_packed, (((1,), (1,)), ((), ())), preferred_element_type=F32,
                             precision=lax.Precision.HIGHEST)
    shift = cpg.bit_length() - 1
    row_k = lax.broadcasted_iota(jnp.int32, k_full.shape, 0)
    lane_k = lax.broadcasted_iota(jnp.int32, k_full.shape, 1)
    causal = jnp.right_shift(lane_k, shift) >= jnp.right_shift(row_k, shift)
    m_intra_ref[...] = jnp.where(causal, k_full, 0.0).astype(BF16)

    row_j = lax.broadcasted_iota(jnp.int32, (SSM_SCAN_ROWS, 2 * SSM_STATE), 0)
    lane_j = lax.broadcasted_iota(jnp.int32, (SSM_SCAN_ROWS, 2 * SSM_STATE), 1)
    pr, pi = _cpow(jnp.left_shift(t, row_j).astype(F32), ar, ai, dt)
    p1_ref[...] = pr
    p2_ref[...] = jnp.where(lane_j < SSM_STATE, -pi, pi)


def _ssm_prep_kernel(*refs):
    for g in range(refs[0].shape[0]):
        _ssm_prep_group(*[r.at[g] for r in refs])


def _ssm_prep(a_re, a_im, log_dt, b_re, b_im, c_re, c_im):
    nl, g, p = a_re.shape
    t, cpg = SSM_CHUNK, SSM_CH_PER_GROUP
    tc = t * cpg
    twice = lambda x: jnp.concatenate([x, x], axis=-1)
    rows = [twice(x)[:, :, None, :] for x in (a_re, a_im, jnp.broadcast_to(log_dt[:, :, None], a_re.shape))]
    bt = [jnp.tile(twice(jnp.swapaxes(b, 2, 3)), (1, 1, t, 1)) for b in (b_re, b_im)]
    ct = [jnp.tile(twice(c), (1, 1, t, 1)) for c in (c_re, c_im)]
    gps = SSM_PREP_GROUPS
    blk = lambda shape: pl.BlockSpec((None, gps) + shape, lambda l, gi: (l, gi, 0, 0))
    return pl.pallas_call(
        _ssm_prep_kernel,
        grid=(nl, g // gps),
        in_specs=[blk((1, 2 * p))] * 3 + [blk((tc, 2 * p))] * 4,
        out_specs=[blk((tc, tc)), blk((tc, 2 * p)), blk((tc, 2 * p)),
                   blk((SSM_SCAN_ROWS, 2 * p)), blk((SSM_SCAN_ROWS, 2 * p))],
        out_shape=[jax.ShapeDtypeStruct((nl, g, tc, tc), BF16),
                   jax.ShapeDtypeStruct((nl, g, tc, 2 * p), BF16),
                   jax.ShapeDtypeStruct((nl, g, tc, 2 * p), BF16),
                   jax.ShapeDtypeStruct((nl, g, SSM_SCAN_ROWS, 2 * p), F32),
                   jax.ShapeDtypeStruct((nl, g, SSM_SCAN_ROWS, 2 * p), F32)],
        compiler_params=_params(2),
        name="ssm_prep",
    )(*rows, *bt, *ct)


def _gelu_tanh(x):
    return 0.5 * x * (1.0 + jnp.tanh(math.sqrt(2.0 / math.pi) * (x + 0.044715 * (x * x * x))))


def _transpose_lane_pieces(vs):
    p = len(vs)
    width = vs[0].shape[1]
    w = width // p
    piece = lax.broadcasted_iota(jnp.int32, vs[0].shape, 1) // w
    vs = list(vs)
    s = p // 2
    while s >= 1:
        keep = jnp.bitwise_and(piece, s) == 0
        for i in range(p):
            if i & s:
                continue
            lo, hi = vs[i], vs[i + s]
            vs[i] = jnp.where(keep, lo, pltpu.roll(hi, s * w, axis=1))
            vs[i + s] = jnp.where(keep, pltpu.roll(lo, width - s * w, axis=1), hi)
        s //= 2
    return vs


def _ssm_kernel(u_ref, m_intra_ref, m_in_ref, m_out_ref, p1_ref, p2_ref, d_ref, o_ref, *, n_chunks):
    t, cpg = SSM_CHUNK, SSM_CH_PER_GROUP
    n_rows = u_ref.shape[0] // t
    groups = u_ref.shape[1] // cpg
    by_pos = [u_ref[pl.ds(tau, n_rows, stride=t), :] for tau in range(t)]
    by_group = [_transpose_lane_pieces(by_pos[h:h + groups]) for h in range(0, t, groups)]
    row = lax.broadcasted_iota(jnp.int32, (n_rows, 2 * SSM_STATE), 0)
    chunk = jnp.bitwise_and(row, n_chunks - 1)

    def chunks_back(x, n):
        return jnp.where(chunk >= n, pltpu.roll(x, n, axis=0), 0.0)

    def regroup(g):
        return jnp.concatenate([half[g] for half in by_group], axis=1)

    ys = []
    us = {0: regroup(0)}
    for g in range(groups):
        if g + 1 < groups:
            us[g + 1] = regroup(g + 1)
        u = us.pop(g)
        ub = u.astype(BF16)
        y = jnp.dot(ub, m_intra_ref[g], preferred_element_type=F32)
        s_in = jnp.dot(ub, m_in_ref[g], preferred_element_type=F32)
        x = chunks_back(s_in, 1)
        for j in range((n_chunks - 1).bit_length()):
            prev = chunks_back(x, 1 << j)
            x = x + p1_ref[g, j:j + 1, :] * prev + p2_ref[g, j:j + 1, :] * pltpu.roll(prev, SSM_STATE, axis=1)
        y_state = lax.dot_general(x.astype(BF16), m_out_ref[g], (((1,), (1,)), ((), ())),
                                  preferred_element_type=F32)
        ys.append(_gelu_tanh(y + y_state + d_ref[g] * u))
    lanes_per_half = groups * cpg
    for h in range(t // groups):
        half = _transpose_lane_pieces([y[:, h * lanes_per_half:(h + 1) * lanes_per_half] for y in ys])
        for i, rows_of_pos in enumerate(half):
            o_ref[pl.ds(h * groups + i, n_rows, stride=t), :] = rows_of_pos


def _ssm(u, layer, mats, d_skip, *, n_chunks):
    m, width = u.shape
    m_intra, m_in, m_out, p1, p2 = mats
    gps = 128 // SSM_CH_PER_GROUP
    lg = lambda x: pl.BlockSpec((None, gps) + x.shape[2:], lambda gi: (layer, gi, 0, 0))
    d_tiled = jnp.tile(d_skip, (1, 1, SSM_CHUNK))[:, :, None, :]
    kern = functools.partial(_ssm_kernel, n_chunks=n_chunks)
    return pl.pallas_call(
        kern,
        grid=(width // 128,),
        in_specs=[pl.BlockSpec((m, 128), lambda gi: (0, gi)),
                  lg(m_intra), lg(m_in), lg(m_out), lg(p1), lg(p2), lg(d_tiled)],
        out_specs=pl.BlockSpec((m, 128), lambda gi: (0, gi)),
        out_shape=jax.ShapeDtypeStruct(u.shape, F32),
        compiler_params=_params(1),
        name="ssm_chunks",
    )(u, m_intra, m_in, m_out, p1, p2, d_tiled)


def _cast_kernel(x_ref, o_ref):
    o_ref[...] = x_ref[...].astype(o_ref.dtype)


def _cast_bf16(w, *, rows):
    nl, r, c = w.shape
    return pl.pallas_call(
        _cast_kernel,
        grid=(nl, r // rows),
        in_specs=[pl.BlockSpec((None, rows, c), lambda l, i: (l, i, 0))],
        out_specs=pl.BlockSpec((None, rows, c), lambda l, i: (l, i, 0)),
        out_shape=jax.ShapeDtypeStruct(w.shape, BF16),
        compiler_params=_params(2),
        name="cast_bf16",
    )(w)


def _outproj_kernel(h_ref, sb_ref, ssm_ref, df_ref, wglu_ref, bglu_ref, w_ref, g_ref, o_ref, mixed_ref):
    w_sb, w_ssm = sb_ref.shape[1], ssm_ref.shape[1]
    y = ssm_ref[...]
    gate = jax.nn.sigmoid(jnp.dot(y.astype(BF16), wglu_ref[...].astype(BF16), preferred_element_type=F32)
                          + bglu_ref[...])
    mixed_ref[:, 0:w_sb] = sb_ref[...]
    mixed_ref[:, w_sb:w_sb + w_ssm] = (y * gate).astype(BF16)
    mixed_ref[:, w_sb + w_ssm:] = df_ref[...]
    out = jnp.dot(mixed_ref[...], w_ref[...], preferred_element_type=F32)
    o_ref[...] = h_ref[...] + _rms(out, g_ref[...])


def _outproj(h, o_sb, y_ssm, o_diff, layer, w_glu, b_glu, w_out_bf, g, *, tm):
    m, d = h.shape
    k = w_out_bf.shape[1]
    row_block = lambda x: pl.BlockSpec((tm, x.shape[1]), lambda i: (i, 0))
    return pl.pallas_call(
        _outproj_kernel,
        grid=(m // tm,),
        in_specs=[
            row_block(h), row_block(o_sb), row_block(y_ssm), row_block(o_diff),
            pl.BlockSpec((None,) + w_glu.shape[1:], lambda i: (layer, 0, 0)),
            _layer_vec_spec(layer, SSM_WIDTH),
            pl.BlockSpec((None, k, d), lambda i: (layer, 0, 0), pipeline_mode=pl.Buffered(1)),
            _layer_vec_spec(layer, d),
        ],
        out_specs=row_block(h),
        out_shape=jax.ShapeDtypeStruct((m, d), F32),
        scratch_shapes=[pltpu.VMEM((tm, k), BF16)],
        compiler_params=_params(1),
        name="outproj",
    )(h, o_sb, y_ssm, o_diff, w_glu, _as_rows(b_glu), w_out_bf, _as_rows(g))


def _trunk(x, ffn1_pre_g, ffn1_w_gate, ffn1_w_up, ffn1_w_down, ffn1_post_g, mix_pre_g, w_in, ssm_a_re, ssm_a_im,
           ssm_log_dt, ssm_b_re, ssm_b_im, ssm_c_re, ssm_c_im, ssm_d, ssm_w_glu, ssm_b_glu, diff_lq1, diff_lk1,
           diff_lq2, diff_lk2, diff_subln_g, rel_bias, w_out, mix_post_g, ffn2_pre_g, ffn2_w_gate, ffn2_w_up,
           ffn2_w_down, ffn2_post_g, *, tm, tf, tm_out):
    batch, seq, d = x.shape
    m = batch * seq
    depth = w_in.shape[0]
    h = x.reshape(m, d)
    bias_tiles = _t5_bias_tiles(rel_bias)
    ssm_mats = _ssm_prep(ssm_a_re, ssm_a_im, ssm_log_dt, ssm_b_re, ssm_b_im, ssm_c_re, ssm_c_im)
    w_out_bf = _cast_bf16(w_out, rows=512)
    for l in range(depth):
        h = _ffn(h, l, ffn1_pre_g, ffn1_w_gate, ffn1_w_up, ffn1_w_down, ffn1_post_g, tm=tm, tf=tf)

        proj, ssm_u = _inproj(h, l, mix_pre_g, w_in, tm=tm)
        proj = proj.reshape(batch, seq, PROJ_COLS)

        o_sb = _sb_attention(proj, batch=batch, seq=seq)

        y_ssm = _ssm(ssm_u, l, ssm_mats, ssm_d, n_chunks=seq // SSM_CHUNK)

        lambda_init = 0.8 - 0.6 * math.exp(-0.3 * l)
        o_diff = _diff_attention(proj, bias_tiles, l, diff_lq1, diff_lk1, diff_lq2, diff_lk2, diff_subln_g,
                                 batch=batch, seq=seq, lambda_init=lambda_init)

        h = _outproj(h, o_sb.reshape(m, -1), y_ssm, o_diff.reshape(m, -1), l, ssm_w_glu, ssm_b_glu, w_out_bf,
                     mix_post_g, tm=tm_out)

        h = _ffn(h, l, ffn2_pre_g, ffn2_w_gate, ffn2_w_up, ffn2_w_down, ffn2_post_g, tm=tm, tf=tf)
    return h.reshape(batch, seq, d)


def kernel(x, ffn1_pre_g, ffn1_w_gate, ffn1_w_up, ffn1_w_down, ffn1_post_g, mix_pre_g, w_in, ssm_a_re, ssm_a_im, ssm_log_dt, ssm_b_re, ssm_b_im, ssm_c_re, ssm_c_im, ssm_d, ssm_w_glu, ssm_b_glu, diff_lq1, diff_lk1, diff_lq2, diff_lk2, diff_subln_g, rel_bias, w_out, mix_post_g, ffn2_pre_g, ffn2_w_gate, ffn2_w_up, ffn2_w_down, ffn2_post_g):
    n_chunks = x.shape[1] // SSM_CHUNK
    assert x.shape[1] % ATT_TILE == 0 and n_chunks <= 1 << SSM_SCAN_ROWS and n_chunks & (n_chunks - 1) == 0
    return _trunk(x, ffn1_pre_g, ffn1_w_gate, ffn1_w_up, ffn1_w_down, ffn1_post_g, mix_pre_g, w_in, ssm_a_re,
                  ssm_a_im, ssm_log_dt, ssm_b_re, ssm_b_im, ssm_c_re, ssm_c_im, ssm_d, ssm_w_glu, ssm_b_glu,
                  diff_lq1, diff_lk1, diff_lq2, diff_lk2, diff_subln_g, rel_bias, w_out, mix_post_g, ffn2_pre_g,
                  ffn2_w_gate, ffn2_w_up, ffn2_w_down, ffn2_post_g, tm=1024, tf=512, tm_out=512)
```

```python
import functools
import math

import jax
import jax.numpy as jnp
from jax import lax
from jax.experimental import pallas as pl
from jax.experimental.pallas import tpu as pltpu

F32 = jnp.float32
BF16 = jnp.bfloat16

HEAD_DIM = 128
SB_HEADS = 6
DIFF_HEADS = 6
DIFF_QK_DIM = HEAD_DIM // 2
SSM_GROUPS = 32
SSM_CH_PER_GROUP = 16
SSM_STATE = 64
SSM_WIDTH = SSM_GROUPS * SSM_CH_PER_GROUP
NUM_BUCKETS = 32
MAX_DISTANCE = 128
NORM_EPS = 1e-6
FFN_RESIDUAL_WEIGHT = 0.5
LOG2E = 1.0 / math.log(2.0)

SB_Q_BLK, SB_K_BLK, SB_V_BLK = 0, 6, 12
DF_Q_BLK, DF_K_BLK, DF_V_BLK = 18, 24, 30
PROJ_COLS = 36 * HEAD_DIM
SSM_COL0 = 3 * SB_HEADS * HEAD_DIM
PROJ_STEP_COLS = SB_HEADS * HEAD_DIM
SB_Q_STEP, DF_Q_STEP = SB_Q_BLK // SB_HEADS, DF_Q_BLK // DIFF_HEADS
SB_LOGIT_SCALE2 = LOG2E / math.sqrt(HEAD_DIM)
DF_LOGIT_SCALE2 = LOG2E / math.sqrt(DIFF_QK_DIM)

V7X_VMEM_LIMIT = 60 * 1024 * 1024
V7X_LANES = 128
ROW_TILE = 1024
FFN_TILE = 512
OUTPROJ_ROW_TILE = 512
FFN_SUB = 256
ROW_CHUNK = 128
ATT_TILE = 256
SB_NEAR_TILES = 2
SB_SATURATED = 151.0
SSM_CHUNK = 16
SSM_PREP_GROUPS = 8
SSM_SCAN_ROWS = 8
NEG_INF = float(jnp.finfo(jnp.float32).min)


def _params(n_grid, vmem=V7X_VMEM_LIMIT):
    return pltpu.CompilerParams(dimension_semantics=("arbitrary",) * n_grid, vmem_limit_bytes=vmem)


def _rms(x, g):
    ms = jnp.mean(x * x, axis=-1, keepdims=True)
    return x * lax.rsqrt(ms + NORM_EPS) * g


def _for_row_chunks(n_rows, chunk, body):
    def step(i, carry):
        body(pl.multiple_of(i * chunk, chunk))
        return carry
    lax.fori_loop(0, n_rows // chunk, step, 0)


def _layer_vec_spec(layer, n):
    return pl.BlockSpec((None, 1, n), lambda *_: (layer, 0, 0))


def _as_rows(p):
    return p.reshape(p.shape[0], 1, p.shape[1])


def _ffn_kernel(x_ref, pre_g_ref, wg_ref, wu_ref, wd_ref, post_g_ref, o_ref, xn_ref, a_ref, *, n_f, n_col_chunks):
    f = pl.program_id(1)
    tm, d = x_ref.shape

    @pl.when(f == 0)
    def _():
        def norm_rows(r0):
            rows = pl.ds(r0, ROW_CHUNK)
            xn_ref[rows, :] = _rms(x_ref[rows, :], pre_g_ref[...]).astype(BF16)
            o_ref[rows, :] = jnp.zeros((ROW_CHUNK, d), F32)
        _for_row_chunks(tm, ROW_CHUNK, norm_rows)

    xn = xn_ref[...]
    tf = wg_ref.shape[1]
    for hf in range(tf // FFN_SUB):
        sub = slice(hf * FFN_SUB, (hf + 1) * FFN_SUB)
        g = jnp.dot(xn, wg_ref[:, sub].astype(BF16), preferred_element_type=F32)
        u = jnp.dot(xn, wu_ref[:, sub].astype(BF16), preferred_element_type=F32)
        a_ref[:, sub] = (g * jax.nn.sigmoid(g) * u).astype(BF16)
    a = a_ref[...]
    cw = d // n_col_chunks
    for c in range(n_col_chunks):
        cols = slice(c * cw, (c + 1) * cw)
        o_ref[:, cols] += jnp.dot(a, wd_ref[:, cols].astype(BF16), preferred_element_type=F32)

    @pl.when(f == n_f - 1)
    def _():
        def finish_rows(r0):
            rows = pl.ds(r0, ROW_CHUNK)
            o_ref[rows, :] = x_ref[rows, :] + FFN_RESIDUAL_WEIGHT * _rms(o_ref[rows, :], post_g_ref[...])
        _for_row_chunks(tm, ROW_CHUNK, finish_rows)


def _ffn(h, layer, pre_g, w_gate, w_up, w_down, post_g, *, tm, tf):
    m, d = h.shape
    d_ff = w_gate.shape[2]
    n_f = d_ff // tf
    kern = functools.partial(_ffn_kernel, n_f=n_f, n_col_chunks=4)
    return pl.pallas_call(
        kern,
        grid=(m // tm, n_f),
        in_specs=[
            pl.BlockSpec((tm, d), lambda i, f: (i, 0)),
            _layer_vec_spec(layer, d),
            pl.BlockSpec((None, d, tf), lambda i, f: (layer, 0, f)),
            pl.BlockSpec((None, d, tf), lambda i, f: (layer, 0, f)),
            pl.BlockSpec((None, tf, d), lambda i, f: (layer, f, 0)),
            _layer_vec_spec(layer, d),
        ],
        out_specs=pl.BlockSpec((tm, d), lambda i, f: (i, 0), pipeline_mode=pl.Buffered(1)),
        out_shape=jax.ShapeDtypeStruct((m, d), F32),
        scratch_shapes=[pltpu.VMEM((tm, d), BF16), pltpu.VMEM((tm, tf), BF16)],
        compiler_params=_params(2),
        name="ffn",
    )(h, _as_rows(pre_g), w_gate, w_up, w_down, _as_rows(post_g))


def _inproj_kernel(x_ref, g_ref, w_ref, proj_ref, u_ref, xn_ref, *, n_head_steps):
    j = pl.program_id(1)
    tm = x_ref.shape[0]

    @pl.when(j == 0)
    def _():
        def norm_rows(r0):
            rows = pl.ds(r0, ROW_CHUNK)
            xn_ref[rows, :] = _rms(x_ref[rows, :], g_ref[...]).astype(BF16)
        _for_row_chunks(tm, ROW_CHUNK, norm_rows)

    @pl.when(j < n_head_steps)
    def _():
        scale = jnp.where(j == SB_Q_STEP, SB_LOGIT_SCALE2, jnp.where(j == DF_Q_STEP, DF_LOGIT_SCALE2, 1.0))
        y = jnp.dot(xn_ref[...], w_ref[0].astype(BF16), preferred_element_type=F32)
        proj_ref[...] = (y * scale).astype(proj_ref.dtype)

    @pl.when(j == n_head_steps)
    def _():
        u_ref[...] = jnp.dot(xn_ref[...], w_ref[0, :, 0:SSM_WIDTH].astype(BF16), preferred_element_type=F32)


def _inproj(h, layer, g, w_in, *, tm):
    m, d = h.shape
    n_head_steps = PROJ_COLS // PROJ_STEP_COLS
    ssm_step0 = SSM_COL0 // PROJ_STEP_COLS

    def w_col(i, j):
        head_blk = j * (PROJ_STEP_COLS // HEAD_DIM) + jnp.where(j >= ssm_step0, SSM_WIDTH // HEAD_DIM, 0)
        return (layer, 0, jnp.where(j < n_head_steps, head_blk, SSM_COL0 // HEAD_DIM) * HEAD_DIM)

    kern = functools.partial(_inproj_kernel, n_head_steps=n_head_steps)
    return pl.pallas_call(
        kern,
        grid=(m // tm, n_head_steps + 1),
        in_specs=[
            pl.BlockSpec((tm, d), lambda i, j: (i, 0)),
            _layer_vec_spec(layer, d),
            pl.BlockSpec((pl.Element(1), pl.Element(d), pl.Element(PROJ_STEP_COLS)), w_col),
        ],
        out_specs=[
            pl.BlockSpec((tm, PROJ_STEP_COLS), lambda i, j: (i, jnp.minimum(j, n_head_steps - 1))),
            pl.BlockSpec((tm, SSM_WIDTH), lambda i, j: (i, 0)),
        ],
        out_shape=[jax.ShapeDtypeStruct((m, PROJ_COLS), BF16), jax.ShapeDtypeStruct((m, SSM_WIDTH), F32)],
        scratch_shapes=[pltpu.VMEM((tm, d), BF16)],
        compiler_params=_params(2),
        name="inproj",
    )(h, _as_rows(g), w_in)


def _split2_bf16(x):
    hi = x.astype(BF16)
    lo = (x - hi.astype(F32)).astype(BF16)
    return hi, lo


def _mask_first_tile(x, keep):
    t = ATT_TILE
    head = jnp.where(keep, x[0:t], 0.0)
    return head if x.shape[0] == t else jnp.concatenate([head, x[t:]], axis=0)


def _sb_tile_front(q, k, suffix_ones, first_tile_mask):
    z = lax.dot_general(q, k, (((1,), (1,)), ((), ())), preferred_element_type=F32)
    pos = jnp.maximum(z, 0.0)
    neg = jnp.minimum(z, 0.0)
    l2 = jnp.log2(1.0 + jnp.exp2(neg - pos))
    sp = pos + l2
    sp_in = sp if first_tile_mask is None else _mask_first_tile(sp, first_tile_mask)
    hi, lo = _split2_bf16(sp_in)
    suffix = jnp.dot(jnp.concatenate([hi, lo], axis=1), suffix_ones, preferred_element_type=F32)
    return (neg - l2) - suffix, jnp.sum(sp_in, axis=1, keepdims=True)


def _sb_tile_back(log_w, later_rows, v, first_tile_mask):
    w = jnp.exp2(log_w - later_rows)
    if first_tile_mask is not None:
        w = _mask_first_tile(w, first_tile_mask)
    return jnp.dot(w.astype(BF16), v, preferred_element_type=F32)


def _sb_tile(q, k, v, later_rows, suffix_ones, first_tile_mask):
    log_w, row_sum = _sb_tile_front(q, k, suffix_ones, first_tile_mask)
    return _sb_tile_back(log_w, later_rows, v, first_tile_mask), row_sum


def _sb_kernel(q_ref, k_ref, v_ref, o_ref, acc_ref, later_ref):
    t = ATT_TILE
    seq = q_ref.shape[0]
    n_tiles = seq // t
    row = lax.broadcasted_iota(jnp.int32, (t, t), 0)
    col = lax.broadcasted_iota(jnp.int32, (t, t), 1)
    strictly_lower = row > col
    ones = jnp.where(strictly_lower, 1.0, 0.0).astype(BF16)
    suffix_ones = jnp.concatenate([ones, ones], axis=0)
    acc = [None] * n_tiles
    later = [None] * n_tiles

    def front(j):
        r0 = j * t
        r1 = min(r0 + SB_NEAR_TILES * t, seq)
        return _sb_tile_front(q_ref[r0:r1, :], k_ref[r0:r0 + t, :], suffix_ones, strictly_lower)

    fronts = {n_tiles - 1: front(n_tiles - 1)}
    for j in range(n_tiles - 1, -1, -1):
        r0 = j * t
        r1 = min(r0 + SB_NEAR_TILES * t, seq)
        if j > 0:
            fronts[j - 1] = front(j - 1)
        log_w, row_sum = fronts.pop(j)
        later_rows = jnp.concatenate([jnp.zeros((t, 1), F32)] + later[j + 1:j + SB_NEAR_TILES], axis=0)
        pv = _sb_tile_back(log_w, later_rows, v_ref[r0:r0 + t, :], strictly_lower)
        for i in range(j, r1 // t):
            rows = slice((i - j) * t, (i - j + 1) * t)
            acc[i] = pv[rows] if i == j else acc[i] + pv[rows]
            later[i] = row_sum[rows] if i == j else later[i] + row_sum[rows]

    for i in range(n_tiles):
        acc_ref[i * t:(i + 1) * t, :] = acc[i]
        later_ref[i * t:(i + 1) * t, :] = later[i]

    def far_tiles_of(i, carry):
        rows = pl.ds(pl.multiple_of(i * t, t), t)
        n_far = i + 1 - SB_NEAR_TILES

        def far_tile(n, inner):
            k0 = pl.multiple_of((n_far - 1 - n) * t, t)
            pv, row_sum = _sb_tile(q_ref[rows, :], k_ref[pl.ds(k0, t), :], v_ref[pl.ds(k0, t), :], later_ref[rows, :],
                                   suffix_ones, None)
            acc_ref[rows, :] += pv
            later_ref[rows, :] += row_sum
            return inner

        lax.fori_loop(0, jnp.where(jnp.min(later_ref[rows, :]) < SB_SATURATED, n_far, 0), far_tile, 0)
        return carry

    if n_tiles > SB_NEAR_TILES:
        any_unsaturated = jnp.min(functools.reduce(jnp.minimum, later[SB_NEAR_TILES:])) < SB_SATURATED
        lax.fori_loop(SB_NEAR_TILES, jnp.where(any_unsaturated, n_tiles, SB_NEAR_TILES), far_tiles_of, 0)
    o_ref[...] = acc_ref[...].astype(o_ref.dtype)


def _sb_attention(proj, *, batch, seq):
    spec = lambda blk: pl.BlockSpec((None, seq, HEAD_DIM), lambda b, h: (b, 0, blk + h))
    return pl.pallas_call(
        _sb_kernel,
        grid=(batch, SB_HEADS),
        in_specs=[spec(SB_Q_BLK), spec(SB_K_BLK), spec(SB_V_BLK)],
        out_specs=spec(0),
        out_shape=jax.ShapeDtypeStruct((batch, seq, SB_HEADS * HEAD_DIM), BF16),
        scratch_shapes=[pltpu.VMEM((seq, HEAD_DIM), F32), pltpu.VMEM((seq, 1), F32)],
        compiler_params=_params(2),
        name="sb_attention",
    )(proj, proj, proj)


def _t5_bias_kernel(rel_ref, o_ref):
    t = ATT_TILE
    h = pl.program_id(0)
    row = lax.broadcasted_iota(jnp.int32, (t, t), 0)
    col = lax.broadcasted_iota(jnp.int32, (t, t), 1)
    max_exact = NUM_BUCKETS // 2
    for d in range(3):
        n = jnp.maximum(d * t + row - col, 0)
        nf = jnp.maximum(n, 1).astype(F32)
        large = max_exact + (jnp.log(nf / max_exact) / math.log(MAX_DISTANCE / max_exact)
                             * (NUM_BUCKETS - max_exact)).astype(jnp.int32)
        large = jnp.minimum(large, NUM_BUCKETS - 1)
        bucket = jnp.where(n < max_exact, n, large)
        bias = jnp.zeros((t, t), F32)
        for b in range(NUM_BUCKETS):
            bias = jnp.where(bucket == b, rel_ref[b, h], bias)
        o_ref[d] = bias * LOG2E


def _t5_bias_tiles(rel_bias):
    t = ATT_TILE
    return pl.pallas_call(
        _t5_bias_kernel,
        grid=(DIFF_HEADS,),
        in_specs=[pl.BlockSpec(memory_space=pltpu.SMEM)],
        out_specs=pl.BlockSpec((None, 3, t, t), lambda h: (h, 0, 0, 0)),
        out_shape=jax.ShapeDtypeStruct((DIFF_HEADS, 3, t, t), F32),
        compiler_params=_params(1),
        name="t5_bias_tiles",
    )(rel_bias)


def _diff_kernel(lq1_ref, lk1_ref, lq2_ref, lk2_ref, subln_g_ref, bias_ref, q_ref, k_ref, v_ref, o_ref,
                 *, lambda_init):
    t = ATT_TILE
    seq = q_ref.shape[0]
    n_tiles = seq // t
    lane = lax.broadcasted_iota(jnp.int32, (t, HEAD_DIM), 1)
    row = lax.broadcasted_iota(jnp.int32, (t, t), 0)
    col = lax.broadcasted_iota(jnp.int32, (t, t), 1)
    causal = col <= row
    lam = (jnp.exp(jnp.sum(lq1_ref[...] * lk1_ref[...], axis=1, keepdims=True))
           - jnp.exp(jnp.sum(lq2_ref[...] * lk2_ref[...], axis=1, keepdims=True)) + lambda_init)

    def logits_of(i):
        q = q_ref[i * t:(i + 1) * t, :]
        zero = jnp.zeros_like(q)
        q_maps = (jnp.where(lane < DIFF_QK_DIM, q, zero), jnp.where(lane >= DIFF_QK_DIM, q, zero))
        k = k_ref[0:(i + 1) * t, :]
        return [lax.dot_general(qm, k, (((1,), (1,)), ((), ())), preferred_element_type=F32) for qm in q_maps]

    raws = {0: logits_of(0)}
    for i in range(n_tiles):
        n_keys = (i + 1) * t
        if i + 1 < n_tiles:
            raws[i + 1] = logits_of(i + 1)
        exps, sums = [], []
        for raw in raws.pop(i):
            parts = []
            for j in range(i + 1):
                s_j = raw[:, j * t:(j + 1) * t] + bias_ref[min(i - j, 2)]
                parts.append(jnp.where(causal, s_j, NEG_INF) if j == i else s_j)
            s = parts[0] if i == 0 else jnp.concatenate(parts, axis=1)
            e = jnp.exp2(s - jnp.max(s, axis=1, keepdims=True))
            exps.append(e)
            sums.append(jnp.sum(e, axis=1, keepdims=True))
        p = exps[0] - exps[1] * (lam * sums[0] / sums[1])
        o = jnp.dot(p.astype(BF16), v_ref[0:n_keys, :], preferred_element_type=F32) * (1.0 / sums[0])
        o_ref[i * t:(i + 1) * t, :] = (_rms(o, subln_g_ref[...]) * (1.0 - lambda_init)).astype(o_ref.dtype)


def _diff_attention(proj, bias_tiles, layer, lq1, lk1, lq2, lk2, subln_g, *, batch, seq, lambda_init):
    t = ATT_TILE
    vec = lambda n: _layer_vec_spec(layer, n)
    spec = lambda blk: pl.BlockSpec((None, seq, HEAD_DIM), lambda b, h: (b, 0, blk + h))
    kern = functools.partial(_diff_kernel, lambda_init=lambda_init)
    return pl.pallas_call(
        kern,
        grid=(batch, DIFF_HEADS),
        in_specs=[
            vec(DIFF_QK_DIM), vec(DIFF_QK_DIM), vec(DIFF_QK_DIM), vec(DIFF_QK_DIM), vec(HEAD_DIM),
            pl.BlockSpec((None, 3, t, t), lambda b, h: (h, 0, 0, 0)),
            spec(DF_Q_BLK), spec(DF_K_BLK), spec(DF_V_BLK),
        ],
        out_specs=spec(0),
        out_shape=jax.ShapeDtypeStruct((batch, seq, DIFF_HEADS * HEAD_DIM), BF16),
        compiler_params=_params(2),
        name="diff_attention",
    )(_as_rows(lq1), _as_rows(lk1), _as_rows(lq2), _as_rows(lk2), _as_rows(subln_g), bias_tiles, proj, proj, proj)


def _cpow(n, ar, ai, dt):
    mag = jnp.exp(n * (ar * dt))
    ang = n * (ai * dt)
    return mag * jnp.cos(ang), mag * jnp.sin(ang)


def _repeat_rows(x, reps):
    n, w = x.shape
    return jnp.broadcast_to(x[:, None, :], (n, reps, w)).reshape(n * reps, w)


def _ssm_prep_group(ar_ref, ai_ref, ldt_ref, bt_re_ref, bt_im_ref, ct_re_ref, ct_im_ref,
                    m_intra_ref, m_in_ref, m_out_ref, p1_ref, p2_ref):
    t = SSM_CHUNK
    cpg = SSM_CH_PER_GROUP
    ar, ai = ar_ref[...], ai_ref[...]
    dt = jnp.exp(ldt_ref[...])
    lb_re, lb_im = _cpow(1.0, ar, ai, dt)
    den = ar * ar + ai * ai
    f_re = ((lb_re - 1.0) * ar + lb_im * ai) / den
    f_im = (lb_im * ar - (lb_re - 1.0) * ai) / den
    bt_re, bt_im = bt_re_ref[...], bt_im_ref[...]
    bb_re = f_re * bt_re - f_im * bt_im
    bb_im = f_re * bt_im + f_im * bt_re
    ct_re, ct_im = ct_re_ref[...], ct_im_ref[...]
    pos = lax.broadcasted_iota(jnp.int32, (t, 2 * SSM_STATE), 0).astype(F32)
    lane_is_re = lax.broadcasted_iota(jnp.int32, (t * cpg, 2 * SSM_STATE), 1) < SSM_STATE

    def times_power(n, x_re, x_im, im_sign):
        e_re, e_im = _cpow(n, ar, ai, dt)
        e_re, e_im = _repeat_rows(e_re, cpg), _repeat_rows(e_im, cpg)
        return jnp.where(lane_is_re, e_re * x_re - e_im * x_im, im_sign * (e_re * x_im + e_im * x_re))

    m_in_ref[...] = times_power((t - 1.0) - pos, bb_re, bb_im, 1.0).astype(BF16)
    m_out_ref[...] = times_power(pos + 1.0, ct_re, ct_im, -1.0).astype(BF16)
    p_packed = times_power(-pos, bb_re, bb_im, -1.0)
    q_packed = times_power(pos, ct_re, ct_im, 1.0)
    k_full = lax.dot_general(p_packed, q_packed, (((1,), (1,)), ((), ())), preferred_element_type=F32,
                             precision=lax.Precision.HIGHEST)
    shift = cpg.bit_length() - 1
    row_k = lax.broadcasted_iota(jnp.int32, k_full.shape, 0)
    lane_k = lax.broadcasted_iota(jnp.int32, k_full.shape, 1)
    causal = jnp.right_shift(lane_k, shift) >= jnp.right_shift(row_k, shift)
    m_intra_ref[...] = jnp.where(causal, k_full, 0.0).astype(BF16)

    row_j = lax.broadcasted_iota(jnp.int32, (SSM_SCAN_ROWS, 2 * SSM_STATE), 0)
    lane_j = lax.broadcasted_iota(jnp.int32, (SSM_SCAN_ROWS, 2 * SSM_STATE), 1)
    pr, pi = _cpow(jnp.left_shift(t, row_j).astype(F32), ar, ai, dt)
    p1_ref[...] = pr
    p2_ref[...] = jnp.where(lane_j < SSM_STATE, -pi, pi)


def _ssm_prep_kernel(*refs):
    for g in range(refs[0].shape[0]):
        _ssm_prep_group(*[r.at[g] for r in refs])


def _ssm_prep(a_re, a_im, log_dt, b_re, b_im, c_re, c_im):
    nl, g, p = a_re.shape
    t, cpg = SSM_CHUNK, SSM_CH_PER_GROUP
    tc = t * cpg
    twice = lambda x: jnp.concatenate([x, x], axis=-1)
    rows = [twice(x)[:, :, None, :] for x in (a_re, a_im, jnp.broadcast_to(log_dt[:, :, None], a_re.shape))]
    bt = [jnp.tile(twice(jnp.swapaxes(b, 2, 3)), (1, 1, t, 1)) for b in (b_re, b_im)]
    ct = [jnp.tile(twice(c), (1, 1, t, 1)) for c in (c_re, c_im)]
    gps = SSM_PREP_GROUPS
    blk = lambda shape: pl.BlockSpec((None, gps) + shape, lambda l, gi: (l, gi, 0, 0))
    return pl.pallas_call(
        _ssm_prep_kernel,
        grid=(nl, g // gps),
        in_specs=[blk((1, 2 * p))] * 3 + [blk((tc, 2 * p))] * 4,
        out_specs=[blk((tc, tc)), blk((tc, 2 * p)), blk((tc, 2 * p)),
                   blk((SSM_SCAN_ROWS, 2 * p)), blk((SSM_SCAN_ROWS, 2 * p))],
        out_shape=[jax.ShapeDtypeStruct((nl, g, tc, tc), BF16),
                   jax.ShapeDtypeStruct((nl, g, tc, 2 * p), BF16),
                   jax.ShapeDtypeStruct((nl, g, tc, 2 * p), BF16),
                   jax.ShapeDtypeStruct((nl, g, SSM_SCAN_ROWS, 2 * p), F32),
                   jax.ShapeDtypeStruct((nl, g, SSM_SCAN_ROWS, 2 * p), F32)],
        compiler_params=_params(2),
        name="ssm_prep",
    )(*rows, *bt, *ct)


def _gelu_tanh(x):
    return 0.5 * x * (1.0 + jnp.tanh(math.sqrt(2.0 / math.pi) * (x + 0.044715 * (x * x * x))))


def _transpose_lane_pieces(vs):
    p = len(vs)
    width = vs[0].shape[1]
    w = width // p
    piece = lax.broadcasted_iota(jnp.int32, vs[0].shape, 1) // w
    vs = list(vs)
    s = p // 2
    while s >= 1:
        keep = jnp.bitwise_and(piece, s) == 0
        for i in range(p):
            if i & s:
                continue
            lo, hi = vs[i], vs[i + s]
            vs[i] = jnp.where(keep, lo, pltpu.roll(hi, s * w, axis=1))
            vs[i + s] = jnp.where(keep, pltpu.roll(lo, width - s * w, axis=1), hi)
        s //= 2
    return vs


def _ssm_kernel(u_ref, m_intra_ref, m_in_ref, m_out_ref, p1_ref, p2_ref, d_ref, o_ref, *, n_chunks):
    t, cpg = SSM_CHUNK, SSM_CH_PER_GROUP
    n_rows = u_ref.shape[0] // t
    groups = u_ref.shape[1] // cpg
    by_pos = [u_ref[pl.ds(tau, n_rows, stride=t), :] for tau in range(t)]
    by_group = [_transpose_lane_pieces(by_pos[h:h + groups]) for h in range(0, t, groups)]
    row = lax.broadcasted_iota(jnp.int32, (n_rows, 2 * SSM_STATE), 0)
    chunk = jnp.bitwise_and(row, n_chunks - 1)

    def chunks_back(x, n):
        return jnp.where(chunk >= n, pltpu.roll(x, n, axis=0), 0.0)

    def regroup(g):
        return jnp.concatenate([half[g] for half in by_group], axis=1)

    ys = []
    us = {0: regroup(0)}
    for g in range(groups):
        if g + 1 < groups:
            us[g + 1] = regroup(g + 1)
        u = us.pop(g)
        ub = u.astype(BF16)
        y = jnp.dot(ub, m_intra_ref[g], preferred_element_type=F32)
        s_in = jnp.dot(ub, m_in_ref[g], preferred_element_type=F32)
        x = chunks_back(s_in, 1)
        for j in range((n_chunks - 1).bit_length()):
            prev = chunks_back(x, 1 << j)
            x = x + p1_ref[g, j:j + 1, :] * prev + p2_ref[g, j:j + 1, :] * pltpu.roll(prev, SSM_STATE, axis=1)
        y_state = lax.dot_general(x.astype(BF16), m_out_ref[g], (((1,), (1,)), ((), ())),
                                  preferred_element_type=F32)
        ys.append(_gelu_tanh(y + y_state + d_ref[g] * u))
    lanes_per_half = groups * cpg
    for h in range(t // groups):
        half = _transpose_lane_pieces([y[:, h * lanes_per_half:(h + 1) * lanes_per_half] for y in ys])
        for i, rows_of_pos in enumerate(half):
            o_ref[pl.ds(h * groups + i, n_rows, stride=t), :] = rows_of_pos


def _ssm(u, layer, mats, d_skip, *, n_chunks):
    m, width = u.shape
    m_intra, m_in, m_out, p1, p2 = mats
    gps = V7X_LANES // SSM_CH_PER_GROUP
    lg = lambda x: pl.BlockSpec((None, gps) + x.shape[2:], lambda gi: (layer, gi, 0, 0))
    d_tiled = jnp.tile(d_skip, (1, 1, SSM_CHUNK))[:, :, None, :]
    kern = functools.partial(_ssm_kernel, n_chunks=n_chunks)
    return pl.pallas_call(
        kern,
        grid=(width // V7X_LANES,),
        in_specs=[pl.BlockSpec((m, V7X_LANES), lambda gi: (0, gi)),
                  lg(m_intra), lg(m_in), lg(m_out), lg(p1), lg(p2), lg(d_tiled)],
        out_specs=pl.BlockSpec((m, V7X_LANES), lambda gi: (0, gi)),
        out_shape=jax.ShapeDtypeStruct(u.shape, F32),
        compiler_params=_params(1),
        name="ssm_chunks",
    )(u, m_intra, m_in, m_out, p1, p2, d_tiled)


def _outproj_kernel(h_ref, sb_ref, ssm_ref, df_ref, wglu_ref, bglu_ref, w_f32_ref, g_ref, o_ref, mixed_ref, w_ref):
    w_sb, w_ssm = sb_ref.shape[1], ssm_ref.shape[1]

    @pl.when(pl.program_id(0) == 0)
    def _():
        def cast_rows(r0):
            rows = pl.ds(r0, ROW_CHUNK)
            w_ref[rows, :] = w_f32_ref[rows, :].astype(BF16)
        _for_row_chunks(w_ref.shape[0], ROW_CHUNK, cast_rows)

    y = ssm_ref[...]
    gate = jax.nn.sigmoid(jnp.dot(y.astype(BF16), wglu_ref[...].astype(BF16), preferred_element_type=F32)
                          + bglu_ref[...])
    mixed_ref[:, 0:w_sb] = sb_ref[...]
    mixed_ref[:, w_sb:w_sb + w_ssm] = (y * gate).astype(BF16)
    mixed_ref[:, w_sb + w_ssm:] = df_ref[...]
    out = jnp.dot(mixed_ref[...], w_ref[...], preferred_element_type=F32)
    o_ref[...] = h_ref[...] + _rms(out, g_ref[...])


def _outproj(h, o_sb, y_ssm, o_diff, layer, w_glu, b_glu, w_out, g, *, tm):
    m, d = h.shape
    k = w_out.shape[1]
    row_block = lambda x: pl.BlockSpec((tm, x.shape[1]), lambda i: (i, 0))
    return pl.pallas_call(
        _outproj_kernel,
        grid=(m // tm,),
        in_specs=[
            row_block(h), row_block(o_sb), row_block(y_ssm), row_block(o_diff),
            pl.BlockSpec((None,) + w_glu.shape[1:], lambda i: (layer, 0, 0)),
            _layer_vec_spec(layer, SSM_WIDTH),
            pl.BlockSpec((None, k, d), lambda i: (layer, 0, 0), pipeline_mode=pl.Buffered(1)),
            _layer_vec_spec(layer, d),
        ],
        out_specs=row_block(h),
        out_shape=jax.ShapeDtypeStruct((m, d), F32),
        scratch_shapes=[pltpu.VMEM((tm, k), BF16), pltpu.VMEM((k, d), BF16)],
        compiler_params=_params(1),
        name="outproj",
    )(h, o_sb, y_ssm, o_diff, w_glu, _as_rows(b_glu), w_out, _as_rows(g))


def _trunk(x, ffn1_pre_g, ffn1_w_gate, ffn1_w_up, ffn1_w_down, ffn1_post_g, mix_pre_g, w_in, ssm_a_re, ssm_a_im,
           ssm_log_dt, ssm_b_re, ssm_b_im, ssm_c_re, ssm_c_im, ssm_d, ssm_w_glu, ssm_b_glu, diff_lq1, diff_lk1,
           diff_lq2, diff_lk2, diff_subln_g, rel_bias, w_out, mix_post_g, ffn2_pre_g, ffn2_w_gate, ffn2_w_up,
           ffn2_w_down, ffn2_post_g, *, tm, tf, tm_out):
    batch, seq, d = x.shape
    m = batch * seq
    depth = w_in.shape[0]
    h = x.reshape(m, d)
    bias_tiles = _t5_bias_tiles(rel_bias)
    ssm_mats = _ssm_prep(ssm_a_re, ssm_a_im, ssm_log_dt, ssm_b_re, ssm_b_im, ssm_c_re, ssm_c_im)
    for l in range(depth):
        h = _ffn(h, l, ffn1_pre_g, ffn1_w_gate, ffn1_w_up, ffn1_w_down, ffn1_post_g, tm=tm, tf=tf)

        proj, ssm_u = _inproj(h, l, mix_pre_g, w_in, tm=tm)
        proj = proj.reshape(batch, seq, PROJ_COLS)

        o_sb = _sb_attention(proj, batch=batch, seq=seq)

        y_ssm = _ssm(ssm_u, l, ssm_mats, ssm_d, n_chunks=seq // SSM_CHUNK)

        lambda_init = 0.8 - 0.6 * math.exp(-0.3 * l)
        o_diff = _diff_attention(proj, bias_tiles, l, diff_lq1, diff_lk1, diff_lq2, diff_lk2, diff_subln_g,
                                 batch=batch, seq=seq, lambda_init=lambda_init)

        h = _outproj(h, o_sb.reshape(m, -1), y_ssm, o_diff.reshape(m, -1), l, ssm_w_glu, ssm_b_glu, w_out,
                     mix_post_g, tm=tm_out)

        h = _ffn(h, l, ffn2_pre_g, ffn2_w_gate, ffn2_w_up, ffn2_w_down, ffn2_post_g, tm=tm, tf=tf)
    return h.reshape(batch, seq, d)


def kernel(x, ffn1_pre_g, ffn1_w_gate, ffn1_w_up, ffn1_w_down, ffn1_post_g, mix_pre_g, w_in, ssm_a_re, ssm_a_im, ssm_log_dt, ssm_b_re, ssm_b_im, ssm_c_re, ssm_c_im, ssm_d, ssm_w_glu, ssm_b_glu, diff_lq1, diff_lk1, diff_lq2, diff_lk2, diff_subln_g, rel_bias, w_out, mix_post_g, ffn2_pre_g, ffn2_w_gate, ffn2_w_up, ffn2_w_down, ffn2_post_g):
    n_chunks = x.shape[1] // SSM_CHUNK
    assert x.shape[1] % ATT_TILE == 0 and n_chunks <= 1 << SSM_SCAN_ROWS and n_chunks & (n_chunks - 1) == 0
    return _trunk(x, ffn1_pre_g, ffn1_w_gate, ffn1_w_up, ffn1_w_down, ffn1_post_g, mix_pre_g, w_in, ssm_a_re,
                  ssm_a_im, ssm_log_dt, ssm_b_re, ssm_b_im, ssm_c_re, ssm_c_im, ssm_d, ssm_w_glu, ssm_b_glu,
                  diff_lq1, diff_lk1, diff_lq2, diff_lk2, diff_subln_g, rel_bias, w_out, mix_post_g, ffn2_pre_g,
                  ffn2_w_gate, ffn2_w_up, ffn2_w_down, ffn2_post_g, tm=ROW_TILE, tf=FFN_TILE, tm_out=OUTPROJ_ROW_TILE)
```

```python
import functools
import math

import jax
import jax.numpy as jnp
from jax import lax
from jax.experimental import pallas as pl
from jax.experimental.pallas import tpu as pltpu

F32 = jnp.float32
BF16 = jnp.bfloat16

HEAD_DIM = 128
SB_HEADS = 6
DIFF_HEADS = 6
DIFF_QK_DIM = HEAD_DIM // 2
SSM_GROUPS = 32
SSM_CH_PER_GROUP = 16
SSM_STATE = 64
SSM_WIDTH = SSM_GROUPS * SSM_CH_PER_GROUP
NUM_BUCKETS = 32
MAX_DISTANCE = 128
NORM_EPS = 1e-6
FFN_RESIDUAL_WEIGHT = 0.5
LOG2E = 1.0 / math.log(2.0)

SB_Q_BLK, SB_K_BLK, SB_V_BLK = 0, 6, 12
DF_Q_BLK, DF_K_BLK, DF_V_BLK = 18, 24, 30
PROJ_COLS = 36 * HEAD_DIM
SSM_COL0 = 3 * SB_HEADS * HEAD_DIM
PROJ_STEP_COLS = SB_HEADS * HEAD_DIM
SB_Q_STEP, DF_Q_STEP = SB_Q_BLK // SB_HEADS, DF_Q_BLK // DIFF_HEADS
SB_LOGIT_SCALE2 = LOG2E / math.sqrt(HEAD_DIM)
DF_LOGIT_SCALE2 = LOG2E / math.sqrt(DIFF_QK_DIM)

V7X_VMEM_LIMIT = 60 * 1024 * 1024
V7X_LANES = 128
ROW_TILE = 1024
FFN_TILE = 512
OUTPROJ_ROW_TILE = 512
FFN_SUB = 256
ROW_CHUNK = 128
ATT_TILE = 256
SB_NEAR_TILES = 2
SB_SATURATED = 151.0
SSM_CHUNK = 16
SSM_PREP_GROUPS = 8
SSM_SCAN_ROWS = 8
NEG_INF = float(jnp.finfo(jnp.float32).min)


def _params(n_grid, vmem=V7X_VMEM_LIMIT):
    return pltpu.CompilerParams(dimension_semantics=("arbitrary",) * n_grid, vmem_limit_bytes=vmem)


def _rms(x, g):
    ms = jnp.mean(x * x, axis=-1, keepdims=True)
    return x * lax.rsqrt(ms + NORM_EPS) * g


def _for_row_chunks(n_rows, chunk, body):
    def step(i, carry):
        body(pl.multiple_of(i * chunk, chunk))
        return carry
    lax.fori_loop(0, n_rows // chunk, step, 0)


def _layer_vec_spec(layer, n):
    return pl.BlockSpec((None, 1, n), lambda *_: (layer, 0, 0))


def _as_rows(p):
    return p.reshape(p.shape[0], 1, p.shape[1])


def _ffn_kernel(x_ref, pre_g_ref, wg_ref, wu_ref, wd_ref, post_g_ref, o_ref, xn_ref, a_ref, *, n_f, n_col_chunks):
    f = pl.program_id(1)
    tm, d = x_ref.shape

    @pl.when(f == 0)
    def _():
        def norm_rows(r0):
            rows = pl.ds(r0, ROW_CHUNK)
            xn_ref[rows, :] = _rms(x_ref[rows, :], pre_g_ref[...]).astype(BF16)
            o_ref[rows, :] = jnp.zeros((ROW_CHUNK, d), F32)
        _for_row_chunks(tm, ROW_CHUNK, norm_rows)

    xn = xn_ref[...]
    tf = wg_ref.shape[1]
    for hf in range(tf // FFN_SUB):
        sub = slice(hf * FFN_SUB, (hf + 1) * FFN_SUB)
        g = jnp.dot(xn, wg_ref[:, sub].astype(BF16), preferred_element_type=F32)
        u = jnp.dot(xn, wu_ref[:, sub].astype(BF16), preferred_element_type=F32)
        a_ref[:, sub] = (g * jax.nn.sigmoid(g) * u).astype(BF16)
    a = a_ref[...]
    cw = d // n_col_chunks
    for c in range(n_col_chunks):
        cols = slice(c * cw, (c + 1) * cw)
        o_ref[:, cols] += jnp.dot(a, wd_ref[:, cols].astype(BF16), preferred_element_type=F32)

    @pl.when(f == n_f - 1)
    def _():
        def finish_rows(r0):
            rows = pl.ds(r0, ROW_CHUNK)
            o_ref[rows, :] = x_ref[rows, :] + FFN_RESIDUAL_WEIGHT * _rms(o_ref[rows, :], post_g_ref[...])
        _for_row_chunks(tm, ROW_CHUNK, finish_rows)


def _ffn(h, layer, pre_g, w_gate, w_up, w_down, post_g, *, tm, tf):
    m, d = h.shape
    d_ff = w_gate.shape[2]
    n_f = d_ff // tf
    kern = functools.partial(_ffn_kernel, n_f=n_f, n_col_chunks=4)
    return pl.pallas_call(
        kern,
        grid=(m // tm, n_f),
        in_specs=[
            pl.BlockSpec((tm, d), lambda i, f: (i, 0)),
            _layer_vec_spec(layer, d),
            pl.BlockSpec((None, d, tf), lambda i, f: (layer, 0, f)),
            pl.BlockSpec((None, d, tf), lambda i, f: (layer, 0, f)),
            pl.BlockSpec((None, tf, d), lambda i, f: (layer, f, 0)),
            _layer_vec_spec(layer, d),
        ],
        out_specs=pl.BlockSpec((tm, d), lambda i, f: (i, 0), pipeline_mode=pl.Buffered(1)),
        out_shape=jax.ShapeDtypeStruct((m, d), F32),
        scratch_shapes=[pltpu.VMEM((tm, d), BF16), pltpu.VMEM((tm, tf), BF16)],
        compiler_params=_params(2),
        name="ffn",
    )(h, _as_rows(pre_g), w_gate, w_up, w_down, _as_rows(post_g))


def _inproj_kernel(x_ref, g_ref, w_ref, proj_ref, u_ref, xn_ref, *, n_head_steps):
    j = pl.program_id(1)
    tm = x_ref.shape[0]

    @pl.when(j == 0)
    def _():
        def norm_rows(r0):
            rows = pl.ds(r0, ROW_CHUNK)
            xn_ref[rows, :] = _rms(x_ref[rows, :], g_ref[...]).astype(BF16)
        _for_row_chunks(tm, ROW_CHUNK, norm_rows)

    @pl.when(j < n_head_steps)
    def _():
        scale = jnp.where(j == SB_Q_STEP, SB_LOGIT_SCALE2, jnp.where(j == DF_Q_STEP, DF_LOGIT_SCALE2, 1.0))
        y = jnp.dot(xn_ref[...], w_ref[0].astype(BF16), preferred_element_type=F32)
        proj_ref[...] = (y * scale).astype(proj_ref.dtype)

    @pl.when(j == n_head_steps)
    def _():
        u_ref[...] = jnp.dot(xn_ref[...], w_ref[0, :, 0:SSM_WIDTH].astype(BF16), preferred_element_type=F32)


def _inproj(h, layer, g, w_in, *, tm):
    m, d = h.shape
    n_head_steps = PROJ_COLS // PROJ_STEP_COLS
    ssm_step0 = SSM_COL0 // PROJ_STEP_COLS

    def w_col(i, j):
        head_blk = j * (PROJ_STEP_COLS // HEAD_DIM) + jnp.where(j >= ssm_step0, SSM_WIDTH // HEAD_DIM, 0)
        return (layer, 0, jnp.where(j < n_head_steps, head_blk, SSM_COL0 // HEAD_DIM) * HEAD_DIM)

    kern = functools.partial(_inproj_kernel, n_head_steps=n_head_steps)
    return pl.pallas_call(
        kern,
        grid=(m // tm, n_head_steps + 1),
        in_specs=[
            pl.BlockSpec((tm, d), lambda i, j: (i, 0)),
            _layer_vec_spec(layer, d),
            pl.BlockSpec((pl.Element(1), pl.Element(d), pl.Element(PROJ_STEP_COLS)), w_col),
        ],
        out_specs=[
            pl.BlockSpec((tm, PROJ_STEP_COLS), lambda i, j: (i, jnp.minimum(j, n_head_steps - 1))),
            pl.BlockSpec((tm, SSM_WIDTH), lambda i, j: (i, 0)),
        ],
        out_shape=[jax.ShapeDtypeStruct((m, PROJ_COLS), BF16), jax.ShapeDtypeStruct((m, SSM_WIDTH), F32)],
        scratch_shapes=[pltpu.VMEM((tm, d), BF16)],
        compiler_params=_params(2),
        name="inproj",
    )(h, _as_rows(g), w_in)


def _split2_bf16(x):
    hi = x.astype(BF16)
    lo = (x - hi.astype(F32)).astype(BF16)
    return hi, lo


def _mask_first_tile(x, keep):
    t = ATT_TILE
    head = jnp.where(keep, x[0:t], 0.0)
    return head if x.shape[0] == t else jnp.concatenate([head, x[t:]], axis=0)


def _sb_tile_front(q, k, suffix_ones, first_tile_mask):
    z = lax.dot_general(q, k, (((1,), (1,)), ((), ())), preferred_element_type=F32)
    pos = jnp.maximum(z, 0.0)
    neg = jnp.minimum(z, 0.0)
    l2 = jnp.log2(1.0 + jnp.exp2(neg - pos))
    sp = pos + l2
    sp_in = sp if first_tile_mask is None else _mask_first_tile(sp, first_tile_mask)
    hi, lo = _split2_bf16(sp_in)
    suffix = jnp.dot(jnp.concatenate([hi, lo], axis=1), suffix_ones, preferred_element_type=F32)
    return (neg - l2) - suffix, jnp.sum(sp_in, axis=1, keepdims=True)


def _sb_tile_back(log_w, later_rows, v, first_tile_mask):
    w = jnp.exp2(log_w - later_rows)
    if first_tile_mask is not None:
        w = _mask_first_tile(w, first_tile_mask)
    return jnp.dot(w.astype(BF16), v, preferred_element_type=F32)


def _sb_tile(q, k, v, later_rows, suffix_ones, first_tile_mask):
    log_w, row_sum = _sb_tile_front(q, k, suffix_ones, first_tile_mask)
    return _sb_tile_back(log_w, later_rows, v, first_tile_mask), row_sum


def _sb_kernel(q_ref, k_ref, v_ref, o_ref, acc_ref, later_ref):
    t = ATT_TILE
    seq = q_ref.shape[0]
    n_tiles = seq // t
    row = lax.broadcasted_iota(jnp.int32, (t, t), 0)
    col = lax.broadcasted_iota(jnp.int32, (t, t), 1)
    strictly_lower = row > col
    ones = jnp.where(strictly_lower, 1.0, 0.0).astype(BF16)
    suffix_ones = jnp.concatenate([ones, ones], axis=0)
    acc = [None] * n_tiles
    later = [None] * n_tiles

    def front(j):
        r0 = j * t
        r1 = min(r0 + SB_NEAR_TILES * t, seq)
        return _sb_tile_front(q_ref[r0:r1, :], k_ref[r0:r0 + t, :], suffix_ones, strictly_lower)

    fronts = {n_tiles - 1: front(n_tiles - 1)}
    for j in range(n_tiles - 1, -1, -1):
        r0 = j * t
        r1 = min(r0 + SB_NEAR_TILES * t, seq)
        if j > 0:
            fronts[j - 1] = front(j - 1)
        log_w, row_sum = fronts.pop(j)
        later_rows = jnp.concatenate([jnp.zeros((t, 1), F32)] + later[j + 1:j + SB_NEAR_TILES], axis=0)
        pv = _sb_tile_back(log_w, later_rows, v_ref[r0:r0 + t, :], strictly_lower)
        for i in range(j, r1 // t):
            rows = slice((i - j) * t, (i - j + 1) * t)
            acc[i] = pv[rows] if i == j else acc[i] + pv[rows]
            later[i] = row_sum[rows] if i == j else later[i] + row_sum[rows]

    for i in range(n_tiles):
        acc_ref[i * t:(i + 1) * t, :] = acc[i]
        later_ref[i * t:(i + 1) * t, :] = later[i]

    def far_tiles_of(i, carry):
        rows = pl.ds(pl.multiple_of(i * t, t), t)
        n_far = i + 1 - SB_NEAR_TILES

        def far_tile(n, inner):
            k0 = pl.multiple_of((n_far - 1 - n) * t, t)
            pv, row_sum = _sb_tile(q_ref[rows, :], k_ref[pl.ds(k0, t), :], v_ref[pl.ds(k0, t), :], later_ref[rows, :],
                                   suffix_ones, None)
            acc_ref[rows, :] += pv
            later_ref[rows, :] += row_sum
            return inner

        lax.fori_loop(0, jnp.where(jnp.min(later_ref[rows, :]) < SB_SATURATED, n_far, 0), far_tile, 0)
        return carry

    if n_tiles > SB_NEAR_TILES:
        any_unsaturated = jnp.min(functools.reduce(jnp.minimum, later[SB_NEAR_TILES:])) < SB_SATURATED
        lax.fori_loop(SB_NEAR_TILES, jnp.where(any_unsaturated, n_tiles, SB_NEAR_TILES), far_tiles_of, 0)
    o_ref[...] = acc_ref[...].astype(o_ref.dtype)


def _sb_attention(proj, *, batch, seq):
    spec = lambda blk: pl.BlockSpec((None, seq, HEAD_DIM), lambda b, h: (b, 0, blk + h))
    return pl.pallas_call(
        _sb_kernel,
        grid=(batch, SB_HEADS),
        in_specs=[spec(SB_Q_BLK), spec(SB_K_BLK), spec(SB_V_BLK)],
        out_specs=spec(0),
        out_shape=jax.ShapeDtypeStruct((batch, seq, SB_HEADS * HEAD_DIM), BF16),
        scratch_shapes=[pltpu.VMEM((seq, HEAD_DIM), F32), pltpu.VMEM((seq, 1), F32)],
        compiler_params=_params(2),
        name="sb_attention",
    )(proj, proj, proj)


def _t5_bias_kernel(rel_ref, o_ref):
    t = ATT_TILE
    h = pl.program_id(0)
    row = lax.broadcasted_iota(jnp.int32, (t, t), 0)
    col = lax.broadcasted_iota(jnp.int32, (t, t), 1)
    max_exact = NUM_BUCKETS // 2
    for d in range(3):
        n = jnp.maximum(d * t + row - col, 0)
        nf = jnp.maximum(n, 1).astype(F32)
        large = max_exact + (jnp.log(nf / max_exact) / math.log(MAX_DISTANCE / max_exact)
                             * (NUM_BUCKETS - max_exact)).astype(jnp.int32)
        large = jnp.minimum(large, NUM_BUCKETS - 1)
        bucket = jnp.where(n < max_exact, n, large)
        bias = jnp.zeros((t, t), F32)
        for b in range(NUM_BUCKETS):
            bias = jnp.where(bucket == b, rel_ref[b, h], bias)
        o_ref[d] = bias * LOG2E


def _t5_bias_tiles(rel_bias):
    t = ATT_TILE
    return pl.pallas_call(
        _t5_bias_kernel,
        grid=(DIFF_HEADS,),
        in_specs=[pl.BlockSpec(memory_space=pltpu.SMEM)],
        out_specs=pl.BlockSpec((None, 3, t, t), lambda h: (h, 0, 0, 0)),
        out_shape=jax.ShapeDtypeStruct((DIFF_HEADS, 3, t, t), F32),
        compiler_params=_params(1),
        name="t5_bias_tiles",
    )(rel_bias)


def _diff_kernel(lq1_ref, lk1_ref, lq2_ref, lk2_ref, subln_g_ref, bias_ref, q_ref, k_ref, v_ref, o_ref,
                 *, lambda_init):
    t = ATT_TILE
    seq = q_ref.shape[0]
    n_tiles = seq // t
    lane = lax.broadcasted_iota(jnp.int32, (t, HEAD_DIM), 1)
    row = lax.broadcasted_iota(jnp.int32, (t, t), 0)
    col = lax.broadcasted_iota(jnp.int32, (t, t), 1)
    causal = col <= row
    lam = (jnp.exp(jnp.sum(lq1_ref[...] * lk1_ref[...], axis=1, keepdims=True))
           - jnp.exp(jnp.sum(lq2_ref[...] * lk2_ref[...], axis=1, keepdims=True)) + lambda_init)

    def logits_of(i):
        q = q_ref[i * t:(i + 1) * t, :]
        zero = jnp.zeros_like(q)
        q_maps = (jnp.where(lane < DIFF_QK_DIM, q, zero), jnp.where(lane >= DIFF_QK_DIM, q, zero))
        k = k_ref[0:(i + 1) * t, :]
        return [lax.dot_general(qm, k, (((1,), (1,)), ((), ())), preferred_element_type=F32) for qm in q_maps]

    raws = {0: logits_of(0)}
    for i in range(n_tiles):
        n_keys = (i + 1) * t
        if i + 1 < n_tiles:
            raws[i + 1] = logits_of(i + 1)
        exps, sums = [], []
        for raw in raws.pop(i):
            parts = []
            for j in range(i + 1):
                s_j = raw[:, j * t:(j + 1) * t] + bias_ref[min(i - j, 2)]
                parts.append(jnp.where(causal, s_j, NEG_INF) if j == i else s_j)
            s = parts[0] if i == 0 else jnp.concatenate(parts, axis=1)
            e = jnp.exp2(s - jnp.max(s, axis=1, keepdims=True))
            exps.append(e)
            sums.append(jnp.sum(e, axis=1, keepdims=True))
        p = exps[0] - exps[1] * (lam * sums[0] / sums[1])
        o = jnp.dot(p.astype(BF16), v_ref[0:n_keys, :], preferred_element_type=F32) * (1.0 / sums[0])
        o_ref[i * t:(i + 1) * t, :] = (_rms(o, subln_g_ref[...]) * (1.0 - lambda_init)).astype(o_ref.dtype)


def _diff_attention(proj, bias_tiles, layer, lq1, lk1, lq2, lk2, subln_g, *, batch, seq, lambda_init):
    t = ATT_TILE
    vec = lambda n: _layer_vec_spec(layer, n)
    spec = lambda blk: pl.BlockSpec((None, seq, HEAD_DIM), lambda b, h: (b, 0, blk + h))
    kern = functools.partial(_diff_kernel, lambda_init=lambda_init)
    return pl.pallas_call(
        kern,
        grid=(batch, DIFF_HEADS),
        in_specs=[
            vec(DIFF_QK_DIM), vec(DIFF_QK_DIM), vec(DIFF_QK_DIM), vec(DIFF_QK_DIM), vec(HEAD_DIM),
            pl.BlockSpec((None, 3, t, t), lambda b, h: (h, 0, 0, 0)),
            spec(DF_Q_BLK), spec(DF_K_BLK), spec(DF_V_BLK),
        ],
        out_specs=spec(0),
        out_shape=jax.ShapeDtypeStruct((batch, seq, DIFF_HEADS * HEAD_DIM), BF16),
        compiler_params=_params(2),
        name="diff_attention",
    )(_as_rows(lq1), _as_rows(lk1), _as_rows(lq2), _as_rows(lk2), _as_rows(subln_g), bias_tiles, proj, proj, proj)


def _dot_nt_split(a, b):
    nt = lambda x, y: lax.dot_general(x, y, (((1,), (1,)), ((), ())), preferred_element_type=F32)
    a_hi, a_lo = _split2_bf16(a)
    b_hi, b_lo = _split2_bf16(b)
    return nt(a_hi, b_hi) + (nt(a_hi, b_lo) + nt(a_lo, b_hi))


def _cpow(n, ar, ai, dt):
    mag = jnp.exp(n * (ar * dt))
    ang = n * (ai * dt)
    return mag * jnp.cos(ang), mag * jnp.sin(ang)


def _repeat_rows(x, reps):
    n, w = x.shape
    return jnp.broadcast_to(x[:, None, :], (n, reps, w)).reshape(n * reps, w)


def _ssm_prep_group(ar_ref, ai_ref, ldt_ref, bt_re_ref, bt_im_ref, ct_re_ref, ct_im_ref,
                    m_intra_ref, m_in_ref, m_out_ref, p1_ref, p2_ref):
    t = SSM_CHUNK
    cpg = SSM_CH_PER_GROUP
    ar, ai = ar_ref[...], ai_ref[...]
    dt = jnp.exp(ldt_ref[...])
    lb_re, lb_im = _cpow(1.0, ar, ai, dt)
    den = ar * ar + ai * ai
    f_re = ((lb_re - 1.0) * ar + lb_im * ai) / den
    f_im = (lb_im * ar - (lb_re - 1.0) * ai) / den
    bt_re, bt_im = bt_re_ref[...], bt_im_ref[...]
    bb_re = f_re * bt_re - f_im * bt_im
    bb_im = f_re * bt_im + f_im * bt_re
    ct_re, ct_im = ct_re_ref[...], ct_im_ref[...]
    pos = lax.broadcasted_iota(jnp.int32, (t, 2 * SSM_STATE), 0).astype(F32)
    lane_is_re = lax.broadcasted_iota(jnp.int32, (t * cpg, 2 * SSM_STATE), 1) < SSM_STATE

    def times_power(n, x_re, x_im, im_sign):
        e_re, e_im = _cpow(n, ar, ai, dt)
        e_re, e_im = _repeat_rows(e_re, cpg), _repeat_rows(e_im, cpg)
        return jnp.where(lane_is_re, e_re * x_re - e_im * x_im, im_sign * (e_re * x_im + e_im * x_re))

    m_in_ref[...] = times_power((t - 1.0) - pos, bb_re, bb_im, 1.0).astype(BF16)
    m_out_ref[...] = times_power(pos + 1.0, ct_re, ct_im, -1.0).astype(BF16)
    p_packed = times_power(-pos, bb_re, bb_im, -1.0)
    q_packed = times_power(pos, ct_re, ct_im, 1.0)
    k_full = _dot_nt_split(p_packed, q_packed)
    shift = cpg.bit_length() - 1
    row_k = lax.broadcasted_iota(jnp.int32, k_full.shape, 0)
    lane_k = lax.broadcasted_iota(jnp.int32, k_full.shape, 1)
    causal = jnp.right_shift(lane_k, shift) >= jnp.right_shift(row_k, shift)
    m_intra_ref[...] = jnp.where(causal, k_full, 0.0).astype(BF16)

    row_j = lax.broadcasted_iota(jnp.int32, (SSM_SCAN_ROWS, 2 * SSM_STATE), 0)
    lane_j = lax.broadcasted_iota(jnp.int32, (SSM_SCAN_ROWS, 2 * SSM_STATE), 1)
    pr, pi = _cpow(jnp.left_shift(t, row_j).astype(F32), ar, ai, dt)
    p1_ref[...] = pr
    p2_ref[...] = jnp.where(lane_j < SSM_STATE, -pi, pi)


def _ssm_prep_kernel(*refs):
    for g in range(refs[0].shape[0]):
        _ssm_prep_group(*[r.at[g] for r in refs])


def _ssm_prep(a_re, a_im, log_dt, b_re, b_im, c_re, c_im):
    nl, g, p = a_re.shape
    t, cpg = SSM_CHUNK, SSM_CH_PER_GROUP
    tc = t * cpg
    twice = lambda x: jnp.concatenate([x, x], axis=-1)
    rows = [twice(x)[:, :, None, :] for x in (a_re, a_im, jnp.broadcast_to(log_dt[:, :, None], a_re.shape))]
    bt = [jnp.tile(twice(jnp.swapaxes(b, 2, 3)), (1, 1, t, 1)) for b in (b_re, b_im)]
    ct = [jnp.tile(twice(c), (1, 1, t, 1)) for c in (c_re, c_im)]
    gps = SSM_PREP_GROUPS
    blk = lambda shape: pl.BlockSpec((None, gps) + shape, lambda l, gi: (l, gi, 0, 0))
    return pl.pallas_call(
        _ssm_prep_kernel,
        grid=(nl, g // gps),
        in_specs=[blk((1, 2 * p))] * 3 + [blk((tc, 2 * p))] * 4,
        out_specs=[blk((tc, tc)), blk((tc, 2 * p)), blk((tc, 2 * p)),
                   blk((SSM_SCAN_ROWS, 2 * p)), blk((SSM_SCAN_ROWS, 2 * p))],
        out_shape=[jax.ShapeDtypeStruct((nl, g, tc, tc), BF16),
                   jax.ShapeDtypeStruct((nl, g, tc, 2 * p), BF16),
                   jax.ShapeDtypeStruct((nl, g, tc, 2 * p), BF16),
                   jax.ShapeDtypeStruct((nl, g, SSM_SCAN_ROWS, 2 * p), F32),
                   jax.ShapeDtypeStruct((nl, g, SSM_SCAN_ROWS, 2 * p), F32)],
        compiler_params=_params(2),
        name="ssm_prep",
    )(*rows, *bt, *ct)


def _gelu_tanh(x):
    return 0.5 * x * (1.0 + jnp.tanh(math.sqrt(2.0 / math.pi) * (x + 0.044715 * (x * x * x))))


def _transpose_lane_pieces(vs):
    p = len(vs)
    width = vs[0].shape[1]
    w = width // p
    piece = lax.broadcasted_iota(jnp.int32, vs[0].shape, 1) // w
    vs = list(vs)
    s = p // 2
    while s >= 1:
        keep = jnp.bitwise_and(piece, s) == 0
        for i in range(p):
            if i & s:
                continue
            lo, hi = vs[i], vs[i + s]
            vs[i] = jnp.where(keep, lo, pltpu.roll(hi, s * w, axis=1))
            vs[i + s] = jnp.where(keep, pltpu.roll(lo, width - s * w, axis=1), hi)
        s //= 2
    return vs


def _ssm_kernel(u_ref, m_intra_ref, m_in_ref, m_out_ref, p1_ref, p2_ref, d_ref, o_ref, *, n_chunks):
    t, cpg = SSM_CHUNK, SSM_CH_PER_GROUP
    n_rows = u_ref.shape[0] // t
    groups = u_ref.shape[1] // cpg
    by_pos = [u_ref[pl.ds(tau, n_rows, stride=t), :] for tau in range(t)]
    by_group = [_transpose_lane_pieces(by_pos[h:h + groups]) for h in range(0, t, groups)]
    row = lax.broadcasted_iota(jnp.int32, (n_rows, 2 * SSM_STATE), 0)
    chunk = jnp.bitwise_and(row, n_chunks - 1)

    def chunks_back(x, n):
        return jnp.where(chunk >= n, pltpu.roll(x, n, axis=0), 0.0)

    def regroup(g):
        return jnp.concatenate([half[g] for half in by_group], axis=1)

    ys = []
    us = {0: regroup(0)}
    for g in range(groups):
        if g + 1 < groups:
            us[g + 1] = regroup(g + 1)
        u = us.pop(g)
        ub = u.astype(BF16)
        y = jnp.dot(ub, m_intra_ref[g], preferred_element_type=F32)
        s_in = jnp.dot(ub, m_in_ref[g], preferred_element_type=F32)
        x = chunks_back(s_in, 1)
        for j in range((n_chunks - 1).bit_length()):
            prev = chunks_back(x, 1 << j)
            x = x + p1_ref[g, j:j + 1, :] * prev + p2_ref[g, j:j + 1, :] * pltpu.roll(prev, SSM_STATE, axis=1)
        y_state = lax.dot_general(x.astype(BF16), m_out_ref[g], (((1,), (1,)), ((), ())),
                                  preferred_element_type=F32)
        ys.append(_gelu_tanh(y + y_state + d_ref[g] * u))
    lanes_per_half = groups * cpg
    for h in range(t // groups):
        half = _transpose_lane_pieces([y[:, h * lanes_per_half:(h + 1) * lanes_per_half] for y in ys])
        for i, rows_of_pos in enumerate(half):
            o_ref[pl.ds(h * groups + i, n_rows, stride=t), :] = rows_of_pos


def _ssm(u, layer, mats, d_skip, *, n_chunks):
    m, width = u.shape
    m_intra, m_in, m_out, p1, p2 = mats
    gps = V7X_LANES // SSM_CH_PER_GROUP
    lg = lambda x: pl.BlockSpec((None, gps) + x.shape[2:], lambda gi: (layer, gi, 0, 0))
    d_tiled = jnp.tile(d_skip, (1, 1, SSM_CHUNK))[:, :, None, :]
    kern = functools.partial(_ssm_kernel, n_chunks=n_chunks)
    return pl.pallas_call(
        kern,
        grid=(width // V7X_LANES,),
        in_specs=[pl.BlockSpec((m, V7X_LANES), lambda gi: (0, gi)),
                  lg(m_intra), lg(m_in), lg(m_out), lg(p1), lg(p2), lg(d_tiled)],
        out_specs=pl.BlockSpec((m, V7X_LANES), lambda gi: (0, gi)),
        out_shape=jax.ShapeDtypeStruct(u.shape, F32),
        compiler_params=_params(1),
        name="ssm_chunks",
    )(u, m_intra, m_in, m_out, p1, p2, d_tiled)


def _outproj_kernel(h_ref, sb_ref, ssm_ref, df_ref, wglu_ref, bglu_ref, w_f32_ref, g_ref, o_ref, mixed_ref, w_ref):
    w_sb, w_ssm = sb_ref.shape[1], ssm_ref.shape[1]

    @pl.when(pl.program_id(0) == 0)
    def _():
        def cast_rows(r0):
            rows = pl.ds(r0, ROW_CHUNK)
            w_ref[rows, :] = w_f32_ref[rows, :].astype(BF16)
        _for_row_chunks(w_ref.shape[0], ROW_CHUNK, cast_rows)

    y = ssm_ref[...]
    gate = jax.nn.sigmoid(jnp.dot(y.astype(BF16), wglu_ref[...].astype(BF16), preferred_element_type=F32)
                          + bglu_ref[...])
    mixed_ref[:, 0:w_sb] = sb_ref[...]
    mixed_ref[:, w_sb:w_sb + w_ssm] = (y * gate).astype(BF16)
    mixed_ref[:, w_sb + w_ssm:] = df_ref[...]
    out = jnp.dot(mixed_ref[...], w_ref[...], preferred_element_type=F32)
    o_ref[...] = h_ref[...] + _rms(out, g_ref[...])


def _outproj(h, o_sb, y_ssm, o_diff, layer, w_glu, b_glu, w_out, g, *, tm):
    m, d = h.shape
    k = w_out.shape[1]
    row_block = lambda x: pl.BlockSpec((tm, x.shape[1]), lambda i: (i, 0))
    return pl.pallas_call(
        _outproj_kernel,
        grid=(m // tm,),
        in_specs=[
            row_block(h), row_block(o_sb), row_block(y_ssm), row_block(o_diff),
            pl.BlockSpec((None,) + w_glu.shape[1:], lambda i: (layer, 0, 0)),
            _layer_vec_spec(layer, SSM_WIDTH),
            pl.BlockSpec((None, k, d), lambda i: (layer, 0, 0), pipeline_mode=pl.Buffered(1)),
            _layer_vec_spec(layer, d),
        ],
        out_specs=row_block(h),
        out_shape=jax.ShapeDtypeStruct((m, d), F32),
        scratch_shapes=[pltpu.VMEM((tm, k), BF16), pltpu.VMEM((k, d), BF16)],
        compiler_params=_params(1),
        name="outproj",
    )(h, o_sb, y_ssm, o_diff, w_glu, _as_rows(b_glu), w_out, _as_rows(g))


def _trunk(x, ffn1_pre_g, ffn1_w_gate, ffn1_w_up, ffn1_w_down, ffn1_post_g, mix_pre_g, w_in, ssm_a_re, ssm_a_im,
           ssm_log_dt, ssm_b_re, ssm_b_im, ssm_c_re, ssm_c_im, ssm_d, ssm_w_glu, ssm_b_glu, diff_lq1, diff_lk1,
           diff_lq2, diff_lk2, diff_subln_g, rel_bias, w_out, mix_post_g, ffn2_pre_g, ffn2_w_gate, ffn2_w_up,
           ffn2_w_down, ffn2_post_g, *, tm, tf, tm_out):
    batch, seq, d = x.shape
    m = batch * seq
    depth = w_in.shape[0]
    h = x.reshape(m, d)
    bias_tiles = _t5_bias_tiles(rel_bias)
    ssm_mats = _ssm_prep(ssm_a_re, ssm_a_im, ssm_log_dt, ssm_b_re, ssm_b_im, ssm_c_re, ssm_c_im)
    for l in range(depth):
        h = _ffn(h, l, ffn1_pre_g, ffn1_w_gate, ffn1_w_up, ffn1_w_down, ffn1_post_g, tm=tm, tf=tf)

        proj, ssm_u = _inproj(h, l, mix_pre_g, w_in, tm=tm)
        proj = proj.reshape(batch, seq, PROJ_COLS)

        o_sb = _sb_attention(proj, batch=batch, seq=seq)

        y_ssm = _ssm(ssm_u, l, ssm_mats, ssm_d, n_chunks=seq // SSM_CHUNK)

        lambda_init = 0.8 - 0.6 * math.exp(-0.3 * l)
        o_diff = _diff_attention(proj, bias_tiles, l, diff_lq1, diff_lk1, diff_lq2, diff_lk2, diff_subln_g,
                                 batch=batch, seq=seq, lambda_init=lambda_init)

        h = _outproj(h, o_sb.reshape(m, -1), y_ssm, o_diff.reshape(m, -1), l, ssm_w_glu, ssm_b_glu, w_out,
                     mix_post_g, tm=tm_out)

        h = _ffn(h, l, ffn2_pre_g, ffn2_w_gate, ffn2_w_up, ffn2_w_down, ffn2_post_g, tm=tm, tf=tf)
    return h.reshape(batch, seq, d)


def kernel(x, ffn1_pre_g, ffn1_w_gate, ffn1_w_up, ffn1_w_down, ffn1_post_g, mix_pre_g, w_in, ssm_a_re, ssm_a_im, ssm_log_dt, ssm_b_re, ssm_b_im, ssm_c_re, ssm_c_im, ssm_d, ssm_w_glu, ssm_b_glu, diff_lq1, diff_lk1, diff_lq2, diff_lk2, diff_subln_g, rel_bias, w_out, mix_post_g, ffn2_pre_g, ffn2_w_gate, ffn2_w_up, ffn2_w_down, ffn2_post_g):
    n_chunks = x.shape[1] // SSM_CHUNK
    assert x.shape[1] % ATT_TILE == 0 and n_chunks <= 1 << SSM_SCAN_ROWS and n_chunks & (n_chunks - 1) == 0
    return _trunk(x, ffn1_pre_g, ffn1_w_gate, ffn1_w_up, ffn1_w_down, ffn1_post_g, mix_pre_g, w_in, ssm_a_re,
                  ssm_a_im, ssm_log_dt, ssm_b_re, ssm_b_im, ssm_c_re, ssm_c_im, ssm_d, ssm_w_glu, ssm_b_glu,
                  diff_lq1, diff_lk1, diff_lq2, diff_lk2, diff_subln_g, rel_bias, w_out, mix_post_g, ffn2_pre_g,
                  ffn2_w_gate, ffn2_w_up, ffn2_w_down, ffn2_post_g, tm=ROW_TILE, tf=FFN_TILE, tm_out=OUTPROJ_ROW_TILE)
```

```python
import functools
import math

import jax
import jax.numpy as jnp
from jax import lax
from jax.experimental import pallas as pl
from jax.experimental.pallas import tpu as pltpu

F32 = jnp.float32
BF16 = jnp.bfloat16

HEAD_DIM = 128
SB_HEADS = 6
DIFF_HEADS = 6
DIFF_QK_DIM = HEAD_DIM // 2
SSM_GROUPS = 32
SSM_CH_PER_GROUP = 16
SSM_STATE = 64
SSM_WIDTH = SSM_GROUPS * SSM_CH_PER_GROUP
NUM_BUCKETS = 32
MAX_DISTANCE = 128
NORM_EPS = 1e-6
FFN_RESIDUAL_WEIGHT = 0.5
LOG2E = 1.0 / math.log(2.0)

SB_Q_BLK, SB_K_BLK, SB_V_BLK = 0, 6, 12
DF_Q_BLK, DF_K_BLK, DF_V_BLK = 18, 24, 30
PROJ_COLS = 36 * HEAD_DIM
SSM_COL0 = 3 * SB_HEADS * HEAD_DIM
PROJ_STEP_COLS = SB_HEADS * HEAD_DIM
SB_Q_STEP, DF_Q_STEP = SB_Q_BLK // SB_HEADS, DF_Q_BLK // DIFF_HEADS
SB_LOGIT_SCALE2 = LOG2E / math.sqrt(HEAD_DIM)
DF_LOGIT_SCALE2 = LOG2E / math.sqrt(DIFF_QK_DIM)

V7X_VMEM_LIMIT = 60 * 1024 * 1024
V7X_LANES = 128
ROW_TILE = 1024
FFN_TILE = 512
OUTPROJ_ROW_TILE = 512
FFN_SUB = 256
ROW_CHUNK = 128
ATT_TILE = 256
SB_NEAR_TILES = 2
SB_SATURATED = 151.0
SSM_CHUNK = 16
SSM_PREP_GROUPS = 8
SSM_SCAN_ROWS = 8
NEG_INF = float(jnp.finfo(jnp.float32).min)


def _params(n_grid, vmem=V7X_VMEM_LIMIT):
    return pltpu.CompilerParams(dimension_semantics=("arbitrary",) * n_grid, vmem_limit_bytes=vmem)


def _rms(x, g):
    ms = jnp.mean(x * x, axis=-1, keepdims=True)
    return x * lax.rsqrt(ms + NORM_EPS) * g


def _for_row_chunks(n_rows, chunk, body):
    def step(i, carry):
        body(pl.multiple_of(i * chunk, chunk))
        return carry
    lax.fori_loop(0, n_rows // chunk, step, 0)


def _layer_vec_spec(layer, n):
    return pl.BlockSpec((None, 1, n), lambda *_: (layer, 0, 0))


def _as_rows(p):
    return p.reshape(p.shape[0], 1, p.shape[1])


def _ffn_kernel(x_ref, pre_g_ref, wg_ref, wu_ref, wd_ref, post_g_ref, o_ref, xn_ref, a_ref, *, n_f, n_col_chunks):
    f = pl.program_id(1)
    tm, d = x_ref.shape

    @pl.when(f == 0)
    def _():
        def norm_rows(r0):
            rows = pl.ds(r0, ROW_CHUNK)
            xn_ref[rows, :] = _rms(x_ref[rows, :], pre_g_ref[...]).astype(BF16)
            o_ref[rows, :] = jnp.zeros((ROW_CHUNK, d), F32)
        _for_row_chunks(tm, ROW_CHUNK, norm_rows)

    xn = xn_ref[...]
    tf = wg_ref.shape[1]
    for hf in range(tf // FFN_SUB):
        sub = slice(hf * FFN_SUB, (hf + 1) * FFN_SUB)
        g = jnp.dot(xn, wg_ref[:, sub].astype(BF16), preferred_element_type=F32)
        u = jnp.dot(xn, wu_ref[:, sub].astype(BF16), preferred_element_type=F32)
        a_ref[:, sub] = (g * jax.nn.sigmoid(g) * u).astype(BF16)
    a = a_ref[...]
    cw = d // n_col_chunks
    for c in range(n_col_chunks):
        cols = slice(c * cw, (c + 1) * cw)
        o_ref[:, cols] += jnp.dot(a, wd_ref[:, cols].astype(BF16), preferred_element_type=F32)

    @pl.when(f == n_f - 1)
    def _():
        def finish_rows(r0):
            rows = pl.ds(r0, ROW_CHUNK)
            o_ref[rows, :] = x_ref[rows, :] + FFN_RESIDUAL_WEIGHT * _rms(o_ref[rows, :], post_g_ref[...])
        _for_row_chunks(tm, ROW_CHUNK, finish_rows)


def _ffn(h, layer, pre_g, w_gate, w_up, w_down, post_g, *, tm, tf):
    m, d = h.shape
    d_ff = w_gate.shape[2]
    n_f = d_ff // tf
    kern = functools.partial(_ffn_kernel, n_f=n_f, n_col_chunks=4)
    return pl.pallas_call(
        kern,
        grid=(m // tm, n_f),
        in_specs=[
            pl.BlockSpec((tm, d), lambda i, f: (i, 0)),
            _layer_vec_spec(layer, d),
            pl.BlockSpec((None, d, tf), lambda i, f: (layer, 0, f)),
            pl.BlockSpec((None, d, tf), lambda i, f: (layer, 0, f)),
            pl.BlockSpec((None, tf, d), lambda i, f: (layer, f, 0)),
            _layer_vec_spec(layer, d),
        ],
        out_specs=pl.BlockSpec((tm, d), lambda i, f: (i, 0), pipeline_mode=pl.Buffered(1)),
        out_shape=jax.ShapeDtypeStruct((m, d), F32),
        scratch_shapes=[pltpu.VMEM((tm, d), BF16), pltpu.VMEM((tm, tf), BF16)],
        compiler_params=_params(2),
        name="ffn",
    )(h, _as_rows(pre_g), w_gate, w_up, w_down, _as_rows(post_g))


def _inproj_kernel(x_ref, g_ref, w_ref, proj_ref, u_ref, xn_ref, *, n_head_steps):
    j = pl.program_id(1)
    tm = x_ref.shape[0]

    @pl.when(j == 0)
    def _():
        def norm_rows(r0):
            rows = pl.ds(r0, ROW_CHUNK)
            xn_ref[rows, :] = _rms(x_ref[rows, :], g_ref[...]).astype(BF16)
        _for_row_chunks(tm, ROW_CHUNK, norm_rows)

    @pl.when(j < n_head_steps)
    def _():
        scale = jnp.where(j == SB_Q_STEP, SB_LOGIT_SCALE2, jnp.where(j == DF_Q_STEP, DF_LOGIT_SCALE2, 1.0))
        y = jnp.dot(xn_ref[...], w_ref[0].astype(BF16), preferred_element_type=F32)
        proj_ref[...] = (y * scale).astype(proj_ref.dtype)

    @pl.when(j == n_head_steps)
    def _():
        u_ref[...] = jnp.dot(xn_ref[...], w_ref[0, :, 0:SSM_WIDTH].astype(BF16), preferred_element_type=F32)


def _inproj(h, layer, g, w_in, *, tm):
    m, d = h.shape
    n_head_steps = PROJ_COLS // PROJ_STEP_COLS
    ssm_step0 = SSM_COL0 // PROJ_STEP_COLS

    def w_col(i, j):
        head_blk = j * (PROJ_STEP_COLS // HEAD_DIM) + jnp.where(j >= ssm_step0, SSM_WIDTH // HEAD_DIM, 0)
        return (layer, 0, jnp.where(j < n_head_steps, head_blk, SSM_COL0 // HEAD_DIM) * HEAD_DIM)

    kern = functools.partial(_inproj_kernel, n_head_steps=n_head_steps)
    return pl.pallas_call(
        kern,
        grid=(m // tm, n_head_steps + 1),
        in_specs=[
            pl.BlockSpec((tm, d), lambda i, j: (i, 0)),
            _layer_vec_spec(layer, d),
            pl.BlockSpec((pl.Element(1), pl.Element(d), pl.Element(PROJ_STEP_COLS)), w_col),
        ],
        out_specs=[
            pl.BlockSpec((tm, PROJ_STEP_COLS), lambda i, j: (i, jnp.minimum(j, n_head_steps - 1))),
            pl.BlockSpec((tm, SSM_WIDTH), lambda i, j: (i, 0)),
        ],
        out_shape=[jax.ShapeDtypeStruct((m, PROJ_COLS), BF16), jax.ShapeDtypeStruct((m, SSM_WIDTH), F32)],
        scratch_shapes=[pltpu.VMEM((tm, d), BF16)],
        compiler_params=_params(2),
        name="inproj",
    )(h, _as_rows(g), w_in)


def _split2_bf16(x):
    hi = x.astype(BF16)
    lo = (x - hi.astype(F32)).astype(BF16)
    return hi, lo


def _mask_first_tile(x, keep):
    t = ATT_TILE
    head = jnp.where(keep, x[0:t], 0.0)
    return head if x.shape[0] == t else jnp.concatenate([head, x[t:]], axis=0)


def _sb_tile_front(q, k, suffix_ones, first_tile_mask):
    z = lax.dot_general(q, k, (((1,), (1,)), ((), ())), preferred_element_type=F32)
    pos = jnp.maximum(z, 0.0)
    neg = jnp.minimum(z, 0.0)
    l2 = jnp.log2(1.0 + jnp.exp2(neg - pos))
    sp = pos + l2
    sp_in = sp if first_tile_mask is None else _mask_first_tile(sp, first_tile_mask)
    hi, lo = _split2_bf16(sp_in)
    suffix = jnp.dot(jnp.concatenate([hi, lo], axis=1), suffix_ones, preferred_element_type=F32)
    return (neg - l2) - suffix, jnp.sum(sp_in, axis=1, keepdims=True)


def _sb_tile_back(log_w, later_rows, v, first_tile_mask):
    w = jnp.exp2(log_w - later_rows)
    if first_tile_mask is not None:
        w = _mask_first_tile(w, first_tile_mask)
    return jnp.dot(w.astype(BF16), v, preferred_element_type=F32)


def _sb_tile(q, k, v, later_rows, suffix_ones, first_tile_mask):
    log_w, row_sum = _sb_tile_front(q, k, suffix_ones, first_tile_mask)
    return _sb_tile_back(log_w, later_rows, v, first_tile_mask), row_sum


def _sb_kernel(q_ref, k_ref, v_ref, o_ref, acc_ref, later_ref):
    t = ATT_TILE
    seq = q_ref.shape[0]
    n_tiles = seq // t
    row = lax.broadcasted_iota(jnp.int32, (t, t), 0)
    col = lax.broadcasted_iota(jnp.int32, (t, t), 1)
    strictly_lower = row > col
    ones = jnp.where(strictly_lower, 1.0, 0.0).astype(BF16)
    suffix_ones = jnp.concatenate([ones, ones], axis=0)
    acc = [None] * n_tiles
    later = [None] * n_tiles

    def front(j):
        r0 = j * t
        r1 = min(r0 + SB_NEAR_TILES * t, seq)
        return _sb_tile_front(q_ref[r0:r1, :], k_ref[r0:r0 + t, :], suffix_ones, strictly_lower)

    fronts = {n_tiles - 1: front(n_tiles - 1)}
    for j in range(n_tiles - 1, -1, -1):
        r0 = j * t
        r1 = min(r0 + SB_NEAR_TILES * t, seq)
        if j > 0:
            fronts[j - 1] = front(j - 1)
        log_w, row_sum = fronts.pop(j)
        later_rows = jnp.concatenate([jnp.zeros((t, 1), F32)] + later[j + 1:j + SB_NEAR_TILES], axis=0)
        pv = _sb_tile_back(log_w, later_rows, v_ref[r0:r0 + t, :], strictly_lower)
        for i in range(j, r1 // t):
            rows = slice((i - j) * t, (i - j + 1) * t)
            acc[i] = pv[rows] if i == j else acc[i] + pv[rows]
            later[i] = row_sum[rows] if i == j else later[i] + row_sum[rows]

    for i in range(n_tiles):
        acc_ref[i * t:(i + 1) * t, :] = acc[i]
        later_ref[i * t:(i + 1) * t, :] = later[i]

    def far_tiles_of(i, carry):
        rows = pl.ds(pl.multiple_of(i * t, t), t)
        n_far = i + 1 - SB_NEAR_TILES

        def far_tile(n, inner):
            k0 = pl.multiple_of((n_far - 1 - n) * t, t)
            pv, row_sum = _sb_tile(q_ref[rows, :], k_ref[pl.ds(k0, t), :], v_ref[pl.ds(k0, t), :], later_ref[rows, :],
                                   suffix_ones, None)
            acc_ref[rows, :] += pv
            later_ref[rows, :] += row_sum
            return inner

        lax.fori_loop(0, jnp.where(jnp.min(later_ref[rows, :]) < SB_SATURATED, n_far, 0), far_tile, 0)
        return carry

    if n_tiles > SB_NEAR_TILES:
        any_unsaturated = jnp.min(functools.reduce(jnp.minimum, later[SB_NEAR_TILES:])) < SB_SATURATED
        lax.fori_loop(SB_NEAR_TILES, jnp.where(any_unsaturated, n_tiles, SB_NEAR_TILES), far_tiles_of, 0)
    o_ref[...] = acc_ref[...].astype(o_ref.dtype)


def _sb_attention(proj, *, batch, seq):
    spec = lambda blk: pl.BlockSpec((None, seq, HEAD_DIM), lambda b, h: (b, 0, blk + h))
    return pl.pallas_call(
        _sb_kernel,
        grid=(batch, SB_HEADS),
        in_specs=[spec(SB_Q_BLK), spec(SB_K_BLK), spec(SB_V_BLK)],
        out_specs=spec(0),
        out_shape=jax.ShapeDtypeStruct((batch, seq, SB_HEADS * HEAD_DIM), BF16),
        scratch_shapes=[pltpu.VMEM((seq, HEAD_DIM), F32), pltpu.VMEM((seq, 1), F32)],
        compiler_params=_params(2),
        name="sb_attention",
    )(proj, proj, proj)


def _t5_bias_kernel(rel_ref, o_ref):
    t = ATT_TILE
    h = pl.program_id(0)
    row = lax.broadcasted_iota(jnp.int32, (t, t), 0)
    col = lax.broadcasted_iota(jnp.int32, (t, t), 1)
    max_exact = NUM_BUCKETS // 2
    for d in range(3):
        n = jnp.maximum(d * t + row - col, 0)
        nf = jnp.maximum(n, 1).astype(F32)
        large = max_exact + (jnp.log(nf / max_exact) / math.log(MAX_DISTANCE / max_exact)
                             * (NUM_BUCKETS - max_exact)).astype(jnp.int32)
        large = jnp.minimum(large, NUM_BUCKETS - 1)
        bucket = jnp.where(n < max_exact, n, large)
        bias = jnp.zeros((t, t), F32)
        for b in range(NUM_BUCKETS):
            bias = jnp.where(bucket == b, rel_ref[b, h], bias)
        o_ref[d] = bias * LOG2E


def _t5_bias_tiles(rel_bias):
    t = ATT_TILE
    return pl.pallas_call(
        _t5_bias_kernel,
        grid=(DIFF_HEADS,),
        in_specs=[pl.BlockSpec(memory_space=pltpu.SMEM)],
        out_specs=pl.BlockSpec((None, 3, t, t), lambda h: (h, 0, 0, 0)),
        out_shape=jax.ShapeDtypeStruct((DIFF_HEADS, 3, t, t), F32),
        compiler_params=_params(1),
        name="t5_bias_tiles",
    )(rel_bias)


def _diff_kernel(lq1_ref, lk1_ref, lq2_ref, lk2_ref, subln_g_ref, bias_ref, q_ref, k_ref, v_ref, o_ref,
                 *, lambda_init):
    t = ATT_TILE
    seq = q_ref.shape[0]
    n_tiles = seq // t
    lane = lax.broadcasted_iota(jnp.int32, (t, HEAD_DIM), 1)
    row = lax.broadcasted_iota(jnp.int32, (t, t), 0)
    col = lax.broadcasted_iota(jnp.int32, (t, t), 1)
    causal = col <= row
    lam = (jnp.exp(jnp.sum(lq1_ref[...] * lk1_ref[...], axis=1, keepdims=True))
           - jnp.exp(jnp.sum(lq2_ref[...] * lk2_ref[...], axis=1, keepdims=True)) + lambda_init)

    def logits_of(i):
        q = q_ref[i * t:(i + 1) * t, :]
        zero = jnp.zeros_like(q)
        q_maps = (jnp.where(lane < DIFF_QK_DIM, q, zero), jnp.where(lane >= DIFF_QK_DIM, q, zero))
        k = k_ref[0:(i + 1) * t, :]
        return [lax.dot_general(qm, k, (((1,), (1,)), ((), ())), preferred_element_type=F32) for qm in q_maps]

    raws = {0: logits_of(0)}
    for i in range(n_tiles):
        n_keys = (i + 1) * t
        if i + 1 < n_tiles:
            raws[i + 1] = logits_of(i + 1)
        exps, sums = [], []
        for raw in raws.pop(i):
            tiles = [raw[:, j * t:(j + 1) * t] for j in range(i + 1)]
            biases = [bias_ref[min(i - j, 2)] for j in range(i + 1)]
            tops = [jnp.where(causal, r + b, NEG_INF) if j == i else r + b
                    for j, (r, b) in enumerate(zip(tiles, biases))]
            row_max = jnp.max(functools.reduce(jnp.maximum, tops), axis=1, keepdims=True)
            parts = []
            for j, (r, b) in enumerate(zip(tiles, biases)):
                e_j = jnp.exp2((r - row_max) + b)
                parts.append(jnp.where(causal, e_j, 0.0) if j == i else e_j)
            e = parts[0] if i == 0 else jnp.concatenate(parts, axis=1)
            exps.append(e)
            sums.append(jnp.sum(functools.reduce(lambda a, c: a + c, parts), axis=1, keepdims=True))
        p = exps[0] - exps[1] * (lam * sums[0] / sums[1])
        o = jnp.dot(p.astype(BF16), v_ref[0:n_keys, :], preferred_element_type=F32) * (1.0 / sums[0])
        o_ref[i * t:(i + 1) * t, :] = (_rms(o, subln_g_ref[...]) * (1.0 - lambda_init)).astype(o_ref.dtype)


def _diff_attention(proj, bias_tiles, layer, lq1, lk1, lq2, lk2, subln_g, *, batch, seq, lambda_init):
    t = ATT_TILE
    vec = lambda n: _layer_vec_spec(layer, n)
    spec = lambda blk: pl.BlockSpec((None, seq, HEAD_DIM), lambda b, h: (b, 0, blk + h))
    kern = functools.partial(_diff_kernel, lambda_init=lambda_init)
    return pl.pallas_call(
        kern,
        grid=(batch, DIFF_HEADS),
        in_specs=[
            vec(DIFF_QK_DIM), vec(DIFF_QK_DIM), vec(DIFF_QK_DIM), vec(DIFF_QK_DIM), vec(HEAD_DIM),
            pl.BlockSpec((None, 3, t, t), lambda b, h: (h, 0, 0, 0)),
            spec(DF_Q_BLK), spec(DF_K_BLK), spec(DF_V_BLK),
        ],
        out_specs=spec(0),
        out_shape=jax.ShapeDtypeStruct((batch, seq, DIFF_HEADS * HEAD_DIM), BF16),
        compiler_params=_params(2),
        name="diff_attention",
    )(_as_rows(lq1), _as_rows(lk1), _as_rows(lq2), _as_rows(lk2), _as_rows(subln_g), bias_tiles, proj, proj, proj)


def _dot_nt_split(a, b):
    nt = lambda x, y: lax.dot_general(x, y, (((1,), (1,)), ((), ())), preferred_element_type=F32)
    a_hi, a_lo = _split2_bf16(a)
    b_hi, b_lo = _split2_bf16(b)
    return nt(a_hi, b_hi) + (nt(a_hi, b_lo) + nt(a_lo, b_hi))


def _cpow(n, ar, ai, dt):
    mag = jnp.exp(n * (ar * dt))
    ang = n * (ai * dt)
    return mag * jnp.cos(ang), mag * jnp.sin(ang)


def _repeat_rows(x, reps):
    n, w = x.shape
    return jnp.broadcast_to(x[:, None, :], (n, reps, w)).reshape(n * reps, w)


def _ssm_prep_group(ar_ref, ai_ref, ldt_ref, bt_re_ref, bt_im_ref, ct_re_ref, ct_im_ref,
                    m_intra_ref, m_in_ref, m_out_ref, p1_ref, p2_ref):
    t = SSM_CHUNK
    cpg = SSM_CH_PER_GROUP
    ar, ai = ar_ref[...], ai_ref[...]
    dt = jnp.exp(ldt_ref[...])
    lb_re, lb_im = _cpow(1.0, ar, ai, dt)
    den = ar * ar + ai * ai
    f_re = ((lb_re - 1.0) * ar + lb_im * ai) / den
    f_im = (lb_im * ar - (lb_re - 1.0) * ai) / den
    bt_re, bt_im = bt_re_ref[...], bt_im_ref[...]
    bb_re = f_re * bt_re - f_im * bt_im
    bb_im = f_re * bt_im + f_im * bt_re
    ct_re, ct_im = ct_re_ref[...], ct_im_ref[...]
    pos = lax.broadcasted_iota(jnp.int32, (t, 2 * SSM_STATE), 0).astype(F32)
    lane_is_re = lax.broadcasted_iota(jnp.int32, (t * cpg, 2 * SSM_STATE), 1) < SSM_STATE

    def times_power(n, x_re, x_im, im_sign):
        e_re, e_im = _cpow(n, ar, ai, dt)
        e_re, e_im = _repeat_rows(e_re, cpg), _repeat_rows(e_im, cpg)
        return jnp.where(lane_is_re, e_re * x_re - e_im * x_im, im_sign * (e_re * x_im + e_im * x_re))

    m_in_ref[...] = times_power((t - 1.0) - pos, bb_re, bb_im, 1.0).astype(BF16)
    m_out_ref[...] = times_power(pos + 1.0, ct_re, ct_im, -1.0).astype(BF16)
    p_packed = times_power(-pos, bb_re, bb_im, -1.0)
    q_packed = times_power(pos, ct_re, ct_im, 1.0)
    k_full = _dot_nt_split(p_packed, q_packed)
    shift = cpg.bit_length() - 1
    row_k = lax.broadcasted_iota(jnp.int32, k_full.shape, 0)
    lane_k = lax.broadcasted_iota(jnp.int32, k_full.shape, 1)
    causal = jnp.right_shift(lane_k, shift) >= jnp.right_shift(row_k, shift)
    m_intra_ref[...] = jnp.where(causal, k_full, 0.0).astype(BF16)

    row_j = lax.broadcasted_iota(jnp.int32, (SSM_SCAN_ROWS, 2 * SSM_STATE), 0)
    lane_j = lax.broadcasted_iota(jnp.int32, (SSM_SCAN_ROWS, 2 * SSM_STATE), 1)
    pr, pi = _cpow(jnp.left_shift(t, row_j).astype(F32), ar, ai, dt)
    p1_ref[...] = pr
    p2_ref[...] = jnp.where(lane_j < SSM_STATE, -pi, pi)


def _ssm_prep_kernel(*refs):
    for g in range(refs[0].shape[0]):
        _ssm_prep_group(*[r.at[g] for r in refs])


def _ssm_prep(a_re, a_im, log_dt, b_re, b_im, c_re, c_im):
    nl, g, p = a_re.shape
    t, cpg = SSM_CHUNK, SSM_CH_PER_GROUP
    tc = t * cpg
    twice = lambda x: jnp.concatenate([x, x], axis=-1)
    rows = [twice(x)[:, :, None, :] for x in (a_re, a_im, jnp.broadcast_to(log_dt[:, :, None], a_re.shape))]
    bt = [jnp.tile(twice(jnp.swapaxes(b, 2, 3)), (1, 1, t, 1)) for b in (b_re, b_im)]
    ct = [jnp.tile(twice(c), (1, 1, t, 1)) for c in (c_re, c_im)]
    gps = SSM_PREP_GROUPS
    blk = lambda shape: pl.BlockSpec((None, gps) + shape, lambda l, gi: (l, gi, 0, 0))
    return pl.pallas_call(
        _ssm_prep_kernel,
        grid=(nl, g // gps),
        in_specs=[blk((1, 2 * p))] * 3 + [blk((tc, 2 * p))] * 4,
        out_specs=[blk((tc, tc)), blk((tc, 2 * p)), blk((tc, 2 * p)),
                   blk((SSM_SCAN_ROWS, 2 * p)), blk((SSM_SCAN_ROWS, 2 * p))],
        out_shape=[jax.ShapeDtypeStruct((nl, g, tc, tc), BF16),
                   jax.ShapeDtypeStruct((nl, g, tc, 2 * p), BF16),
                   jax.ShapeDtypeStruct((nl, g, tc, 2 * p), BF16),
                   jax.ShapeDtypeStruct((nl, g, SSM_SCAN_ROWS, 2 * p), F32),
                   jax.ShapeDtypeStruct((nl, g, SSM_SCAN_ROWS, 2 * p), F32)],
        compiler_params=_params(2),
        name="ssm_prep",
    )(*rows, *bt, *ct)


def _gelu_tanh(x):
    return 0.5 * x * (1.0 + jnp.tanh(math.sqrt(2.0 / math.pi) * (x + 0.044715 * (x * x * x))))


def _transpose_lane_pieces(vs):
    p = len(vs)
    width = vs[0].shape[1]
    w = width // p
    piece = lax.broadcasted_iota(jnp.int32, vs[0].shape, 1) // w
    vs = list(vs)
    s = p // 2
    while s >= 1:
        keep = jnp.bitwise_and(piece, s) == 0
        for i in range(p):
            if i & s:
                continue
            lo, hi = vs[i], vs[i + s]
            vs[i] = jnp.where(keep, lo, pltpu.roll(hi, s * w, axis=1))
            vs[i + s] = jnp.where(keep, pltpu.roll(lo, width - s * w, axis=1), hi)
        s //= 2
    return vs


def _ssm_kernel(u_ref, m_intra_ref, m_in_ref, m_out_ref, p1_ref, p2_ref, d_ref, o_ref, *, n_chunks):
    t, cpg = SSM_CHUNK, SSM_CH_PER_GROUP
    n_rows = u_ref.shape[0] // t
    groups = u_ref.shape[1] // cpg
    by_pos = [u_ref[pl.ds(tau, n_rows, stride=t), :] for tau in range(t)]
    by_group = [_transpose_lane_pieces(by_pos[h:h + groups]) for h in range(0, t, groups)]
    row = lax.broadcasted_iota(jnp.int32, (n_rows, 2 * SSM_STATE), 0)
    chunk = jnp.bitwise_and(row, n_chunks - 1)

    def chunks_back(x, n):
        return jnp.where(chunk >= n, pltpu.roll(x, n, axis=0), 0.0)

    def regroup(g):
        return jnp.concatenate([half[g] for half in by_group], axis=1)

    ys = []
    us = {0: regroup(0)}
    for g in range(groups):
        if g + 1 < groups:
            us[g + 1] = regroup(g + 1)
        u = us.pop(g)
        ub = u.astype(BF16)
        y = jnp.dot(ub, m_intra_ref[g], preferred_element_type=F32)
        s_in = jnp.dot(ub, m_in_ref[g], preferred_element_type=F32)
        x = chunks_back(s_in, 1)
        for j in range((n_chunks - 1).bit_length()):
            prev = chunks_back(x, 1 << j)
            x = x + p1_ref[g, j:j + 1, :] * prev + p2_ref[g, j:j + 1, :] * pltpu.roll(prev, SSM_STATE, axis=1)
        y_state = lax.dot_general(x.astype(BF16), m_out_ref[g], (((1,), (1,)), ((), ())),
                                  preferred_element_type=F32)
        ys.append(_gelu_tanh(y + y_state + d_ref[g] * u))
    lanes_per_half = groups * cpg
    for h in range(t // groups):
        half = _transpose_lane_pieces([y[:, h * lanes_per_half:(h + 1) * lanes_per_half] for y in ys])
        for i, rows_of_pos in enumerate(half):
            o_ref[pl.ds(h * groups + i, n_rows, stride=t), :] = rows_of_pos


def _ssm(u, layer, mats, d_skip, *, n_chunks):
    m, width = u.shape
    m_intra, m_in, m_out, p1, p2 = mats
    gps = V7X_LANES // SSM_CH_PER_GROUP
    lg = lambda x: pl.BlockSpec((None, gps) + x.shape[2:], lambda gi: (layer, gi, 0, 0))
    d_tiled = jnp.tile(d_skip, (1, 1, SSM_CHUNK))[:, :, None, :]
    kern = functools.partial(_ssm_kernel, n_chunks=n_chunks)
    return pl.pallas_call(
        kern,
        grid=(width // V7X_LANES,),
        in_specs=[pl.BlockSpec((m, V7X_LANES), lambda gi: (0, gi)),
                  lg(m_intra), lg(m_in), lg(m_out), lg(p1), lg(p2), lg(d_tiled)],
        out_specs=pl.BlockSpec((m, V7X_LANES), lambda gi: (0, gi)),
        out_shape=jax.ShapeDtypeStruct(u.shape, F32),
        compiler_params=_params(1),
        name="ssm_chunks",
    )(u, m_intra, m_in, m_out, p1, p2, d_tiled)


def _outproj_kernel(h_ref, sb_ref, ssm_ref, df_ref, wglu_ref, bglu_ref, w_f32_ref, g_ref, o_ref, mixed_ref, w_ref):
    w_sb, w_ssm = sb_ref.shape[1], ssm_ref.shape[1]

    @pl.when(pl.program_id(0) == 0)
    def _():
        def cast_rows(r0):
            rows = pl.ds(r0, ROW_CHUNK)
            w_ref[rows, :] = w_f32_ref[rows, :].astype(BF16)
        _for_row_chunks(w_ref.shape[0], ROW_CHUNK, cast_rows)

    y = ssm_ref[...]
    gate = jax.nn.sigmoid(jnp.dot(y.astype(BF16), wglu_ref[...].astype(BF16), preferred_element_type=F32)
                          + bglu_ref[...])
    mixed_ref[:, 0:w_sb] = sb_ref[...]
    mixed_ref[:, w_sb:w_sb + w_ssm] = (y * gate).astype(BF16)
    mixed_ref[:, w_sb + w_ssm:] = df_ref[...]
    out = jnp.dot(mixed_ref[...], w_ref[...], preferred_element_type=F32)
    o_ref[...] = h_ref[...] + _rms(out, g_ref[...])


def _outproj(h, o_sb, y_ssm, o_diff, layer, w_glu, b_glu, w_out, g, *, tm):
    m, d = h.shape
    k = w_out.shape[1]
    row_block = lambda x: pl.BlockSpec((tm, x.shape[1]), lambda i: (i, 0))
    return pl.pallas_call(
        _outproj_kernel,
        grid=(m // tm,),
        in_specs=[
            row_block(h), row_block(o_sb), row_block(y_ssm), row_block(o_diff),
            pl.BlockSpec((None,) + w_glu.shape[1:], lambda i: (layer, 0, 0)),
            _layer_vec_spec(layer, SSM_WIDTH),
            pl.BlockSpec((None, k, d), lambda i: (layer, 0, 0), pipeline_mode=pl.Buffered(1)),
            _layer_vec_spec(layer, d),
        ],
        out_specs=row_block(h),
        out_shape=jax.ShapeDtypeStruct((m, d), F32),
        scratch_shapes=[pltpu.VMEM((tm, k), BF16), pltpu.VMEM((k, d), BF16)],
        compiler_params=_params(1),
        name="outproj",
    )(h, o_sb, y_ssm, o_diff, w_glu, _as_rows(b_glu), w_out, _as_rows(g))


def _trunk(x, ffn1_pre_g, ffn1_w_gate, ffn1_w_up, ffn1_w_down, ffn1_post_g, mix_pre_g, w_in, ssm_a_re, ssm_a_im,
           ssm_log_dt, ssm_b_re, ssm_b_im, ssm_c_re, ssm_c_im, ssm_d, ssm_w_glu, ssm_b_glu, diff_lq1, diff_lk1,
           diff_lq2, diff_lk2, diff_subln_g, rel_bias, w_out, mix_post_g, ffn2_pre_g, ffn2_w_gate, ffn2_w_up,
           ffn2_w_down, ffn2_post_g, *, tm, tf, tm_out):
    batch, seq, d = x.shape
    m = batch * seq
    depth = w_in.shape[0]
    h = x.reshape(m, d)
    bias_tiles = _t5_bias_tiles(rel_bias)
    ssm_mats = _ssm_prep(ssm_a_re, ssm_a_im, ssm_log_dt, ssm_b_re, ssm_b_im, ssm_c_re, ssm_c_im)
    for l in range(depth):
        h = _ffn(h, l, ffn1_pre_g, ffn1_w_gate, ffn1_w_up, ffn1_w_down, ffn1_post_g, tm=tm, tf=tf)

        proj, ssm_u = _inproj(h, l, mix_pre_g, w_in, tm=tm)
        proj = proj.reshape(batch, seq, PROJ_COLS)

        o_sb = _sb_attention(proj, batch=batch, seq=seq)

        y_ssm = _ssm(ssm_u, l, ssm_mats, ssm_d, n_chunks=seq // SSM_CHUNK)

        lambda_init = 0.8 - 0.6 * math.exp(-0.3 * l)
        o_diff = _diff_attention(proj, bias_tiles, l, diff_lq1, diff_lk1, diff_lq2, diff_lk2, diff_subln_g,
                                 batch=batch, seq=seq, lambda_init=lambda_init)

        h = _outproj(h, o_sb.reshape(m, -1), y_ssm, o_diff.reshape(m, -1), l, ssm_w_glu, ssm_b_glu, w_out,
                     mix_post_g, tm=tm_out)

        h = _ffn(h, l, ffn2_pre_g, ffn2_w_gate, ffn2_w_up, ffn2_w_down, ffn2_post_g, tm=tm, tf=tf)
    return h.reshape(batch, seq, d)


def kernel(x, ffn1_pre_g, ffn1_w_gate, ffn1_w_up, ffn1_w_down, ffn1_post_g, mix_pre_g, w_in, ssm_a_re, ssm_a_im, ssm_log_dt, ssm_b_re, ssm_b_im, ssm_c_re, ssm_c_im, ssm_d, ssm_w_glu, ssm_b_glu, diff_lq1, diff_lk1, diff_lq2, diff_lk2, diff_subln_g, rel_bias, w_out, mix_post_g, ffn2_pre_g, ffn2_w_gate, ffn2_w_up, ffn2_w_down, ffn2_post_g):
    n_chunks = x.shape[1] // SSM_CHUNK
    assert x.shape[1] % ATT_TILE == 0 and n_chunks <= 1 << SSM_SCAN_ROWS and n_chunks & (n_chunks - 1) == 0
    return _trunk(x, ffn1_pre_g, ffn1_w_gate, ffn1_w_up, ffn1_w_down, ffn1_post_g, mix_pre_g, w_in, ssm_a_re,
                  ssm_a_im, ssm_log_dt, ssm_b_re, ssm_b_im, ssm_c_re, ssm_c_im, ssm_d, ssm_w_glu, ssm_b_glu,
                  diff_lq1, diff_lk1, diff_lq2, diff_lk2, diff_subln_g, rel_bias, w_out, mix_post_g, ffn2_pre_g,
                  ffn2_w_gate, ffn2_w_up, ffn2_w_down, ffn2_post_g, tm=ROW_TILE, tf=FFN_TILE, tm_out=OUTPROJ_ROW_TILE)
```
